```python
import math
import jax, jax.numpy as jnp
from jax import lax
import numpy as np

D_MODEL = 1024
BATCH = 8
SEQ = 2048
DEPTH = 4
DEC_BATCH = 128
DEC_SEQ = 4
PAST_LEN = 8192
PAGE_SIZE = 128

N_HEADS = 8
N_KV_HEADS = 2
HEAD_DIM = 64
GQA_GROUP = N_HEADS // N_KV_HEADS
D_ATTN = N_HEADS * HEAD_DIM
D_CONV = D_MODEL - D_ATTN
D_MIX = D_ATTN + D_CONV
D_KV = N_KV_HEADS * HEAD_DIM
D_IN = D_ATTN + 2 * D_KV + 3 * D_CONV
CONV_WIDTH = 3
WINDOW = 128
BLOCK = 128
N_META = 16
N_BUCKETS = 32
MAX_DISTANCE = 128
D_FF = 2816
DN_ALPHA = (2 * DEPTH) ** 0.25
DN_BETA = (8 * DEPTH) ** -0.25
LN_EPS = 1e-5

kernel_name = "hymba_swa_shortconv_macaron_deepnorm_step"


def layer_norm(x, g, b):
    xf = x.astype(jnp.float32)
    mu = jnp.mean(xf, axis=-1, keepdims=True)
    var = jnp.mean(jnp.square(xf - mu), axis=-1, keepdims=True)
    y = (xf - mu) * lax.rsqrt(var + LN_EPS) * g.astype(jnp.float32) + b.astype(jnp.float32)
    return y.astype(x.dtype)


def swiglu(x, wg, wu, wd):
    return (jax.nn.silu(x @ wg) * (x @ wu)) @ wd


def t5_bucket(d):
    d = jnp.maximum(d, 0)
    max_exact = N_BUCKETS // 2
    df = jnp.maximum(d, 1).astype(jnp.float32)
    large = max_exact + (jnp.log(df / max_exact) / math.log(MAX_DISTANCE / max_exact)
                         * (N_BUCKETS - max_exact)).astype(jnp.int32)
    large = jnp.minimum(large, N_BUCKETS - 1)
    return jnp.where(d < max_exact, d, large)


def rel_bias_heads(d, rel_tab):
    b = rel_tab[t5_bucket(d)].astype(jnp.float32)
    b = jnp.moveaxis(b, -1, 0)
    return b.reshape((N_KV_HEADS, GQA_GROUP) + d.shape)


def sink_softmax(s, sink):
    sk = sink.astype(jnp.float32).reshape(N_KV_HEADS, GQA_GROUP, 1, 1)
    m = jnp.maximum(jnp.max(s, axis=-1, keepdims=True), sk)
    p = jnp.exp(s - m)
    return p / (jnp.sum(p, axis=-1, keepdims=True) + jnp.exp(sk - m))


def project(x, w):
    z = x @ w
    cuts = [D_ATTN, D_ATTN + D_KV, D_ATTN + 2 * D_KV,
            D_ATTN + 2 * D_KV + D_CONV, D_ATTN + 2 * D_KV + 2 * D_CONV]
    q, k, v, bg, cg, h = jnp.split(z, cuts, axis=-1)
    lead = x.shape[:-1]
    q = q.reshape(lead + (N_HEADS, HEAD_DIM))
    k = k.reshape(lead + (N_KV_HEADS, HEAD_DIM))
    v = v.reshape(lead + (N_KV_HEADS, HEAD_DIM))
    return q, k, v, bg, cg, h


def swa_prompt(q, k, v, sink, rel_tab):
    B, L = q.shape[0], q.shape[1]
    pad = (-L) % BLOCK
    nb = (L + pad) // BLOCK
    padf = lambda t: jnp.pad(t, ((0, 0), (pad, 0), (0, 0), (0, 0)))
    qb = padf(q).reshape(B, nb, BLOCK, N_KV_HEADS, GQA_GROUP, HEAD_DIM)
    kb = padf(k).reshape(B, nb, BLOCK, N_KV_HEADS, HEAD_DIM)
    vb = padf(v).reshape(B, nb, BLOCK, N_KV_HEADS, HEAD_DIM)

    def band(t):
        prev = jnp.pad(t, ((0, 0), (1, 0), (0, 0), (0, 0), (0, 0)))[:, :nb]
        return jnp.concatenate([prev, t], axis=2)

    kk, vv = band(kb), band(vb)
    s = jnp.einsum('bnqkgd,bnskd->bnkgqs', qb, kk).astype(jnp.float32) * (HEAD_DIM ** -0.5)
    qi = jnp.arange(BLOCK)
    sj = jnp.arange(2 * BLOCK)
    d = qi[:, None] + BLOCK - sj[None, :]
    key_idx = (jnp.arange(nb)[:, None] - 1) * BLOCK + sj[None, :]
    valid = ((d >= 0) & (d <= WINDOW))[None] & (key_idx >= pad)[:, None, :]
    s = jnp.where(valid[None, :, None, None], s + rel_bias_heads(d, rel_tab), -jnp.inf)
    p = sink_softmax(s, sink).astype(vv.dtype)
    o = jnp.einsum('bnkgqs,bnskd->bnqkgd', p, vv).reshape(B, nb * BLOCK, D_ATTN)[:, pad:]
    return o, k[:, -WINDOW:], v[:, -WINDOW:]


def swa_sample(q, k, v, k_buf, v_buf, sink, rel_tab):
    B, T = q.shape[0], q.shape[1]
    W = k_buf.shape[1]
    kk = jnp.concatenate([k_buf.astype(k.dtype), k], axis=1)
    vv = jnp.concatenate([v_buf.astype(v.dtype), v], axis=1)
    qg = q.reshape(B, T, N_KV_HEADS, GQA_GROUP, HEAD_DIM)
    s = jnp.einsum('btkgd,bskd->bkgts', qg, kk).astype(jnp.float32) * (HEAD_DIM ** -0.5)
    d = jnp.arange(T)[:, None] + W - jnp.arange(W + T)[None, :]
    valid = (d >= 0) & (d <= WINDOW)
    s = jnp.where(valid, s + rel_bias_heads(d, rel_tab), -jnp.inf)
    p = sink_softmax(s, sink).astype(vv.dtype)
    o = jnp.einsum('bkgts,bskd->btkgd', p, vv).reshape(B, T, D_ATTN)
    return o, kk[:, -WINDOW:], vv[:, -WINDOW:]


def short_conv(u, prev, w):
    T = u.shape[1]
    full = jnp.concatenate([prev.astype(u.dtype), u], axis=1)
    out = sum(w[j] * full[:, j:j + T] for j in range(CONV_WIDTH))
    return out, full[:, -(CONV_WIDTH - 1):]


def token_mixer(x, attend, conv_prev, w_in_l, conv_w_l, w_out_l):
    q, k, v, bg, cg, h = project(x, w_in_l)
    a, k_state, v_state = attend(q, k, v)
    c, conv_state = short_conv(cg * h, conv_prev, conv_w_l)
    y = jnp.concatenate([a, bg * c], axis=-1) @ w_out_l
    return y, k_state, v_state, conv_state


def trunk_layer(x, attend, conv_prev, w_in_l, conv_w_l, w_out_l, wg_l, wu_l, wd_l, g_l, b_l):
    x = layer_norm(DN_ALPHA * x + 0.5 * swiglu(x, wg_l[0], wu_l[0], wd_l[0]), g_l[0], b_l[0])
    y, ks, vs, cs = token_mixer(x, attend, conv_prev, w_in_l, conv_w_l, w_out_l)
    x = layer_norm(DN_ALPHA * x + y, g_l[1], b_l[1])
    x = layer_norm(DN_ALPHA * x + 0.5 * swiglu(x, wg_l[1], wu_l[1], wd_l[1]), g_l[2], b_l[2])
    return x, ks, vs, cs


def setup_inputs(seed: int = 0) -> dict:
    key = jax.random.key(seed)
    ks = jax.random.split(key, 16)
    nrm = lambda k, shape: jax.random.normal(k, shape, jnp.float32)
    x_prompt = nrm(ks[0], (BATCH, SEQ, D_MODEL))
    x_sample = nrm(ks[1], (DEC_BATCH, DEC_SEQ, D_MODEL))
    cache_k = nrm(ks[2], (DEPTH, DEC_BATCH, WINDOW, N_KV_HEADS, HEAD_DIM))
    cache_v = nrm(ks[3], (DEPTH, DEC_BATCH, WINDOW, N_KV_HEADS, HEAD_DIM)) * DN_BETA
    state_conv = nrm(ks[4], (DEPTH, DEC_BATCH, CONV_WIDTH - 1, D_CONV)) * DN_BETA
    meta_tokens = nrm(ks[5], (N_META, D_MODEL))
    rel_bias = 0.5 * nrm(ks[6], (N_BUCKETS, N_HEADS))
    col_scale = jnp.concatenate([
        jnp.ones((D_ATTN + D_KV,), jnp.float32),
        jnp.full((D_KV,), DN_BETA, jnp.float32),
        jnp.ones((2 * D_CONV,), jnp.float32),
        jnp.full((D_CONV,), DN_BETA, jnp.float32)])
    w_in = nrm(ks[7], (DEPTH, D_MODEL, D_IN)) * (D_MODEL ** -0.5) * col_scale
    conv_w = nrm(ks[8], (DEPTH, CONV_WIDTH, D_CONV)) * (CONV_WIDTH ** -0.5)
    attn_sink = nrm(ks[9], (DEPTH, N_HEADS))
    w_out = nrm(ks[10], (DEPTH, D_MIX, D_MODEL)) * (D_MIX ** -0.5) * DN_BETA
    ffn_w_gate = nrm(ks[11], (DEPTH, 2, D_MODEL, D_FF)) * (D_MODEL ** -0.5)
    ffn_w_up = nrm(ks[12], (DEPTH, 2, D_MODEL, D_FF)) * (D_MODEL ** -0.5) * DN_BETA
    ffn_w_down = nrm(ks[13], (DEPTH, 2, D_FF, D_MODEL)) * (D_FF ** -0.5) * DN_BETA
    ln_g = 1.0 + 0.02 * nrm(ks[14], (DEPTH, 3, D_MODEL))
    ln_b = 0.02 * nrm(ks[15], (DEPTH, 3, D_MODEL))
    return {"x_prompt": x_prompt, "x_sample": x_sample, "cache_k": cache_k,
            "cache_v": cache_v, "state_conv": state_conv, "meta_tokens": meta_tokens,
            "rel_bias": rel_bias, "w_in": w_in, "conv_w": conv_w, "attn_sink": attn_sink,
            "w_out": w_out, "ffn_w_gate": ffn_w_gate, "ffn_w_up": ffn_w_up,
            "ffn_w_down": ffn_w_down, "ln_g": ln_g, "ln_b": ln_b}


def reference(x_prompt, x_sample, cache_k, cache_v, state_conv, meta_tokens, rel_bias,
              w_in, conv_w, attn_sink, w_out, ffn_w_gate, ffn_w_up, ffn_w_down, ln_g, ln_b):
    Bp = x_prompt.shape[0]
    meta = jnp.broadcast_to(meta_tokens.astype(x_prompt.dtype)[None], (Bp, N_META, D_MODEL))
    xp = jnp.concatenate([meta, x_prompt], axis=1)
    xs = x_sample
    conv_zero = jnp.zeros((Bp, CONV_WIDTH - 1, D_CONV), xp.dtype)

    kp_l, vp_l, cp_l, ks_l, vs_l, cs_l = [], [], [], [], [], []
    for l in range(DEPTH):
        def attend_p(q, k, v, l=l):
            return swa_prompt(q, k, v, attn_sink[l], rel_bias)

        def attend_s(q, k, v, l=l):
            return swa_sample(q, k, v, cache_k[l], cache_v[l], attn_sink[l], rel_bias)

        xp, kp, vp, cp = trunk_layer(xp, attend_p, conv_zero, w_in[l], conv_w[l], w_out[l],
                                     ffn_w_gate[l], ffn_w_up[l], ffn_w_down[l], ln_g[l], ln_b[l])
        xs, kss, vss, css = trunk_layer(xs, attend_s, state_conv[l], w_in[l], conv_w[l], w_out[l],
                                        ffn_w_gate[l], ffn_w_up[l], ffn_w_down[l], ln_g[l], ln_b[l])
        kp_l.append(kp); vp_l.append(vp); cp_l.append(cp)
        ks_l.append(kss); vs_l.append(vss); cs_l.append(css)

    y_prompt = xp[:, N_META:]
    y_sample = xs
    new_k_prompt = jnp.stack(kp_l, axis=0)
    new_v_prompt = jnp.stack(vp_l, axis=0)
    new_conv_prompt = jnp.stack(cp_l, axis=0)
    new_k_sample = jnp.stack(ks_l, axis=0)
    new_v_sample = jnp.stack(vs_l, axis=0)
    new_conv_sample = jnp.stack(cs_l, axis=0)
    return (y_prompt, y_sample, new_k_prompt, new_v_prompt, new_conv_prompt,
            new_k_sample, new_v_sample, new_conv_sample)
```

```python
import functools
import math

import jax
import jax.numpy as jnp
from jax import lax
from jax.experimental import pallas as pl
from jax.experimental.pallas import tpu as pltpu

F32 = jnp.float32
BF16 = jnp.bfloat16

D_MODEL = 1024
N_HEADS = 8
N_KV_HEADS = 2
HEAD_DIM = 64
GQA_GROUP = N_HEADS // N_KV_HEADS
D_ATTN = N_HEADS * HEAD_DIM
D_CONV = D_MODEL - D_ATTN
D_KV = N_KV_HEADS * HEAD_DIM
D_IN = D_ATTN + 2 * D_KV + 3 * D_CONV
D_FF = 2816
CONV_WIDTH = 3
WINDOW = 128
BLOCK = 128
N_META = 16
N_BUCKETS = 32
MAX_DISTANCE = 128
LN_EPS = 1e-5
Q_SCALE = HEAD_DIM ** -0.5

V7X_VMEM_LIMIT_BYTES = 56 * 1024 * 1024
SUBLANES = 8
LANES = 128

FFN_ROWS = 896
FFN_CHUNK = 1408
MIX_ROWS = 512
SAMPLE_SEQS = 16
META_PAD = BLOCK - N_META

NEG_INF = float("-inf")


def _const_spec(shape):
    nd = len(shape)
    return pl.BlockSpec(shape, lambda *_: (0,) * nd, pipeline_mode=pl.Buffered(1))


def _params(*sem):
    return pltpu.CompilerParams(dimension_semantics=sem,
                                vmem_limit_bytes=V7X_VMEM_LIMIT_BYTES)


def _layer_norm(r, g, b):
    mu = jnp.mean(r, axis=-1, keepdims=True)
    rc = r - mu
    var = jnp.mean(rc * rc, axis=-1, keepdims=True)
    return rc * lax.rsqrt(var + LN_EPS) * g + b


def _t5_bucket(d):
    d = jnp.maximum(d, 0)
    max_exact = N_BUCKETS // 2
    df = jnp.maximum(d, 1).astype(F32)
    large = max_exact + (jnp.log(df / max_exact) / math.log(MAX_DISTANCE / max_exact)
                         * (N_BUCKETS - max_exact)).astype(jnp.int32)
    large = jnp.minimum(large, N_BUCKETS - 1)
    return jnp.where(d < max_exact, d, large)


def _bias_kernel(tab_ref, bp_ref, bs_ref, *, n_new):
    qi = lax.broadcasted_iota(jnp.int32, (BLOCK, 2 * BLOCK), 0)
    sj = lax.broadcasted_iota(jnp.int32, (BLOCK, 2 * BLOCK), 1)
    d = qi + BLOCK - sj
    valid = (d >= 0) & (d <= WINDOW)
    bk = _t5_bucket(d)
    for h in range(N_HEADS):
        acc = jnp.zeros(d.shape, F32)
        for b in range(N_BUCKETS):
            acc = jnp.where(bk == b, tab_ref[b, h], acc)
        bp_ref[h] = jnp.where(valid, acc, NEG_INF)
    rows, cols = bs_ref.shape
    r = lax.broadcasted_iota(jnp.int32, (rows, cols), 0)
    s = lax.broadcasted_iota(jnp.int32, (rows, cols), 1)
    ds = r // N_HEADS + WINDOW - s
    hs = r % N_HEADS
    valid_s = (ds >= 0) & (ds <= WINDOW) & (s < WINDOW + n_new)
    bks = _t5_bucket(ds)
    acc = jnp.zeros((rows, cols), F32)
    for h in range(N_HEADS):
        for b in range(N_BUCKETS):
            acc = jnp.where((bks == b) & (hs == h), tab_ref[b, h], acc)
    bs_ref[...] = jnp.where(valid_s, acc, NEG_INF)


def _bias_tables(rel_bias, n_new):
    rows = n_new * N_HEADS
    return pl.pallas_call(
        functools.partial(_bias_kernel, n_new=n_new),
        out_shape=(jax.ShapeDtypeStruct((N_HEADS, BLOCK, 2 * BLOCK), F32),
                   jax.ShapeDtypeStruct((rows, WINDOW + SUBLANES), F32)),
        in_specs=[pl.BlockSpec(memory_space=pltpu.SMEM)],
        name="bias_tables",
    )(rel_bias)


def _ffn_kernel(x_ref, wg_ref, wu_ref, wd_ref, g_ref, b_ref, o_ref, *, alpha):
    x = x_ref[...]
    xb = x.astype(BF16)
    y = None
    for c in range(0, D_FF, FFN_CHUNK):
        gate = jnp.dot(xb, wg_ref[:, c:c + FFN_CHUNK], preferred_element_type=F32)
        up = jnp.dot(xb, wu_ref[:, c:c + FFN_CHUNK], preferred_element_type=F32)
        act = (jax.nn.silu(gate) * up).astype(BF16)
        part = jnp.dot(act, wd_ref[c:c + FFN_CHUNK, :], preferred_element_type=F32)
        y = part if y is None else y + part
    o_ref[...] = _layer_norm(alpha * x + 0.5 * y, g_ref[...], b_ref[...])


def _ffn(x_all, wg, wu, wd, g, b, alpha):
    rows = x_all.shape[0]
    assert rows % FFN_ROWS == 0 and D_FF % FFN_CHUNK == 0
    row_spec = pl.BlockSpec((FFN_ROWS, D_MODEL), lambda i: (i, 0))
    return pl.pallas_call(
        functools.partial(_ffn_kernel, alpha=alpha),
        out_shape=jax.ShapeDtypeStruct(x_all.shape, F32),
        grid=(rows // FFN_ROWS,),
        in_specs=[row_spec, _const_spec(wg.shape), _const_spec(wu.shape), _const_spec(wd.shape),
                  _const_spec(g.shape), _const_spec(b.shape)],
        out_specs=row_spec,
        input_output_aliases={0: 0},
        compiler_params=_params("arbitrary"),
        name="ffn_ln",
    )(x_all, wg, wu, wd, g, b)


def _mixer_kernel(x_ref, win_ref, wout_ref, convw_ref, sink_ref, bias_ref, fmask_ref,
                  kinit_ref, vinit_ref, uinit_ref, g_ref, b_ref,
                  xo_ref, klast_ref, vlast_ref, ulast_ref,
                  kscr, vscr, uscr, ascr, *, rows, alpha, zero_rows):
    t = pl.program_id(1)

    @pl.when(t == 0)
    def _():
        kscr[0:BLOCK, :] = kinit_ref[0].astype(BF16)
        vscr[0:BLOCK, :] = vinit_ref[0].astype(BF16)
        uscr[0:SUBLANES, :] = uinit_ref[0]

    x = x_ref[...]
    if zero_rows:
        ridx = lax.broadcasted_iota(jnp.int32, x.shape, 0)
        x = jnp.where(ridx >= zero_rows, x, 0.0)
    z = jnp.dot(x.astype(BF16), win_ref[...], preferred_element_type=F32)
    k = z[:, D_ATTN:D_ATTN + D_KV]
    v = z[:, D_ATTN + D_KV:D_ATTN + 2 * D_KV]
    c0 = D_ATTN + 2 * D_KV
    u = z[:, c0 + D_CONV:c0 + 2 * D_CONV] * z[:, c0 + 2 * D_CONV:c0 + 3 * D_CONV]
    klast_ref[0] = k[rows - BLOCK:, :]
    vlast_ref[0] = v[rows - BLOCK:, :]
    ulast_ref[0] = u[rows - SUBLANES:, :]
    kscr[BLOCK:BLOCK + rows, :] = k.astype(BF16)
    vscr[BLOCK:BLOCK + rows, :] = v.astype(BF16)
    uscr[SUBLANES:SUBLANES + rows, :] = u
    qb = (z[:, :D_ATTN] * Q_SCALE).astype(BF16)

    first = jnp.where(t == 0, fmask_ref[...], 0.0)
    for j in range(rows // BLOCK):
        r0 = j * BLOCK
        kk = kscr[r0:r0 + 2 * BLOCK, :]
        vv = vscr[r0:r0 + 2 * BLOCK, :]
        for hp in range(N_HEADS // 2):
            pair = []
            for h in (2 * hp, 2 * hp + 1):
                kv = h // GQA_GROUP
                qh = qb[r0:r0 + BLOCK, h * HEAD_DIM:(h + 1) * HEAD_DIM]
                kh = kk[:, kv * HEAD_DIM:(kv + 1) * HEAD_DIM]
                vh = vv[:, kv * HEAD_DIM:(kv + 1) * HEAD_DIM]
                s = lax.dot_general(qh, kh, (((1,), (1,)), ((), ())),
                                    preferred_element_type=F32)
                s = s + bias_ref[h]
                if j == 0:
                    s = s + first
                sk = sink_ref[h]
                m = jnp.maximum(jnp.max(s, axis=-1, keepdims=True), sk)
                p = jnp.exp(s - m)
                den = jnp.sum(p, axis=-1, keepdims=True) + jnp.exp(sk - m)
                o = jnp.dot(p.astype(BF16), vh, preferred_element_type=F32)
                pair.append(o / den)
            ascr[r0:r0 + BLOCK, 2 * hp * HEAD_DIM:(2 * hp + 2) * HEAD_DIM] = (
                jnp.concatenate(pair, axis=1))

    um2 = uscr[SUBLANES - 2:SUBLANES - 2 + rows, :]
    um1 = uscr[SUBLANES - 1:SUBLANES - 1 + rows, :]
    conv = convw_ref[0:1, :] * um2 + convw_ref[1:2, :] * um1 + convw_ref[2:3, :] * u
    mix = z[:, c0:c0 + D_CONV] * conv
    y = (jnp.dot(ascr[...].astype(BF16), wout_ref[0:D_ATTN, :], preferred_element_type=F32)
         + jnp.dot(mix.astype(BF16), wout_ref[D_ATTN:, :], preferred_element_type=F32))
    xo_ref[...] = _layer_norm(alpha * x + y, g_ref[...], b_ref[...])

    kscr[0:BLOCK, :] = kscr[rows:rows + BLOCK, :]
    vscr[0:BLOCK, :] = vscr[rows:rows + BLOCK, :]
    uscr[0:SUBLANES, :] = uscr[rows:rows + SUBLANES, :]


def _mixer(x_all, row0, n_seq, seq_len, rows, zero_rows, win, wout, convw, sink, bias_p,
           fmask, kinit, vinit, uinit, g, b, alpha, name):
    assert seq_len % rows == 0 and rows % BLOCK == 0 and row0 % rows == 0
    steps = seq_len // rows
    blk0 = row0 // rows
    x_spec = pl.BlockSpec((rows, D_MODEL), lambda s, t: (blk0 + s * steps + t, 0))
    seq_spec = lambda shape: pl.BlockSpec((1,) + shape, lambda s, t: (s, 0, 0))
    kern = functools.partial(_mixer_kernel, rows=rows, alpha=alpha, zero_rows=zero_rows)
    return pl.pallas_call(
        kern,
        out_shape=(jax.ShapeDtypeStruct(x_all.shape, F32),
                   jax.ShapeDtypeStruct((n_seq, BLOCK, D_KV), F32),
                   jax.ShapeDtypeStruct((n_seq, BLOCK, D_KV), F32),
                   jax.ShapeDtypeStruct((n_seq, SUBLANES, D_CONV), F32)),
        grid=(n_seq, steps),
        in_specs=[x_spec, _const_spec(win.shape), _const_spec(wout.shape),
                  _const_spec(convw.shape),
                  pl.BlockSpec(memory_space=pltpu.SMEM),
                  _const_spec(bias_p.shape), _const_spec(fmask.shape),
                  _const_spec(kinit.shape), _const_spec(vinit.shape), _const_spec(uinit.shape),
                  _const_spec(g.shape), _const_spec(b.shape)],
        out_specs=(x_spec, seq_spec((BLOCK, D_KV)), seq_spec((BLOCK, D_KV)),
                   seq_spec((SUBLANES, D_CONV))),
        scratch_shapes=[pltpu.VMEM((BLOCK + rows, D_KV), BF16),
                        pltpu.VMEM((BLOCK + rows, D_KV), BF16),
                        pltpu.VMEM((SUBLANES + rows, D_CONV), F32),
                        pltpu.VMEM((rows, D_ATTN), F32)],
        input_output_aliases={0: 0},
        compiler_params=_params("arbitrary", "arbitrary"),
        name=name,
    )(x_all, win, wout, convw, sink, bias_p, fmask, kinit, vinit, uinit, g, b)


def _sample_proj_kernel(x_ref, w_ref, state_ref, convw_ref,
                        q_ref, k_ref, v_ref, mix_ref, u_ref, *, n_new):
    z = jnp.dot(x_ref[...].astype(BF16), w_ref[...], preferred_element_type=F32)
    qw = N_HEADS * LANES
    q_ref[...] = z[:, :qw] * Q_SCALE
    k_ref[...] = z[:, qw:qw + D_KV]
    v_ref[...] = z[:, qw + D_KV:qw + 2 * D_KV]
    c0 = qw + 2 * D_KV
    u = z[:, c0 + D_CONV:c0 + 2 * D_CONV] * z[:, c0 + 2 * D_CONV:c0 + 3 * D_CONV]
    u_ref[...] = u
    n = u.shape[0]
    tok = lax.broadcasted_iota(jnp.int32, u.shape, 0) % n_new
    st = state_ref[...]
    um2 = jnp.where(tok < 2, st, pltpu.roll(u, 2, 0))
    um1 = jnp.where(tok < 1, pltpu.roll(st, n - 1, 0), pltpu.roll(u, 1, 0))
    conv = convw_ref[0:1, :] * um2 + convw_ref[1:2, :] * um1 + convw_ref[2:3, :] * u
    mix_ref[...] = z[:, c0:c0 + D_CONV] * conv


def _sample_proj(x_all, row0, n, w_exp, state_rows, convw, n_new):
    assert row0 % n == 0
    return pl.pallas_call(
        functools.partial(_sample_proj_kernel, n_new=n_new),
        out_shape=(jax.ShapeDtypeStruct((n, N_HEADS * LANES), F32),
                   jax.ShapeDtypeStruct((n, D_KV), F32),
                   jax.ShapeDtypeStruct((n, D_KV), F32),
                   jax.ShapeDtypeStruct((n, D_CONV), F32),
                   jax.ShapeDtypeStruct((n, D_CONV), F32)),
        grid=(1,),
        in_specs=[pl.BlockSpec((n, D_MODEL), lambda i: (row0 // n, 0)),
                  _const_spec(w_exp.shape), _const_spec(state_rows.shape),
                  _const_spec(convw.shape)],
        out_specs=tuple(pl.BlockSpec((n, w), lambda i: (0, 0))
                        for w in (N_HEADS * LANES, D_KV, D_KV, D_CONV, D_CONV)),
        compiler_params=_params("arbitrary"),
        name="sample_proj",
    )(x_all, w_exp, state_rows, convw)


def _sample_attn_kernel(q_ref, ck_ref, cv_ref, kn_ref, vn_ref, bias_ref, sink_ref,
                        o_ref, ok_ref, ov_ref, *, n_new):
    qb = q_ref[...].astype(BF16)
    ck = ck_ref[...]
    cv = cv_ref[...]
    kn = kn_ref[...]
    vn = vn_ref[...]
    bias = bias_ref[...]
    sink = sink_ref[...]
    s_c = jnp.einsum("bqd,bkd->bqk", qb, ck.astype(BF16),
                     preferred_element_type=F32) + bias[None, :, :WINDOW]
    qf = qb.astype(F32)
    knf = kn.astype(BF16).astype(F32)
    vnf = vn.astype(BF16).astype(F32)
    s_n = [jnp.sum(qf * knf[:, i:i + 1, :], axis=-1, keepdims=True)
           + bias[None, :, WINDOW + i:WINDOW + i + 1] for i in range(n_new)]
    m = jnp.maximum(jnp.max(s_c, axis=-1, keepdims=True), sink[None])
    for s in s_n:
        m = jnp.maximum(m, s)
    p_c = jnp.exp(s_c - m)
    den = jnp.sum(p_c, axis=-1, keepdims=True) + jnp.exp(sink[None] - m)
    o = jnp.einsum("bqk,bkd->bqd", p_c.astype(BF16), cv.astype(BF16),
                   preferred_element_type=F32)
    for i, s in enumerate(s_n):
        p = jnp.exp(s - m)
        den = den + p
        o = o + p.astype(BF16).astype(F32) * vnf[:, i:i + 1, :]
    o_ref[...] = o / den
    ok_ref[:, 0:WINDOW - n_new, :] = ck_ref[:, n_new:WINDOW, :]
    ok_ref[:, WINDOW - n_new:WINDOW, :] = kn
    ov_ref[:, 0:WINDOW - n_new, :] = cv_ref[:, n_new:WINDOW, :]
    ov_ref[:, WINDOW - n_new:WINDOW, :] = vn


def _sample_attn(q_rows, ck, cv, kn, vn, bias_s, sink_rows, n_new):
    n_seq = ck.shape[0]
    assert n_seq % SAMPLE_SEQS == 0
    qr = n_new * N_HEADS
    seq_spec = lambda r, w: pl.BlockSpec((SAMPLE_SEQS, r, w), lambda i: (i, 0, 0))
    return pl.pallas_call(
        functools.partial(_sample_attn_kernel, n_new=n_new),
        out_shape=(jax.ShapeDtypeStruct((n_seq, qr, LANES), F32),
                   jax.ShapeDtypeStruct((n_seq, WINDOW, D_KV), F32),
                   jax.ShapeDtypeStruct((n_seq, WINDOW, D_KV), F32)),
        grid=(n_seq // SAMPLE_SEQS,),
        in_specs=[seq_spec(qr, LANES), seq_spec(WINDOW, D_KV), seq_spec(WINDOW, D_KV),
                  seq_spec(n_new, D_KV), seq_spec(n_new, D_KV),
                  _const_spec(bias_s.shape), _const_spec(sink_rows.shape)],
        out_specs=(seq_spec(qr, LANES), seq_spec(WINDOW, D_KV), seq_spec(WINDOW, D_KV)),
        compiler_params=_params("arbitrary"),
        name="sample_attn",
    )(q_rows, ck, cv, kn, vn, bias_s, sink_rows)


def _sample_out_kernel(x_ref, a_ref, mix_ref, woa_ref, woc_ref, g_ref, b_ref, xo_ref, *, alpha):
    y = (jnp.dot(a_ref[...].astype(BF16), woa_ref[...], preferred_element_type=F32)
         + jnp.dot(mix_ref[...].astype(BF16), woc_ref[...], preferred_element_type=F32))
    xo_ref[...] = _layer_norm(alpha * x_ref[...] + y, g_ref[...], b_ref[...])


def _sample_out(x_all, row0, n, a_exp, mix, woa_exp, woc, g, b, alpha):
    x_spec = pl.BlockSpec((n, D_MODEL), lambda i: (row0 // n, 0))
    return pl.pallas_call(
        functools.partial(_sample_out_kernel, alpha=alpha),
        out_shape=jax.ShapeDtypeStruct(x_all.shape, F32),
        grid=(1,),
        in_specs=[x_spec, _const_spec(a_exp.shape), _const_spec(mix.shape),
                  _const_spec(woa_exp.shape), _const_spec(woc.shape),
                  _const_spec(g.shape), _const_spec(b.shape)],
        out_specs=x_spec,
        input_output_aliases={0: 0},
        compiler_params=_params("arbitrary"),
        name="sample_out",
    )(x_all, a_exp, mix, woa_exp, woc, g, b)


def _expand_q_cols(wq):
    d = wq.shape[0]
    w = wq.reshape(d, N_KV_HEADS, GQA_GROUP, HEAD_DIM)
    slabs = []
    for kv in range(N_KV_HEADS):
        parts = [jnp.zeros_like(w[:, kv])] * N_KV_HEADS
        parts[kv] = w[:, kv]
        slabs.append(jnp.concatenate(parts, axis=-1))
    return jnp.concatenate(slabs, axis=1).reshape(d, N_HEADS * LANES)


def _expand_o_rows(wo):
    d = wo.shape[1]
    w = wo.reshape(N_KV_HEADS, GQA_GROUP, HEAD_DIM, d)
    slabs = []
    for kv in range(N_KV_HEADS):
        parts = [jnp.zeros_like(w[kv])] * N_KV_HEADS
        parts[kv] = w[kv]
        slabs.append(jnp.concatenate(parts, axis=1))
    return jnp.concatenate(slabs, axis=0).reshape(N_HEADS * LANES, d)


def kernel(x_prompt, x_sample, cache_k, cache_v, state_conv, meta_tokens, rel_bias, w_in, conv_w,
           attn_sink, w_out, ffn_w_gate, ffn_w_up, ffn_w_down, ln_g, ln_b):
    depth = w_in.shape[0]
    alpha = float((2 * depth) ** 0.25)
    n_prompt, seq, d_model = x_prompt.shape
    n_sample, n_new, _ = x_sample.shape
    assert d_model == D_MODEL and seq % MIX_ROWS == 0
    assert n_new >= CONV_WIDTH - 1 and cache_k.shape[2] == WINDOW
    rows_p = n_prompt * seq
    rows_s = n_sample * n_new
    row_meta = rows_p + rows_s
    assert row_meta % BLOCK == 0

    x_all = jnp.concatenate([
        x_prompt.reshape(rows_p, D_MODEL), x_sample.reshape(rows_s, D_MODEL),
        jnp.zeros((META_PAD, D_MODEL), F32), meta_tokens.astype(F32)], axis=0)

    bias_p, bias_s = _bias_tables(rel_bias, n_new)
    col = jnp.arange(2 * BLOCK)[None, :]
    fmask_prompt = jnp.where(col < META_PAD, NEG_INF, 0.0).astype(F32)
    fmask_meta = jnp.where(col < BLOCK + META_PAD, NEG_INF, 0.0).astype(F32)
    zeros_kv = jnp.zeros((1, BLOCK, D_KV), F32)
    zeros_u = jnp.zeros((1, SUBLANES, D_CONV), F32)

    wg = ffn_w_gate.astype(BF16)
    wu = ffn_w_up.astype(BF16)
    wd = ffn_w_down.astype(BF16)
    win = w_in.astype(BF16)
    wout = w_out.astype(BF16)
    ln_g3 = ln_g.reshape(depth, 3, 1, D_MODEL)
    ln_b3 = ln_b.reshape(depth, 3, 1, D_MODEL)
    ck_all = cache_k.reshape(depth, n_sample, WINDOW, D_KV)
    cv_all = cache_v.reshape(depth, n_sample, WINDOW, D_KV)

    kp, vp, cp, ks, vs, cs = [], [], [], [], [], []
    for l in range(depth):
        g, b = ln_g3[l], ln_b3[l]
        x_all = _ffn(x_all, wg[l, 0], wu[l, 0], wd[l, 0], g[0], b[0], alpha)

        x_all, k_m, v_m, u_m = _mixer(
            x_all, row_meta, 1, BLOCK, BLOCK, META_PAD, win[l], wout[l], conv_w[l], attn_sink[l],
            bias_p, fmask_meta, zeros_kv, zeros_kv, zeros_u, g[1], b[1], alpha, "mixer_meta")
        x_all, k_p, v_p, u_p = _mixer(
            x_all, 0, n_prompt, seq, MIX_ROWS, 0, win[l], wout[l], conv_w[l], attn_sink[l],
            bias_p, fmask_prompt, k_m, v_m, u_m, g[1], b[1], alpha, "mixer_prompt")

        w_exp = jnp.concatenate([_expand_q_cols(w_in[l][:, :D_ATTN]), w_in[l][:, D_ATTN:]],
                                axis=1).astype(BF16)
        woa_exp = _expand_o_rows(w_out[l][:D_ATTN]).astype(BF16)
        state_rows = jnp.pad(state_conv[l], ((0, 0), (0, n_new - (CONV_WIDTH - 1)), (0, 0))
                             ).reshape(rows_s, D_CONV)
        q_s, k_s, v_s, mix_s, u_s = _sample_proj(x_all, rows_p, rows_s, w_exp, state_rows,
                                                 conv_w[l], n_new)
        sink_rows = jnp.tile(attn_sink[l], n_new).reshape(n_new * N_HEADS, 1)
        o_s, k_new, v_new = _sample_attn(
            q_s.reshape(n_sample, n_new * N_HEADS, LANES), ck_all[l], cv_all[l],
            k_s.reshape(n_sample, n_new, D_KV), v_s.reshape(n_sample, n_new, D_KV),
            bias_s, sink_rows, n_new)
        x_all = _sample_out(x_all, rows_p, rows_s, o_s.reshape(rows_s, N_HEADS * LANES), mix_s,
                            woa_exp, wout[l][D_ATTN:], g[1], b[1], alpha)

        x_all = _ffn(x_all, wg[l, 1], wu[l, 1], wd[l, 1], g[2], b[2], alpha)

        kp.append(k_p)
        vp.append(v_p)
        cp.append(u_p[:, SUBLANES - (CONV_WIDTH - 1):, :])
        ks.append(k_new)
        vs.append(v_new)
        cs.append(u_s.reshape(n_sample, n_new, D_CONV)[:, n_new - (CONV_WIDTH - 1):, :])

    kv_shape = (depth, -1, WINDOW, N_KV_HEADS, HEAD_DIM)
    return (x_all[:rows_p].reshape(n_prompt, seq, D_MODEL),
            x_all[rows_p:rows_p + rows_s].reshape(n_sample, n_new, D_MODEL),
            jnp.stack(kp).reshape(kv_shape), jnp.stack(vp).reshape(kv_shape), jnp.stack(cp),
            jnp.stack(ks).reshape(kv_shape), jnp.stack(vs).reshape(kv_shape), jnp.stack(cs))
```

```python
import functools
import math

import jax
import jax.numpy as jnp
from jax import lax
from jax.experimental import pallas as pl
from jax.experimental.pallas import tpu as pltpu

F32 = jnp.float32
BF16 = jnp.bfloat16

D_MODEL = 1024
N_HEADS = 8
N_KV_HEADS = 2
HEAD_DIM = 64
GQA_GROUP = N_HEADS // N_KV_HEADS
D_ATTN = N_HEADS * HEAD_DIM
D_CONV = D_MODEL - D_ATTN
D_KV = N_KV_HEADS * HEAD_DIM
D_IN = D_ATTN + 2 * D_KV + 3 * D_CONV
D_FF = 2816
CONV_WIDTH = 3
WINDOW = 128
BLOCK = 128
N_META = 16
N_BUCKETS = 32
MAX_DISTANCE = 128
LN_EPS = 1e-5
Q_SCALE = HEAD_DIM ** -0.5

V7X_VMEM_LIMIT_BYTES = 56 * 1024 * 1024
V7X_MXU_COLUMNS = 256
SUBLANES = 8
LANES = 128

FFN_ROWS = 1024
FFN_SPLIT = 2
FFN_CHUNK = V7X_MXU_COLUMNS
MIX_ROWS = 512
SAMPLE_SEQS = 16
META_PAD = BLOCK - N_META

NEG_INF = float("-inf")


def _const_spec(shape):
    nd = len(shape)
    return pl.BlockSpec(shape, lambda *_: (0,) * nd, pipeline_mode=pl.Buffered(1))


def _layer_spec(shape, *lead):
    block = (None,) * len(lead) + tuple(shape)
    idx = tuple(lead) + (0,) * len(shape)
    return pl.BlockSpec(block, lambda *_: idx, pipeline_mode=pl.Buffered(1))


def _params(*sem):
    return pltpu.CompilerParams(dimension_semantics=sem,
                                vmem_limit_bytes=V7X_VMEM_LIMIT_BYTES)


def _layer_norm(r, g, b):
    mu = jnp.mean(r, axis=-1, keepdims=True)
    rc = r - mu
    var = jnp.mean(rc * rc, axis=-1, keepdims=True)
    return rc * lax.rsqrt(var + LN_EPS) * g + b


def _t5_bucket(d):
    d = jnp.maximum(d, 0)
    max_exact = N_BUCKETS // 2
    df = jnp.maximum(d, 1).astype(F32)
    large = max_exact + (jnp.log(df / max_exact) / math.log(MAX_DISTANCE / max_exact)
                         * (N_BUCKETS - max_exact)).astype(jnp.int32)
    large = jnp.minimum(large, N_BUCKETS - 1)
    return jnp.where(d < max_exact, d, large)


def _bias_kernel(tab_ref, bp_ref, bs_ref, *, n_new):
    qi = lax.broadcasted_iota(jnp.int32, (BLOCK, 2 * BLOCK), 0)
    sj = lax.broadcasted_iota(jnp.int32, (BLOCK, 2 * BLOCK), 1)
    d = qi + BLOCK - sj
    valid = (d >= 0) & (d <= WINDOW)
    bk = _t5_bucket(d)
    for h in range(N_HEADS):
        acc = jnp.zeros(d.shape, F32)
        for b in range(N_BUCKETS):
            acc = jnp.where(bk == b, tab_ref[b, h], acc)
        bp_ref[h] = jnp.where(valid, acc, NEG_INF)
    rows, cols = bs_ref.shape
    r = lax.broadcasted_iota(jnp.int32, (rows, cols), 0)
    s = lax.broadcasted_iota(jnp.int32, (rows, cols), 1)
    ds = r // N_HEADS + WINDOW - s
    hs = r % N_HEADS
    valid_s = (ds >= 0) & (ds <= WINDOW) & (s < WINDOW + n_new)
    bks = _t5_bucket(ds)
    acc = jnp.zeros((rows, cols), F32)
    for h in range(N_HEADS):
        for b in range(N_BUCKETS):
            acc = jnp.where((bks == b) & (hs == h), tab_ref[b, h], acc)
    bs_ref[...] = jnp.where(valid_s, acc, NEG_INF)


def _bias_tables(rel_bias, n_new):
    rows = n_new * N_HEADS
    return pl.pallas_call(
        functools.partial(_bias_kernel, n_new=n_new),
        out_shape=(jax.ShapeDtypeStruct((N_HEADS, BLOCK, 2 * BLOCK), F32),
                   jax.ShapeDtypeStruct((rows, WINDOW + SUBLANES), F32)),
        in_specs=[pl.BlockSpec(memory_space=pltpu.SMEM)],
        name="bias_tables",
    )(rel_bias)


def _ffn_rows(x, wg_ref, wu_ref, wd_ref, g, b, act_ref, alpha):
    rows = x.shape[0]
    xb = x.astype(BF16)
    for c in range(0, D_FF, FFN_CHUNK):
        gate = jnp.dot(xb, wg_ref[:, c:c + FFN_CHUNK], preferred_element_type=F32)
        up = jnp.dot(xb, wu_ref[:, c:c + FFN_CHUNK], preferred_element_type=F32)
        act_ref[0:rows, c:c + FFN_CHUNK] = (jax.nn.silu(gate) * up).astype(BF16)
    y = jnp.dot(act_ref[0:rows, :], wd_ref[...], preferred_element_type=F32)
    return _layer_norm(alpha * x + 0.5 * y, g, b)


def _ffn_kernel(xp_ref, xs_ref, wg_ref, wu_ref, wd_ref, g_ref, b_ref, op_ref, os_ref,
                *act_refs, alpha, n_big):
    i = pl.program_id(0)
    args = (wg_ref, wu_ref, wd_ref, g_ref[...], b_ref[...])

    @pl.when(i < n_big)
    def _():
        part = FFN_ROWS // FFN_SPLIT
        for s in range(FFN_SPLIT):
            rs = slice(s * part, (s + 1) * part)
            op_ref[rs, :] = _ffn_rows(xp_ref[rs, :], *args, act_refs[s], alpha)

    @pl.when(i == n_big)
    def _():
        os_ref[...] = _ffn_rows(xs_ref[...], *args, act_refs[-1], alpha)


def _ffn(xp, xs, wg, wu, wd, ln_g, ln_b, layer, which, ln_idx, alpha):
    assert xp.shape[0] % FFN_ROWS == 0 and D_FF % FFN_CHUNK == 0 and FFN_ROWS % FFN_SPLIT == 0
    n_big = xp.shape[0] // FFN_ROWS
    rows_s = xs.shape[0]
    big_spec = pl.BlockSpec((FFN_ROWS, D_MODEL), lambda i: (jnp.minimum(i, n_big - 1), 0))
    small_spec = pl.BlockSpec((rows_s, D_MODEL), lambda i: (0, 0))
    part = FFN_ROWS // FFN_SPLIT
    scratch = [pltpu.VMEM((part, D_FF), BF16) for _ in range(FFN_SPLIT)]
    scratch.append(pltpu.VMEM((rows_s, D_FF), BF16))
    return pl.pallas_call(
        functools.partial(_ffn_kernel, alpha=alpha, n_big=n_big),
        out_shape=(jax.ShapeDtypeStruct(xp.shape, F32), jax.ShapeDtypeStruct(xs.shape, F32)),
        grid=(n_big + 1,),
        in_specs=[big_spec, small_spec,
                  _layer_spec((D_MODEL, D_FF), layer, which),
                  _layer_spec((D_MODEL, D_FF), layer, which),
                  _layer_spec((D_FF, D_MODEL), layer, which),
                  _layer_spec((1, D_MODEL), layer, ln_idx),
                  _layer_spec((1, D_MODEL), layer, ln_idx)],
        out_specs=(big_spec, small_spec),
        scratch_shapes=scratch,
        compiler_params=_params("arbitrary"),
        name="ffn_ln",
    )(xp, xs, wg, wu, wd, ln_g, ln_b)


def _mixer_kernel(x_ref, win_ref, wout_ref, convw_ref, sink_ref, bias_ref, fmask_ref,
                  kinit_ref, vinit_ref, uinit_ref, g_ref, b_ref,
                  xo_ref, klast_ref, vlast_ref, ulast_ref,
                  kd_scr, vd_scr, uscr, ascr, *, rows, alpha, zero_rows, layer):
    t = pl.program_id(1)
    lane = lax.broadcasted_iota(jnp.int32, (1, LANES), 1)
    low = lane < HEAD_DIM

    def dup_heads(a):
        sw = pltpu.roll(a, HEAD_DIM, 1)
        return (jnp.where(low, a, sw).astype(BF16), jnp.where(low, sw, a).astype(BF16))

    @pl.when(t == 0)
    def _():
        for kv, (kd, vd) in enumerate(zip(dup_heads(kinit_ref[0]), dup_heads(vinit_ref[0]))):
            kd_scr[kv, 0:BLOCK, :] = kd
            vd_scr[kv, 0:BLOCK, :] = vd
        uscr[0:SUBLANES, :] = uinit_ref[0]

    x = x_ref[...]
    if zero_rows:
        ridx = lax.broadcasted_iota(jnp.int32, x.shape, 0)
        x = jnp.where(ridx >= zero_rows, x, 0.0)
    xb = x.astype(BF16)
    c0 = D_ATTN + 2 * D_KV
    zq = jnp.dot(xb, win_ref[:, 0:c0], preferred_element_type=F32)
    zc = jnp.dot(xb, win_ref[:, c0:], preferred_element_type=F32)
    k = zq[:, D_ATTN:D_ATTN + D_KV]
    v = zq[:, D_ATTN + D_KV:c0]
    u = zc[:, D_CONV:2 * D_CONV] * zc[:, 2 * D_CONV:3 * D_CONV]
    klast_ref[0] = k[rows - BLOCK:, :]
    vlast_ref[0] = v[rows - BLOCK:, :]
    ulast_ref[0] = u[rows - SUBLANES:, :]
    for kv, (kd, vd) in enumerate(zip(dup_heads(k), dup_heads(v))):
        kd_scr[kv, BLOCK:BLOCK + rows, :] = kd
        vd_scr[kv, BLOCK:BLOCK + rows, :] = vd
    uscr[SUBLANES:SUBLANES + rows, :] = u

    qs = zq[:, :D_ATTN] * Q_SCALE
    lane_q = lax.broadcasted_iota(jnp.int32, (1, D_ATTN), 1) % LANES
    q_even = jnp.where(lane_q < HEAD_DIM, qs, 0.0).astype(BF16)
    q_odd = jnp.where(lane_q < HEAD_DIM, 0.0, qs).astype(BF16)

    first = jnp.where(t == 0, fmask_ref[...], 0.0)
    ones_cols = jnp.ones((2 * BLOCK, LANES), BF16)
    for j in range(rows // BLOCK):
        r0 = j * BLOCK
        for kv in range(N_KV_HEADS):
            kd = kd_scr[kv, r0:r0 + 2 * BLOCK, :]
            vdx = jnp.concatenate([vd_scr[kv, r0:r0 + 2 * BLOCK, :], ones_cols], axis=1)
            heads = range(kv * GQA_GROUP, (kv + 1) * GQA_GROUP)
            q4 = jnp.concatenate(
                [(q_odd if h % 2 else q_even)[r0:r0 + BLOCK, (h // 2) * LANES:(h // 2 + 1) * LANES]
                 for h in heads], axis=0)
            s4 = lax.dot_general(q4, kd, (((1,), (1,)), ((), ())), preferred_element_type=F32)
            ps, es = [], []
            for g, h in enumerate(heads):
                s = s4[g * BLOCK:(g + 1) * BLOCK] + bias_ref[h]
                if j == 0:
                    s = s + first
                sl, sr = s[:, :LANES], s[:, LANES:]
                sk = sink_ref[layer, h]
                m1 = jnp.max(jnp.maximum(sl, sr), axis=-1, keepdims=True)
                mb = jnp.broadcast_to(jnp.maximum(m1, sk), (BLOCK, LANES))
                ps.append(jnp.concatenate([jnp.exp(sl - mb), jnp.exp(sr - mb)],
                                          axis=1).astype(BF16))
                es.append(jnp.exp(sk - mb))
            ox = jnp.dot(jnp.concatenate(ps, axis=0), vdx, preferred_element_type=F32)
            outs = []
            for g in range(GQA_GROUP):
                og = ox[g * BLOCK:(g + 1) * BLOCK]
                outs.append(og[:, :LANES] / (og[:, LANES:] + es[g]))
            for i in range(GQA_GROUP // 2):
                slab = jnp.where(low, outs[2 * i], outs[2 * i + 1])
                col = (kv * GQA_GROUP // 2 + i) * LANES
                ascr[r0:r0 + BLOCK, col:col + LANES] = slab.astype(BF16)

    um2 = uscr[SUBLANES - 2:SUBLANES - 2 + rows, :]
    um1 = uscr[SUBLANES - 1:SUBLANES - 1 + rows, :]
    conv = convw_ref[0:1, :] * um2 + convw_ref[1:2, :] * um1 + convw_ref[2:3, :] * u
    mix = zc[:, 0:D_CONV] * conv
    y = (jnp.dot(ascr[...], wout_ref[0:D_ATTN, :], preferred_element_type=F32)
         + jnp.dot(mix.astype(BF16), wout_ref[D_ATTN:, :], preferred_element_type=F32))
    xo_ref[...] = _layer_norm(alpha * x + y, g_ref[...], b_ref[...])

    for kv in range(N_KV_HEADS):
        kd_scr[kv, 0:BLOCK, :] = kd_scr[kv, rows:rows + BLOCK, :]
        vd_scr[kv, 0:BLOCK, :] = vd_scr[kv, rows:rows + BLOCK, :]
    uscr[0:SUBLANES, :] = uscr[rows:rows + SUBLANES, :]


def _mixer(x, row0, n_seq, seq_len, rows, zero_rows, in_place, win, wout, convw, sink, bias_p,
           fmask, kinit, vinit, uinit, ln_g, ln_b, layer, alpha, name):
    assert seq_len % rows == 0 and rows % BLOCK == 0 and row0 % rows == 0
    steps = seq_len // rows
    blk0 = row0 // rows
    x_spec = pl.BlockSpec((rows, D_MODEL), lambda s, t: (blk0 + s * steps + t, 0))
    seq_spec = lambda shape: pl.BlockSpec((1,) + shape, lambda s, t: (s, 0, 0))
    kern = functools.partial(_mixer_kernel, rows=rows, alpha=alpha, zero_rows=zero_rows,
                             layer=layer)
    return pl.pallas_call(
        kern,
        out_shape=(jax.ShapeDtypeStruct(x.shape, F32),
                   jax.ShapeDtypeStruct((n_seq, BLOCK, D_KV), F32),
                   jax.ShapeDtypeStruct((n_seq, BLOCK, D_KV), F32),
                   jax.ShapeDtypeStruct((n_seq, SUBLANES, D_CONV), F32)),
        grid=(n_seq, steps),
        in_specs=[x_spec,
                  _layer_spec((D_MODEL, D_IN), layer),
                  _layer_spec((D_MODEL, D_MODEL), layer),
                  _layer_spec((CONV_WIDTH, D_CONV), layer),
                  pl.BlockSpec(memory_space=pltpu.SMEM),
                  _const_spec(bias_p.shape), _const_spec(fmask.shape),
                  _const_spec(kinit.shape), _const_spec(vinit.shape), _const_spec(uinit.shape),
                  _layer_spec((1, D_MODEL), layer, 1),
                  _layer_spec((1, D_MODEL), layer, 1)],
        out_specs=(x_spec, seq_spec((BLOCK, D_KV)), seq_spec((BLOCK, D_KV)),
                   seq_spec((SUBLANES, D_CONV))),
        scratch_shapes=[pltpu.VMEM((N_KV_HEADS, BLOCK + rows, D_KV), BF16),
                        pltpu.VMEM((N_KV_HEADS, BLOCK + rows, D_KV), BF16),
                        pltpu.VMEM((SUBLANES + rows, D_CONV), F32),
                        pltpu.VMEM((rows, D_ATTN), BF16)],
        input_output_aliases={0: 0} if in_place else {},
        compiler_params=_params("arbitrary", "arbitrary"),
        name=name,
    )(x, win, wout, convw, sink, bias_p, fmask, kinit, vinit, uinit, ln_g, ln_b)


def _sample_proj_kernel(x_ref, w_ref, state_ref, convw_ref,
                        q_ref, k_ref, v_ref, mix_ref, u_ref, *, n_new):
    z = jnp.dot(x_ref[...].astype(BF16), w_ref[...], preferred_element_type=F32)
    qw = N_HEADS * LANES
    q_ref[...] = z[:, :qw] * Q_SCALE
    k_ref[...] = z[:, qw:qw + D_KV]
    v_ref[...] = z[:, qw + D_KV:qw + 2 * D_KV]
    c0 = qw + 2 * D_KV
    u = z[:, c0 + D_CONV:c0 + 2 * D_CONV] * z[:, c0 + 2 * D_CONV:c0 + 3 * D_CONV]
    u_ref[...] = u
    n = u.shape[0]
    tok = lax.broadcasted_iota(jnp.int32, u.shape, 0) % n_new
    st = state_ref[...]
    um2 = jnp.where(tok < 2, st, pltpu.roll(u, 2, 0))
    um1 = jnp.where(tok < 1, pltpu.roll(st, n - 1, 0), pltpu.roll(u, 1, 0))
    conv = convw_ref[0:1, :] * um2 + convw_ref[1:2, :] * um1 + convw_ref[2:3, :] * u
    mix_ref[...] = z[:, c0:c0 + D_CONV] * conv


def _sample_proj(xs, n, w_exp, state_rows, convw, layer, n_new):
    qw = N_HEADS * LANES
    return pl.pallas_call(
        functools.partial(_sample_proj_kernel, n_new=n_new),
        out_shape=(jax.ShapeDtypeStruct((n, qw), F32),
                   jax.ShapeDtypeStruct((n, D_KV), F32),
                   jax.ShapeDtypeStruct((n, D_KV), F32),
                   jax.ShapeDtypeStruct((n, D_CONV), F32),
                   jax.ShapeDtypeStruct((n, D_CONV), F32)),
        grid=(1,),
        in_specs=[pl.BlockSpec((n, D_MODEL), lambda i: (0, 0)),
                  _layer_spec((D_MODEL, w_exp.shape[2]), layer),
                  _layer_spec((n, D_CONV), layer),
                  _layer_spec((CONV_WIDTH, D_CONV), layer)],
        out_specs=tuple(pl.BlockSpec((n, w), lambda i: (0, 0))
                        for w in (qw, D_KV, D_KV, D_CONV, D_CONV)),
        compiler_params=_params("arbitrary"),
        name="sample_proj",
    )(xs, w_exp, state_rows, convw)


def _sample_attn_kernel(q_ref, ck_ref, cv_ref, kn_ref, vn_ref, bias_ref, sink_ref, *rest, n_new):
    o_ref, ok_ref, ov_ref = rest[-3:]
    qb = q_ref[...].astype(BF16)
    ck = ck_ref[...]
    cv = cv_ref[...]
    kn = kn_ref[...]
    vn = vn_ref[...]
    bias = bias_ref[...]
    sink = sink_ref[...]
    s_c = jnp.einsum("bqd,bkd->bqk", qb, ck.astype(BF16),
                     preferred_element_type=F32) + bias[None, :, :WINDOW]
    qf = qb.astype(F32)
    knf = kn.astype(BF16).astype(F32)
    vnf = vn.astype(BF16).astype(F32)
    s_n = [jnp.sum(qf * knf[:, i:i + 1, :], axis=-1, keepdims=True)
           + bias[None, :, WINDOW + i:WINDOW + i + 1] for i in range(n_new)]
    m = jnp.maximum(jnp.max(s_c, axis=-1, keepdims=True), sink[None])
    for s in s_n:
        m = jnp.maximum(m, s)
    p_c = jnp.exp(s_c - m)
    den = jnp.sum(p_c, axis=-1, keepdims=True) + jnp.exp(sink[None] - m)
    o = jnp.einsum("bqk,bkd->bqd", p_c.astype(BF16), cv.astype(BF16),
                   preferred_element_type=F32)
    for i, s in enumerate(s_n):
        p = jnp.exp(s - m)
        den = den + p
        o = o + p.astype(BF16).astype(F32) * vnf[:, i:i + 1, :]
    o_ref[...] = o / den
    ok_ref[:, 0:WINDOW - n_new, :] = ck_ref[:, n_new:WINDOW, :]
    ok_ref[:, WINDOW - n_new:WINDOW, :] = kn
    ov_ref[:, 0:WINDOW - n_new, :] = cv_ref[:, n_new:WINDOW, :]
    ov_ref[:, WINDOW - n_new:WINDOW, :] = vn


def _sample_attn(q_rows, ck_all, cv_all, kn, vn, bias_s, sink_rows, k_stack, v_stack, layer,
                 n_new):
    depth, n_seq = ck_all.shape[0], ck_all.shape[1]
    assert n_seq % SAMPLE_SEQS == 0
    qr = n_new * N_HEADS
    seq_spec = lambda r, w: pl.BlockSpec((SAMPLE_SEQS, r, w), lambda i: (i, 0, 0))
    lay_spec = pl.BlockSpec((None, SAMPLE_SEQS, WINDOW, D_KV), lambda i: (layer, i, 0, 0))
    stack_shape = jax.ShapeDtypeStruct((depth, n_seq, WINDOW, D_KV), F32)
    in_specs = [seq_spec(qr, LANES), lay_spec, lay_spec,
                seq_spec(n_new, D_KV), seq_spec(n_new, D_KV),
                _layer_spec(bias_s.shape), _layer_spec(sink_rows.shape[1:], layer)]
    args = [q_rows, ck_all, cv_all, kn, vn, bias_s, sink_rows]
    aliases = {}
    if k_stack is not None:
        in_specs += [pl.BlockSpec(memory_space=pl.ANY)] * 2
        aliases = {len(args): 1, len(args) + 1: 2}
        args += [k_stack, v_stack]
    return pl.pallas_call(
        functools.partial(_sample_attn_kernel, n_new=n_new),
        out_shape=(jax.ShapeDtypeStruct((n_seq, qr, LANES), F32), stack_shape, stack_shape),
        grid=(n_seq // SAMPLE_SEQS,),
        in_specs=in_specs,
        out_specs=(seq_spec(qr, LANES), lay_spec, lay_spec),
        input_output_aliases=aliases,
        compiler_params=_params("arbitrary"),
        name="sample_attn",
    )(*args)


def _sample_out_kernel(x_ref, a_ref, mix_ref, woa_ref, woc_ref, g_ref, b_ref, xo_ref, *, alpha):
    y = (jnp.dot(a_ref[...].astype(BF16), woa_ref[...], preferred_element_type=F32)
         + jnp.dot(mix_ref[...].astype(BF16), woc_ref[...], preferred_element_type=F32))
    xo_ref[...] = _layer_norm(alpha * x_ref[...] + y, g_ref[...], b_ref[...])


def _sample_out(xs, n, a_exp, mix, woa_exp, wout, ln_g, ln_b, layer, alpha):
    x_spec = pl.BlockSpec((n, D_MODEL), lambda i: (0, 0))
    return pl.pallas_call(
        functools.partial(_sample_out_kernel, alpha=alpha),
        out_shape=jax.ShapeDtypeStruct(xs.shape, F32),
        grid=(1,),
        in_specs=[x_spec, _const_spec(a_exp.shape), _const_spec(mix.shape),
                  _layer_spec((N_HEADS * LANES, D_MODEL), layer),
                  pl.BlockSpec((None, D_CONV, D_MODEL), lambda i: (layer, D_ATTN // D_CONV, 0),
                               pipeline_mode=pl.Buffered(1)),
                  _layer_spec((1, D_MODEL), layer, 1),
                  _layer_spec((1, D_MODEL), layer, 1)],
        out_specs=x_spec,
        input_output_aliases={0: 0},
        compiler_params=_params("arbitrary"),
        name="sample_out",
    )(xs, a_exp, mix, woa_exp, wout, ln_g, ln_b)


def _expand_q_cols(wq):
    depth, d = wq.shape[0], wq.shape[1]
    w = wq.reshape(depth, d, N_KV_HEADS, GQA_GROUP, HEAD_DIM)
    slabs = []
    for kv in range(N_KV_HEADS):
        parts = [jnp.zeros_like(w[:, :, kv])] * N_KV_HEADS
        parts[kv] = w[:, :, kv]
        slabs.append(jnp.concatenate(parts, axis=-1))
    return jnp.concatenate(slabs, axis=2).reshape(depth, d, N_HEADS * LANES)


def _expand_o_rows(wo):
    depth, d = wo.shape[0], wo.shape[2]
    w = wo.reshape(depth, N_KV_HEADS, GQA_GROUP, HEAD_DIM, d)
    slabs = []
    for kv in range(N_KV_HEADS):
        parts = [jnp.zeros_like(w[:, kv])] * N_KV_HEADS
        parts[kv] = w[:, kv]
        slabs.append(jnp.concatenate(parts, axis=2))
    return jnp.concatenate(slabs, axis=1).reshape(depth, N_HEADS * LANES, d)


def kernel(x_prompt, x_sample, cache_k, cache_v, state_conv, meta_tokens, rel_bias, w_in, conv_w,
           attn_sink, w_out, ffn_w_gate, ffn_w_up, ffn_w_down, ln_g, ln_b):
    depth = w_in.shape[0]
    alpha = float((2 * depth) ** 0.25)
    n_prompt, seq, d_model = x_prompt.shape
    n_sample, n_new, _ = x_sample.shape
    assert d_model == D_MODEL and seq % MIX_ROWS == 0
    assert n_new >= CONV_WIDTH - 1 and cache_k.shape[2] == WINDOW
    rows_p = n_prompt * seq
    rows_s = n_sample * n_new
    assert rows_s % BLOCK == 0

    xp = x_prompt.reshape(rows_p, D_MODEL)
    xs = jnp.concatenate([x_sample.reshape(rows_s, D_MODEL),
                          jnp.zeros((META_PAD, D_MODEL), F32), meta_tokens.astype(F32)], axis=0)

    bias_p, bias_s = _bias_tables(rel_bias, n_new)
    col = jnp.arange(2 * BLOCK)[None, :]
    fmask_prompt = jnp.where(col < META_PAD, NEG_INF, 0.0).astype(F32)
    fmask_meta = jnp.where(col < BLOCK + META_PAD, NEG_INF, 0.0).astype(F32)
    zeros_kv = jnp.zeros((1, BLOCK, D_KV), F32)
    zeros_u = jnp.zeros((1, SUBLANES, D_CONV), F32)

    wg = ffn_w_gate.astype(BF16)
    wu = ffn_w_up.astype(BF16)
    wd = ffn_w_down.astype(BF16)
    win = w_in.astype(BF16)
    wout = w_out.astype(BF16)
    w_exp = jnp.concatenate([_expand_q_cols(win[:, :, :D_ATTN]), win[:, :, D_ATTN:]], axis=2)
    woa_exp = _expand_o_rows(wout[:, :D_ATTN])
    ln_g4 = ln_g.reshape(depth, 3, 1, D_MODEL)
    ln_b4 = ln_b.reshape(depth, 3, 1, D_MODEL)
    ck_all = cache_k.reshape(depth, n_sample, WINDOW, D_KV)
    cv_all = cache_v.reshape(depth, n_sample, WINDOW, D_KV)
    state_rows = jnp.pad(state_conv, ((0, 0), (0, 0), (0, n_new - (CONV_WIDTH - 1)), (0, 0))
                         ).reshape(depth, rows_s, D_CONV)
    sink_rows = jnp.tile(attn_sink, (1, n_new)).reshape(depth, n_new * N_HEADS, 1)

    kp, vp, cp, cs = [], [], [], []
    k_stack = v_stack = None
    for l in range(depth):
        xp, xs = _ffn(xp, xs, wg, wu, wd, ln_g4, ln_b4, l, 0, 0, alpha)

        xs, k_m, v_m, u_m = _mixer(
            xs, rows_s, 1, BLOCK, BLOCK, META_PAD, True, win, wout, conv_w, attn_sink, bias_p,
            fmask_meta, zeros_kv, zeros_kv, zeros_u, ln_g4, ln_b4, l, alpha, "mixer_meta")
        xp, k_p, v_p, u_p = _mixer(
            xp, 0, n_prompt, seq, MIX_ROWS, 0, False, win, wout, conv_w, attn_sink, bias_p,
            fmask_prompt, k_m, v_m, u_m, ln_g4, ln_b4, l, alpha, "mixer_prompt")

        q_s, k_s, v_s, mix_s, u_s = _sample_proj(xs, rows_s, w_exp, state_rows, conv_w, l, n_new)
        o_s, k_stack, v_stack = _sample_attn(
            q_s.reshape(n_sample, n_new * N_HEADS, LANES), ck_all, cv_all,
            k_s.reshape(n_sample, n_new, D_KV), v_s.reshape(n_sample, n_new, D_KV),
            bias_s, sink_rows, k_stack, v_stack, l, n_new)
        xs = _sample_out(xs, rows_s, o_s.reshape(rows_s, N_HEADS * LANES), mix_s, woa_exp, wout,
                         ln_g4, ln_b4, l, alpha)

        xp, xs = _ffn(xp, xs, wg, wu, wd, ln_g4, ln_b4, l, 1, 2, alpha)

        kp.append(k_p)
        vp.append(v_p)
        cp.append(u_p[:, SUBLANES - (CONV_WIDTH - 1):, :])
        cs.append(u_s.reshape(n_sample, n_new, D_CONV)[:, n_new - (CONV_WIDTH - 1):, :])

    kv_shape = (depth, -1, WINDOW, N_KV_HEADS, HEAD_DIM)
    return (xp.reshape(n_prompt, seq, D_MODEL),
            xs[:rows_s].reshape(n_sample, n_new, D_MODEL),
            jnp.stack(kp).reshape(kv_shape), jnp.stack(vp).reshape(kv_shape), jnp.stack(cp),
            k_stack.reshape(kv_shape), v_stack.reshape(kv_shape), jnp.stack(cs))
```

```python
import functools
import math

import jax
import jax.numpy as jnp
from jax import lax
from jax.experimental import pallas as pl
from jax.experimental.pallas import tpu as pltpu

F32 = jnp.float32
BF16 = jnp.bfloat16

D_MODEL = 1024
N_HEADS = 8
N_KV_HEADS = 2
HEAD_DIM = 64
GQA_GROUP = N_HEADS // N_KV_HEADS
D_ATTN = N_HEADS * HEAD_DIM
D_CONV = D_MODEL - D_ATTN
D_KV = N_KV_HEADS * HEAD_DIM
D_IN = D_ATTN + 2 * D_KV + 3 * D_CONV
D_FF = 2816
CONV_WIDTH = 3
WINDOW = 128
BLOCK = 128
N_META = 16
N_BUCKETS = 32
MAX_DISTANCE = 128
LN_EPS = 1e-5
Q_SCALE = HEAD_DIM ** -0.5

V7X_VMEM_LIMIT_BYTES = 56 * 1024 * 1024
V7X_MXU_COLUMNS = 256
SUBLANES = 8
LANES = 128

FFN_ROWS = 1024
FFN_NORM_PARTS = 8
FFN_CHUNK = V7X_MXU_COLUMNS
MIX_ROWS = 512
SAMPLE_SEQS = 16
META_PAD = BLOCK - N_META

NEG_INF = float("-inf")


def _const_spec(shape):
    nd = len(shape)
    return pl.BlockSpec(shape, lambda *_: (0,) * nd, pipeline_mode=pl.Buffered(1))


def _layer_spec(shape, *lead):
    block = (None,) * len(lead) + tuple(shape)
    idx = tuple(lead) + (0,) * len(shape)
    return pl.BlockSpec(block, lambda *_: idx, pipeline_mode=pl.Buffered(1))


def _params(*sem):
    return pltpu.CompilerParams(dimension_semantics=sem,
                                vmem_limit_bytes=V7X_VMEM_LIMIT_BYTES)


def _layer_norm(r, g, b):
    mu = jnp.mean(r, axis=-1, keepdims=True)
    rc = r - mu
    var = jnp.mean(rc * rc, axis=-1, keepdims=True)
    return rc * lax.rsqrt(var + LN_EPS) * g + b


def _t5_bucket(d):
    d = jnp.maximum(d, 0)
    max_exact = N_BUCKETS // 2
    df = jnp.maximum(d, 1).astype(F32)
    large = max_exact + (jnp.log(df / max_exact) / math.log(MAX_DISTANCE / max_exact)
                         * (N_BUCKETS - max_exact)).astype(jnp.int32)
    large = jnp.minimum(large, N_BUCKETS - 1)
    return jnp.where(d < max_exact, d, large)


def _bias_kernel(tab_ref, bp_ref, bs_ref, *, n_new):
    qi = lax.broadcasted_iota(jnp.int32, (BLOCK, 2 * BLOCK), 0)
    sj = lax.broadcasted_iota(jnp.int32, (BLOCK, 2 * BLOCK), 1)
    d = qi + BLOCK - sj
    valid = (d >= 0) & (d <= WINDOW)
    bk = _t5_bucket(d)
    for h in range(N_HEADS):
        acc = jnp.zeros(d.shape, F32)
        for b in range(N_BUCKETS):
            acc = jnp.where(bk == b, tab_ref[b, h], acc)
        bp_ref[h] = jnp.where(valid, acc, NEG_INF)
    rows, cols = bs_ref.shape
    r = lax.broadcasted_iota(jnp.int32, (rows, cols), 0)
    s = lax.broadcasted_iota(jnp.int32, (rows, cols), 1)
    ds = r // N_HEADS + WINDOW - s
    hs = r % N_HEADS
    valid_s = (ds >= 0) & (ds <= WINDOW) & (s < WINDOW + n_new)
    bks = _t5_bucket(ds)
    acc = jnp.zeros((rows, cols), F32)
    for h in range(N_HEADS):
        for b in range(N_BUCKETS):
            acc = jnp.where((bks == b) & (hs == h), tab_ref[b, h], acc)
    bs_ref[...] = jnp.where(valid_s, acc, NEG_INF)


def _bias_tables(rel_bias, n_new):
    rows = n_new * N_HEADS
    return pl.pallas_call(
        functools.partial(_bias_kernel, n_new=n_new),
        out_shape=(jax.ShapeDtypeStruct((N_HEADS, BLOCK, 2 * BLOCK), F32),
                   jax.ShapeDtypeStruct((rows, WINDOW + SUBLANES), F32)),
        in_specs=[pl.BlockSpec(memory_space=pltpu.SMEM)],
        name="bias_tables",
    )(rel_bias)


def _zero_after(v):
    u = lax.bitcast_convert_type(v, jnp.uint32)
    z = lax.shift_right_logical(lax.shift_right_logical(u, jnp.uint32(16)), jnp.uint32(16))
    return lax.bitcast_convert_type(z, F32)


def _fold_rows(a):
    a = a.reshape(a.shape[0] // 16, 16, a.shape[1]).sum(axis=0)
    return a.reshape(16, a.shape[1] // LANES, LANES).sum(axis=1)


def _ffn_pre_norm(x, wg_ref, wu_ref, wd_ref, act_ref, alpha, side_work):
    rows = x.shape[0]
    xb = x.astype(BF16)
    for ci, c in enumerate(range(0, D_FF, FFN_CHUNK)):
        gate = jnp.dot(xb, wg_ref[:, c:c + FFN_CHUNK], preferred_element_type=F32)
        up = jnp.dot(xb, wu_ref[:, c:c + FFN_CHUNK], preferred_element_type=F32)
        act = jax.nn.silu(gate) * up
        act_ref[0:rows, c:c + FFN_CHUNK] = act.astype(BF16)
        z = side_work(ci)
        if z is not None:
            act_ref[0:16, c:c + LANES] = (act[0:16, 0:LANES] + z).astype(BF16)
    y = jnp.dot(act_ref[0:rows, :], wd_ref[...], preferred_element_type=F32)
    return alpha * x + 0.5 * y


def _ffn_kernel(xp_ref, xs_ref, wg_ref, wu_ref, wd_ref, g_ref, b_ref, op_ref, os_ref,
                act_ref, r_ref, *, alpha, n_big):
    i = pl.program_id(0)
    w = (wg_ref, wu_ref, wd_ref)
    g, b = g_ref[...], b_ref[...]
    part = FFN_ROWS // FFN_NORM_PARTS

    def norm_previous_tile(ci):
        if ci >= FFN_NORM_PARTS:
            return None
        rs = slice(ci * part, (ci + 1) * part)
        out = _layer_norm(r_ref[rs, :], g, b)
        op_ref[rs, :] = out
        return _zero_after(_fold_rows(out))

    @pl.when(i == 0)
    def _():
        r_ref[...] = jnp.zeros(r_ref.shape, F32)

    @pl.when(i < n_big)
    def _():
        r_ref[...] = _ffn_pre_norm(xp_ref[...], *w, act_ref, alpha, norm_previous_tile)

    @pl.when(i == n_big)
    def _():
        os_ref[...] = _ffn_pre_norm(xs_ref[...], *w, act_ref, alpha, norm_previous_tile)

    @pl.when(i == n_big + 1)
    def _():
        os_ref[...] = _layer_norm(os_ref[...], g, b)


def _ffn(xp, xs, wg, wu, wd, ln_g, ln_b, layer, which, ln_idx, alpha):
    assert xp.shape[0] % FFN_ROWS == 0 and D_FF % FFN_CHUNK == 0
    assert FFN_NORM_PARTS <= D_FF // FFN_CHUNK and FFN_ROWS % (16 * FFN_NORM_PARTS) == 0
    n_big = xp.shape[0] // FFN_ROWS
    rows_s = xs.shape[0]
    assert rows_s <= FFN_ROWS
    in_spec = pl.BlockSpec((FFN_ROWS, D_MODEL), lambda i: (jnp.minimum(i, n_big - 1), 0))
    out_spec = pl.BlockSpec((FFN_ROWS, D_MODEL), lambda i: (jnp.clip(i - 1, 0, n_big - 1), 0))
    small_in = pl.BlockSpec((rows_s, D_MODEL), lambda i: (0, 0), pipeline_mode=pl.Buffered(1))
    small_out = pl.BlockSpec((rows_s, D_MODEL), lambda i: (0, 0))
    return pl.pallas_call(
        functools.partial(_ffn_kernel, alpha=alpha, n_big=n_big),
        out_shape=(jax.ShapeDtypeStruct(xp.shape, F32), jax.ShapeDtypeStruct(xs.shape, F32)),
        grid=(n_big + 2,),
        in_specs=[in_spec, small_in,
                  _layer_spec((D_MODEL, D_FF), layer, which),
                  _layer_spec((D_MODEL, D_FF), layer, which),
                  _layer_spec((D_FF, D_MODEL), layer, which),
                  _layer_spec((1, D_MODEL), layer, ln_idx),
                  _layer_spec((1, D_MODEL), layer, ln_idx)],
        out_specs=(out_spec, small_out),
        scratch_shapes=[pltpu.VMEM((FFN_ROWS, D_FF), BF16), pltpu.VMEM((FFN_ROWS, D_MODEL), F32)],
        compiler_params=_params("arbitrary"),
        name="ffn_ln",
    )(xp, xs, wg, wu, wd, ln_g, ln_b)


def _mixer_kernel(x_ref, win_ref, wout_ref, convw_ref, sink_ref, bias_ref, fmask_ref,
                  kinit_ref, vinit_ref, uinit_ref, g_ref, b_ref,
                  xo_ref, klast_ref, vlast_ref, ulast_ref,
                  kd_scr, vd_scr, uscr, ascr, *, rows, alpha, zero_rows, layer):
    t = pl.program_id(1)
    lane = lax.broadcasted_iota(jnp.int32, (1, LANES), 1)
    low = lane < HEAD_DIM

    def dup_heads(a):
        sw = pltpu.roll(a, HEAD_DIM, 1)
        return (jnp.where(low, a, sw).astype(BF16), jnp.where(low, sw, a).astype(BF16))

    @pl.when(t == 0)
    def _():
        for kv, (kd, vd) in enumerate(zip(dup_heads(kinit_ref[0]), dup_heads(vinit_ref[0]))):
            kd_scr[kv, 0:BLOCK, :] = kd
            vd_scr[kv, 0:BLOCK, :] = vd
        uscr[0:SUBLANES, :] = uinit_ref[0]

    x = x_ref[...]
    if zero_rows:
        ridx = lax.broadcasted_iota(jnp.int32, x.shape, 0)
        x = jnp.where(ridx >= zero_rows, x, 0.0)
    xb = x.astype(BF16)
    c0 = D_ATTN + 2 * D_KV
    zq = jnp.dot(xb, win_ref[:, 0:c0], preferred_element_type=F32)
    zc = jnp.dot(xb, win_ref[:, c0:], preferred_element_type=F32)
    k = zq[:, D_ATTN:D_ATTN + D_KV]
    v = zq[:, D_ATTN + D_KV:c0]
    u = zc[:, D_CONV:2 * D_CONV] * zc[:, 2 * D_CONV:3 * D_CONV]
    klast_ref[0] = k[rows - BLOCK:, :]
    vlast_ref[0] = v[rows - BLOCK:, :]
    ulast_ref[0] = u[rows - SUBLANES:, :]
    for kv, (kd, vd) in enumerate(zip(dup_heads(k), dup_heads(v))):
        kd_scr[kv, BLOCK:BLOCK + rows, :] = kd
        vd_scr[kv, BLOCK:BLOCK + rows, :] = vd
    uscr[SUBLANES:SUBLANES + rows, :] = u

    qs = zq[:, :D_ATTN] * Q_SCALE
    lane_q = lax.broadcasted_iota(jnp.int32, (1, D_ATTN), 1) % LANES
    q_even = jnp.where(lane_q < HEAD_DIM, qs, 0.0).astype(BF16)
    q_odd = jnp.where(lane_q < HEAD_DIM, 0.0, qs).astype(BF16)

    first = jnp.where(t == 0, fmask_ref[...], 0.0)
    ones_cols = jnp.ones((2 * BLOCK, LANES), BF16)
    for j in range(rows // BLOCK):
        r0 = j * BLOCK
        for kv in range(N_KV_HEADS):
            kd = kd_scr[kv, r0:r0 + 2 * BLOCK, :]
            vdx = jnp.concatenate([vd_scr[kv, r0:r0 + 2 * BLOCK, :], ones_cols], axis=1)
            heads = range(kv * GQA_GROUP, (kv + 1) * GQA_GROUP)
            q4 = jnp.concatenate(
                [(q_odd if h % 2 else q_even)[r0:r0 + BLOCK, (h // 2) * LANES:(h // 2 + 1) * LANES]
                 for h in heads], axis=0)
            s4 = lax.dot_general(q4, kd, (((1,), (1,)), ((), ())), preferred_element_type=F32)
            ps, es = [], []
            for g, h in enumerate(heads):
                s = s4[g * BLOCK:(g + 1) * BLOCK] + bias_ref[h]
                if j == 0:
                    s = s + first
                sl, sr = s[:, :LANES], s[:, LANES:]
                sk = sink_ref[layer, h]
                m1 = jnp.max(jnp.maximum(sl, sr), axis=-1, keepdims=True)
                mb = jnp.broadcast_to(jnp.maximum(m1, sk), (BLOCK, LANES))
                ps.append(jnp.concatenate([jnp.exp(sl - mb), jnp.exp(sr - mb)],
                                          axis=1).astype(BF16))
                es.append(jnp.exp(sk - mb))
            ox = jnp.dot(jnp.concatenate(ps, axis=0), vdx, preferred_element_type=F32)
            outs = []
            for g in range(GQA_GROUP):
                og = ox[g * BLOCK:(g + 1) * BLOCK]
                outs.append(og[:, :LANES] / (og[:, LANES:] + es[g]))
            for i in range(GQA_GROUP // 2):
                slab = jnp.where(low, outs[2 * i], outs[2 * i + 1])
                col = (kv * GQA_GROUP // 2 + i) * LANES
                ascr[r0:r0 + BLOCK, col:col + LANES] = slab.astype(BF16)

    um2 = uscr[SUBLANES - 2:SUBLANES - 2 + rows, :]
    um1 = uscr[SUBLANES - 1:SUBLANES - 1 + rows, :]
    conv = convw_ref[0:1, :] * um2 + convw_ref[1:2, :] * um1 + convw_ref[2:3, :] * u
    mix = zc[:, 0:D_CONV] * conv
    y = (jnp.dot(ascr[...], wout_ref[0:D_ATTN, :], preferred_element_type=F32)
         + jnp.dot(mix.astype(BF16), wout_ref[D_ATTN:, :], preferred_element_type=F32))
    xo_ref[...] = _layer_norm(alpha * x + y, g_ref[...], b_ref[...])

    for kv in range(N_KV_HEADS):
        kd_scr[kv, 0:BLOCK, :] = kd_scr[kv, rows:rows + BLOCK, :]
        vd_scr[kv, 0:BLOCK, :] = vd_scr[kv, rows:rows + BLOCK, :]
    uscr[0:SUBLANES, :] = uscr[rows:rows + SUBLANES, :]


def _mixer(x, row0, n_seq, seq_len, rows, zero_rows, in_place, win, wout, convw, sink, bias_p,
           fmask, kinit, vinit, uinit, ln_g, ln_b, layer, alpha, name):
    assert seq_len % rows == 0 and rows % BLOCK == 0 and row0 % rows == 0
    steps = seq_len // rows
    blk0 = row0 // rows
    x_spec = pl.BlockSpec((rows, D_MODEL), lambda s, t: (blk0 + s * steps + t, 0))
    seq_spec = lambda shape: pl.BlockSpec((1,) + shape, lambda s, t: (s, 0, 0))
    kern = functools.partial(_mixer_kernel, rows=rows, alpha=alpha, zero_rows=zero_rows,
                             layer=layer)
    return pl.pallas_call(
        kern,
        out_shape=(jax.ShapeDtypeStruct(x.shape, F32),
                   jax.ShapeDtypeStruct((n_seq, BLOCK, D_KV), F32),
                   jax.ShapeDtypeStruct((n_seq, BLOCK, D_KV), F32),
                   jax.ShapeDtypeStruct((n_seq, SUBLANES, D_CONV), F32)),
        grid=(n_seq, steps),
        in_specs=[x_spec,
                  _layer_spec((D_MODEL, D_IN), layer),
                  _layer_spec((D_MODEL, D_MODEL), layer),
                  _layer_spec((CONV_WIDTH, D_CONV), layer),
                  pl.BlockSpec(memory_space=pltpu.SMEM),
                  _const_spec(bias_p.shape), _const_spec(fmask.shape),
                  _const_spec(kinit.shape), _const_spec(vinit.shape), _const_spec(uinit.shape),
                  _layer_spec((1, D_MODEL), layer, 1),
                  _layer_spec((1, D_MODEL), layer, 1)],
        out_specs=(x_spec, seq_spec((BLOCK, D_KV)), seq_spec((BLOCK, D_KV)),
                   seq_spec((SUBLANES, D_CONV))),
        scratch_shapes=[pltpu.VMEM((N_KV_HEADS, BLOCK + rows, D_KV), BF16),
                        pltpu.VMEM((N_KV_HEADS, BLOCK + rows, D_KV), BF16),
                        pltpu.VMEM((SUBLANES + rows, D_CONV), F32),
                        pltpu.VMEM((rows, D_ATTN), BF16)],
        input_output_aliases={0: 0} if in_place else {},
        compiler_params=_params("arbitrary", "arbitrary"),
        name=name,
    )(x, win, wout, convw, sink, bias_p, fmask, kinit, vinit, uinit, ln_g, ln_b)


def _sample_proj_kernel(x_ref, w_ref, state_ref, convw_ref,
                        q_ref, k_ref, v_ref, mix_ref, u_ref, *, n_new):
    z = jnp.dot(x_ref[...].astype(BF16), w_ref[...], preferred_element_type=F32)
    qw = N_HEADS * LANES
    q_ref[...] = z[:, :qw] * Q_SCALE
    k_ref[...] = z[:, qw:qw + D_KV]
    v_ref[...] = z[:, qw + D_KV:qw + 2 * D_KV]
    c0 = qw + 2 * D_KV
    u = z[:, c0 + D_CONV:c0 + 2 * D_CONV] * z[:, c0 + 2 * D_CONV:c0 + 3 * D_CONV]
    u_ref[...] = u
    n = u.shape[0]
    tok = lax.broadcasted_iota(jnp.int32, u.shape, 0) % n_new
    st = state_ref[...]
    um2 = jnp.where(tok < 2, st, pltpu.roll(u, 2, 0))
    um1 = jnp.where(tok < 1, pltpu.roll(st, n - 1, 0), pltpu.roll(u, 1, 0))
    conv = convw_ref[0:1, :] * um2 + convw_ref[1:2, :] * um1 + convw_ref[2:3, :] * u
    mix_ref[...] = z[:, c0:c0 + D_CONV] * conv


def _sample_proj(xs, n, w_exp, state_rows, convw, layer, n_new):
    qw = N_HEADS * LANES
    return pl.pallas_call(
        functools.partial(_sample_proj_kernel, n_new=n_new),
        out_shape=(jax.ShapeDtypeStruct((n, qw), F32),
                   jax.ShapeDtypeStruct((n, D_KV), F32),
                   jax.ShapeDtypeStruct((n, D_KV), F32),
                   jax.ShapeDtypeStruct((n, D_CONV), F32),
                   jax.ShapeDtypeStruct((n, D_CONV), F32)),
        grid=(1,),
        in_specs=[pl.BlockSpec((n, D_MODEL), lambda i: (0, 0)),
                  _layer_spec((D_MODEL, w_exp.shape[2]), layer),
                  _layer_spec((n, D_CONV), layer),
                  _layer_spec((CONV_WIDTH, D_CONV), layer)],
        out_specs=tuple(pl.BlockSpec((n, w), lambda i: (0, 0))
                        for w in (qw, D_KV, D_KV, D_CONV, D_CONV)),
        compiler_params=_params("arbitrary"),
        name="sample_proj",
    )(xs, w_exp, state_rows, convw)


def _sample_attn_kernel(q_ref, ck_ref, cv_ref, kn_ref, vn_ref, bias_ref, sink_ref, *rest, n_new):
    o_ref, ok_ref, ov_ref = rest[-3:]
    qb = q_ref[...].astype(BF16)
    ck = ck_ref[...]
    cv = cv_ref[...]
    kn = kn_ref[...]
    vn = vn_ref[...]
    bias = bias_ref[...]
    sink = sink_ref[...]
    s_c = jnp.einsum("bqd,bkd->bqk", qb, ck.astype(BF16),
                     preferred_element_type=F32) + bias[None, :, :WINDOW]
    qf = qb.astype(F32)
    knf = kn.astype(BF16).astype(F32)
    vnf = vn.astype(BF16).astype(F32)
    s_n = [jnp.sum(qf * knf[:, i:i + 1, :], axis=-1, keepdims=True)
           + bias[None, :, WINDOW + i:WINDOW + i + 1] for i in range(n_new)]
    m = jnp.maximum(jnp.max(s_c, axis=-1, keepdims=True), sink[None])
    for s in s_n:
        m = jnp.maximum(m, s)
    p_c = jnp.exp(s_c - m)
    den = jnp.sum(p_c, axis=-1, keepdims=True) + jnp.exp(sink[None] - m)
    o = jnp.einsum("bqk,bkd->bqd", p_c.astype(BF16), cv.astype(BF16),
                   preferred_element_type=F32)
    for i, s in enumerate(s_n):
        p = jnp.exp(s - m)
        den = den + p
        o = o + p.astype(BF16).astype(F32) * vnf[:, i:i + 1, :]
    o_ref[...] = o / den
    ok_ref[:, 0:WINDOW - n_new, :] = ck_ref[:, n_new:WINDOW, :]
    ok_ref[:, WINDOW - n_new:WINDOW, :] = kn
    ov_ref[:, 0:WINDOW - n_new, :] = cv_ref[:, n_new:WINDOW, :]
    ov_ref[:, WINDOW - n_new:WINDOW, :] = vn


def _sample_attn(q_rows, ck_all, cv_all, kn, vn, bias_s, sink_rows, k_stack, v_stack, layer,
                 n_new):
    depth, n_seq = ck_all.shape[0], ck_all.shape[1]
    assert n_seq % SAMPLE_SEQS == 0
    qr = n_new * N_HEADS
    seq_spec = lambda r, w: pl.BlockSpec((SAMPLE_SEQS, r, w), lambda i: (i, 0, 0))
    lay_spec = pl.BlockSpec((None, SAMPLE_SEQS, WINDOW, D_KV), lambda i: (layer, i, 0, 0))
    stack_shape = jax.ShapeDtypeStruct((depth, n_seq, WINDOW, D_KV), F32)
    in_specs = [seq_spec(qr, LANES), lay_spec, lay_spec,
                seq_spec(n_new, D_KV), seq_spec(n_new, D_KV),
                _layer_spec(bias_s.shape), _layer_spec(sink_rows.shape[1:], layer)]
    args = [q_rows, ck_all, cv_all, kn, vn, bias_s, sink_rows]
    aliases = {}
    if k_stack is not None:
        in_specs += [pl.BlockSpec(memory_space=pl.ANY)] * 2
        aliases = {len(args): 1, len(args) + 1: 2}
        args += [k_stack, v_stack]
    return pl.pallas_call(
        functools.partial(_sample_attn_kernel, n_new=n_new),
        out_shape=(jax.ShapeDtypeStruct((n_seq, qr, LANES), F32), stack_shape, stack_shape),
        grid=(n_seq // SAMPLE_SEQS,),
        in_specs=in_specs,
        out_specs=(seq_spec(qr, LANES), lay_spec, lay_spec),
        input_output_aliases=aliases,
        compiler_params=_params("arbitrary"),
        name="sample_attn",
    )(*args)


def _sample_out_kernel(x_ref, a_ref, mix_ref, woa_ref, woc_ref, g_ref, b_ref, xo_ref, *, alpha):
    y = (jnp.dot(a_ref[...].astype(BF16), woa_ref[...], preferred_element_type=F32)
         + jnp.dot(mix_ref[...].astype(BF16), woc_ref[...], preferred_element_type=F32))
    xo_ref[...] = _layer_norm(alpha * x_ref[...] + y, g_ref[...], b_ref[...])


def _sample_out(xs, n, a_exp, mix, woa_exp, wout, ln_g, ln_b, layer, alpha):
    x_spec = pl.BlockSpec((n, D_MODEL), lambda i: (0, 0))
    return pl.pallas_call(
        functools.partial(_sample_out_kernel, alpha=alpha),
        out_shape=jax.ShapeDtypeStruct(xs.shape, F32),
        grid=(1,),
        in_specs=[x_spec, _const_spec(a_exp.shape), _const_spec(mix.shape),
                  _layer_spec((N_HEADS * LANES, D_MODEL), layer),
                  pl.BlockSpec((None, D_CONV, D_MODEL), lambda i: (layer, D_ATTN // D_CONV, 0),
                               pipeline_mode=pl.Buffered(1)),
                  _layer_spec((1, D_MODEL), layer, 1),
                  _layer_spec((1, D_MODEL), layer, 1)],
        out_specs=x_spec,
        input_output_aliases={0: 0},
        compiler_params=_params("arbitrary"),
        name="sample_out",
    )(xs, a_exp, mix, woa_exp, wout, ln_g, ln_b)


def _expand_q_cols(wq):
    depth, d = wq.shape[0], wq.shape[1]
    w = wq.reshape(depth, d, N_KV_HEADS, GQA_GROUP, HEAD_DIM)
    slabs = []
    for kv in range(N_KV_HEADS):
        parts = [jnp.zeros_like(w[:, :, kv])] * N_KV_HEADS
        parts[kv] = w[:, :, kv]
        slabs.append(jnp.concatenate(parts, axis=-1))
    return jnp.concatenate(slabs, axis=2).reshape(depth, d, N_HEADS * LANES)


def _expand_o_rows(wo):
    depth, d = wo.shape[0], wo.shape[2]
    w = wo.reshape(depth, N_KV_HEADS, GQA_GROUP, HEAD_DIM, d)
    slabs = []
    for kv in range(N_KV_HEADS):
        parts = [jnp.zeros_like(w[:, kv])] * N_KV_HEADS
        parts[kv] = w[:, kv]
        slabs.append(jnp.concatenate(parts, axis=2))
    return jnp.concatenate(slabs, axis=1).reshape(depth, N_HEADS * LANES, d)


def kernel(x_prompt, x_sample, cache_k, cache_v, state_conv, meta_tokens, rel_bias, w_in, conv_w,
           attn_sink, w_out, ffn_w_gate, ffn_w_up, ffn_w_down, ln_g, ln_b):
    depth = w_in.shape[0]
    alpha = float((2 * depth) ** 0.25)
    n_prompt, seq, d_model = x_prompt.shape
    n_sample, n_new, _ = x_sample.shape
    assert d_model == D_MODEL and seq % MIX_ROWS == 0
    assert n_new >= CONV_WIDTH - 1 and cache_k.shape[2] == WINDOW
    rows_p = n_prompt * seq
    rows_s = n_sample * n_new
    assert rows_s % BLOCK == 0

    xp = x_prompt.reshape(rows_p, D_MODEL)
    xs = jnp.concatenate([x_sample.reshape(rows_s, D_MODEL),
                          jnp.zeros((META_PAD, D_MODEL), F32), meta_tokens.astype(F32)], axis=0)

    bias_p, bias_s = _bias_tables(rel_bias, n_new)
    col = jnp.arange(2 * BLOCK)[None, :]
    fmask_prompt = jnp.where(col < META_PAD, NEG_INF, 0.0).astype(F32)
    fmask_meta = jnp.where(col < BLOCK + META_PAD, NEG_INF, 0.0).astype(F32)
    zeros_kv = jnp.zeros((1, BLOCK, D_KV), F32)
    zeros_u = jnp.zeros((1, SUBLANES, D_CONV), F32)

    wg = ffn_w_gate.astype(BF16)
    wu = ffn_w_up.astype(BF16)
    wd = ffn_w_down.astype(BF16)
    win = w_in.astype(BF16)
    wout = w_out.astype(BF16)
    w_exp = jnp.concatenate([_expand_q_cols(win[:, :, :D_ATTN]), win[:, :, D_ATTN:]], axis=2)
    woa_exp = _expand_o_rows(wout[:, :D_ATTN])
    ln_g4 = ln_g.reshape(depth, 3, 1, D_MODEL)
    ln_b4 = ln_b.reshape(depth, 3, 1, D_MODEL)
    ck_all = cache_k.reshape(depth, n_sample, WINDOW, D_KV)
    cv_all = cache_v.reshape(depth, n_sample, WINDOW, D_KV)
    state_rows = jnp.pad(state_conv, ((0, 0), (0, 0), (0, n_new - (CONV_WIDTH - 1)), (0, 0))
                         ).reshape(depth, rows_s, D_CONV)
    sink_rows = jnp.tile(attn_sink, (1, n_new)).reshape(depth, n_new * N_HEADS, 1)

    kp, vp, cp, cs = [], [], [], []
    k_stack = v_stack = None
    for l in range(depth):
        xp, xs = _ffn(xp, xs, wg, wu, wd, ln_g4, ln_b4, l, 0, 0, alpha)

        xs, k_m, v_m, u_m = _mixer(
            xs, rows_s, 1, BLOCK, BLOCK, META_PAD, True, win, wout, conv_w, attn_sink, bias_p,
            fmask_meta, zeros_kv, zeros_kv, zeros_u, ln_g4, ln_b4, l, alpha, "mixer_meta")
        xp, k_p, v_p, u_p = _mixer(
            xp, 0, n_prompt, seq, MIX_ROWS, 0, False, win, wout, conv_w, attn_sink, bias_p,
            fmask_prompt, k_m, v_m, u_m, ln_g4, ln_b4, l, alpha, "mixer_prompt")

        q_s, k_s, v_s, mix_s, u_s = _sample_proj(xs, rows_s, w_exp, state_rows, conv_w, l, n_new)
        o_s, k_stack, v_stack = _sample_attn(
            q_s.reshape(n_sample, n_new * N_HEADS, LANES), ck_all, cv_all,
            k_s.reshape(n_sample, n_new, D_KV), v_s.reshape(n_sample, n_new, D_KV),
            bias_s, sink_rows, k_stack, v_stack, l, n_new)
        xs = _sample_out(xs, rows_s, o_s.reshape(rows_s, N_HEADS * LANES), mix_s, woa_exp, wout,
                         ln_g4, ln_b4, l, alpha)

        xp, xs = _ffn(xp, xs, wg, wu, wd, ln_g4, ln_b4, l, 1, 2, alpha)

        kp.append(k_p)
        vp.append(v_p)
        cp.append(u_p[:, SUBLANES - (CONV_WIDTH - 1):, :])
        cs.append(u_s.reshape(n_sample, n_new, D_CONV)[:, n_new - (CONV_WIDTH - 1):, :])

    kv_shape = (depth, -1, WINDOW, N_KV_HEADS, HEAD_DIM)
    return (xp.reshape(n_prompt, seq, D_MODEL),
            xs[:rows_s].reshape(n_sample, n_new, D_MODEL),
            jnp.stack(kp).reshape(kv_shape), jnp.stack(vp).reshape(kv_shape), jnp.stack(cp),
            k_stack.reshape(kv_shape), v_stack.reshape(kv_shape), jnp.stack(cs))
```

```python
import functools
import math

import jax
import jax.numpy as jnp
from jax import lax
from jax.experimental import pallas as pl
from jax.experimental.pallas import tpu as pltpu

F32 = jnp.float32
BF16 = jnp.bfloat16

D_MODEL = 1024
N_HEADS = 8
N_KV_HEADS = 2
HEAD_DIM = 64
GQA_GROUP = N_HEADS // N_KV_HEADS
D_ATTN = N_HEADS * HEAD_DIM
D_CONV = D_MODEL - D_ATTN
D_KV = N_KV_HEADS * HEAD_DIM
D_IN = D_ATTN + 2 * D_KV + 3 * D_CONV
D_FF = 2816
CONV_WIDTH = 3
WINDOW = 128
BLOCK = 128
N_META = 16
N_BUCKETS = 32
MAX_DISTANCE = 128
LN_EPS = 1e-5
Q_SCALE = HEAD_DIM ** -0.5

V7X_VMEM_LIMIT_BYTES = 56 * 1024 * 1024
V7X_MXU_COLUMNS = 256
SUBLANES = 8
LANES = 128

FFN_ROWS = 1024
FFN_NORM_PARTS = 8
FFN_CHUNK = V7X_MXU_COLUMNS
MIX_ROWS = 512
MIX_NORM_PARTS = 4
SAMPLE_SEQS = 16
META_PAD = BLOCK - N_META

NEG_INF = float("-inf")


def _const_spec(shape):
    nd = len(shape)
    return pl.BlockSpec(shape, lambda *_: (0,) * nd, pipeline_mode=pl.Buffered(1))


def _layer_spec(shape, *lead):
    block = (None,) * len(lead) + tuple(shape)
    idx = tuple(lead) + (0,) * len(shape)
    return pl.BlockSpec(block, lambda *_: idx, pipeline_mode=pl.Buffered(1))


def _params(*sem):
    return pltpu.CompilerParams(dimension_semantics=sem,
                                vmem_limit_bytes=V7X_VMEM_LIMIT_BYTES)


def _layer_norm(r, g, b):
    mu = jnp.mean(r, axis=-1, keepdims=True)
    rc = r - mu
    var = jnp.mean(rc * rc, axis=-1, keepdims=True)
    return rc * lax.rsqrt(var + LN_EPS) * g + b


def _t5_bucket(d):
    d = jnp.maximum(d, 0)
    max_exact = N_BUCKETS // 2
    df = jnp.maximum(d, 1).astype(F32)
    large = max_exact + (jnp.log(df / max_exact) / math.log(MAX_DISTANCE / max_exact)
                         * (N_BUCKETS - max_exact)).astype(jnp.int32)
    large = jnp.minimum(large, N_BUCKETS - 1)
    return jnp.where(d < max_exact, d, large)


def _bias_kernel(tab_ref, bp_ref, bs_ref, *, n_new):
    qi = lax.broadcasted_iota(jnp.int32, (BLOCK, 2 * BLOCK), 0)
    sj = lax.broadcasted_iota(jnp.int32, (BLOCK, 2 * BLOCK), 1)
    d = qi + BLOCK - sj
    valid = (d >= 0) & (d <= WINDOW)
    bk = _t5_bucket(d)
    for h in range(N_HEADS):
        acc = jnp.zeros(d.shape, F32)
        for b in range(N_BUCKETS):
            acc = jnp.where(bk == b, tab_ref[b, h], acc)
        bp_ref[h] = jnp.where(valid, acc, NEG_INF)
    rows, cols = bs_ref.shape
    r = lax.broadcasted_iota(jnp.int32, (rows, cols), 0)
    s = lax.broadcasted_iota(jnp.int32, (rows, cols), 1)
    ds = r // N_HEADS + WINDOW - s
    hs = r % N_HEADS
    valid_s = (ds >= 0) & (ds <= WINDOW) & (s < WINDOW + n_new)
    bks = _t5_bucket(ds)
    acc = jnp.zeros((rows, cols), F32)
    for h in range(N_HEADS):
        for b in range(N_BUCKETS):
            acc = jnp.where((bks == b) & (hs == h), tab_ref[b, h], acc)
    bs_ref[...] = jnp.where(valid_s, acc, NEG_INF)


def _bias_tables(rel_bias, n_new):
    rows = n_new * N_HEADS
    return pl.pallas_call(
        functools.partial(_bias_kernel, n_new=n_new),
        out_shape=(jax.ShapeDtypeStruct((N_HEADS, BLOCK, 2 * BLOCK), F32),
                   jax.ShapeDtypeStruct((rows, WINDOW + SUBLANES), F32)),
        in_specs=[pl.BlockSpec(memory_space=pltpu.SMEM)],
        name="bias_tables",
    )(rel_bias)


def _zero_after(v):
    u = lax.bitcast_convert_type(v, jnp.uint32)
    z = lax.shift_right_logical(lax.shift_right_logical(u, jnp.uint32(16)), jnp.uint32(16))
    return lax.bitcast_convert_type(z, F32)


def _fold_rows(a):
    a = a.reshape(a.shape[0] // 16, 16, a.shape[1]).sum(axis=0)
    return a.reshape(16, a.shape[1] // LANES, LANES).sum(axis=1)


def _ffn_pre_norm(x, wg_ref, wu_ref, wd_ref, act_ref, alpha, side_work):
    rows = x.shape[0]
    xb = x.astype(BF16)
    for ci, c in enumerate(range(0, D_FF, FFN_CHUNK)):
        gate = jnp.dot(xb, wg_ref[:, c:c + FFN_CHUNK], preferred_element_type=F32)
        up = jnp.dot(xb, wu_ref[:, c:c + FFN_CHUNK], preferred_element_type=F32)
        act = jax.nn.silu(gate) * up
        act_ref[0:rows, c:c + FFN_CHUNK] = act.astype(BF16)
        z = side_work(ci)
        if z is not None:
            act_ref[0:16, c:c + LANES] = (act[0:16, 0:LANES] + z).astype(BF16)
    y = jnp.dot(act_ref[0:rows, :], wd_ref[...], preferred_element_type=F32)
    return alpha * x + 0.5 * y


def _ffn_kernel(xp_ref, xs_ref, wg_ref, wu_ref, wd_ref, g_ref, b_ref, op_ref, os_ref,
                act_ref, r_ref, *, alpha, n_big):
    i = pl.program_id(0)
    w = (wg_ref, wu_ref, wd_ref)
    g, b = g_ref[...], b_ref[...]
    part = FFN_ROWS // FFN_NORM_PARTS

    def norm_previous_tile(ci):
        if ci >= FFN_NORM_PARTS:
            return None
        rs = slice(ci * part, (ci + 1) * part)
        out = _layer_norm(r_ref[rs, :], g, b)
        op_ref[rs, :] = out
        return _zero_after(_fold_rows(out))

    @pl.when(i == 0)
    def _():
        r_ref[...] = jnp.zeros(r_ref.shape, F32)

    @pl.when(i < n_big)
    def _():
        r_ref[...] = _ffn_pre_norm(xp_ref[...], *w, act_ref, alpha, norm_previous_tile)

    @pl.when(i == n_big)
    def _():
        os_ref[...] = _ffn_pre_norm(xs_ref[...], *w, act_ref, alpha, norm_previous_tile)

    @pl.when(i == n_big + 1)
    def _():
        os_ref[...] = _layer_norm(os_ref[...], g, b)


def _ffn(xp, xs, wg, wu, wd, ln_g, ln_b, layer, which, ln_idx, alpha):
    assert xp.shape[0] % FFN_ROWS == 0 and D_FF % FFN_CHUNK == 0
    assert FFN_NORM_PARTS <= D_FF // FFN_CHUNK and FFN_ROWS % (16 * FFN_NORM_PARTS) == 0
    n_big = xp.shape[0] // FFN_ROWS
    rows_s = xs.shape[0]
    assert rows_s <= FFN_ROWS
    in_spec = pl.BlockSpec((FFN_ROWS, D_MODEL), lambda i: (jnp.minimum(i, n_big - 1), 0))
    out_spec = pl.BlockSpec((FFN_ROWS, D_MODEL), lambda i: (jnp.clip(i - 1, 0, n_big - 1), 0))
    small_in = pl.BlockSpec((rows_s, D_MODEL), lambda i: (0, 0), pipeline_mode=pl.Buffered(1))
    small_out = pl.BlockSpec((rows_s, D_MODEL), lambda i: (0, 0))
    return pl.pallas_call(
        functools.partial(_ffn_kernel, alpha=alpha, n_big=n_big),
        out_shape=(jax.ShapeDtypeStruct(xp.shape, F32), jax.ShapeDtypeStruct(xs.shape, F32)),
        grid=(n_big + 2,),
        in_specs=[in_spec, small_in,
                  _layer_spec((D_MODEL, D_FF), layer, which),
                  _layer_spec((D_MODEL, D_FF), layer, which),
                  _layer_spec((D_FF, D_MODEL), layer, which),
                  _layer_spec((1, D_MODEL), layer, ln_idx),
                  _layer_spec((1, D_MODEL), layer, ln_idx)],
        out_specs=(out_spec, small_out),
        scratch_shapes=[pltpu.VMEM((FFN_ROWS, D_FF), BF16), pltpu.VMEM((FFN_ROWS, D_MODEL), F32)],
        compiler_params=_params("arbitrary"),
        name="ffn_ln",
    )(xp, xs, wg, wu, wd, ln_g, ln_b)


def _mixer_kernel(x_ref, win_ref, wout_ref, convw_ref, sink_ref, bias_ref, fmask_ref,
                  kinit_ref, vinit_ref, uinit_ref, g_ref, b_ref,
                  xo_ref, klast_ref, vlast_ref, ulast_ref,
                  kd_scr, vd_scr, uscr, ascr, r_scr, *, rows, steps, n_tiles, alpha, zero_rows,
                  layer):
    step = pl.program_id(0)
    g, b = g_ref[...], b_ref[...]

    @pl.when(step == 0)
    def _():
        r_scr[...] = jnp.zeros(r_scr.shape, F32)

    @pl.when(step == n_tiles)
    def _():
        xo_ref[...] = _layer_norm(r_scr[...], g, b)

    @pl.when(step < n_tiles)
    def _():
        _mixer_tile(x_ref, win_ref, wout_ref, convw_ref, sink_ref, bias_ref, fmask_ref,
                    kinit_ref, vinit_ref, uinit_ref, g, b, xo_ref, klast_ref, vlast_ref,
                    ulast_ref, kd_scr, vd_scr, uscr, ascr, r_scr, t=step % steps, rows=rows,
                    alpha=alpha, zero_rows=zero_rows, layer=layer)


def _mixer_tile(x_ref, win_ref, wout_ref, convw_ref, sink_ref, bias_ref, fmask_ref,
                kinit_ref, vinit_ref, uinit_ref, g, b, xo_ref, klast_ref, vlast_ref, ulast_ref,
                kd_scr, vd_scr, uscr, ascr, r_scr, *, t, rows, alpha, zero_rows, layer):
    lane = lax.broadcasted_iota(jnp.int32, (1, LANES), 1)
    low = lane < HEAD_DIM

    def dup_heads(a):
        sw = pltpu.roll(a, HEAD_DIM, 1)
        return (jnp.where(low, a, sw).astype(BF16), jnp.where(low, sw, a).astype(BF16))

    @pl.when(t == 0)
    def _():
        for kv, (kd, vd) in enumerate(zip(dup_heads(kinit_ref[0]), dup_heads(vinit_ref[0]))):
            kd_scr[kv, 0:BLOCK, :] = kd
            vd_scr[kv, 0:BLOCK, :] = vd
        uscr[0:SUBLANES, :] = uinit_ref[0]

    x = x_ref[...]
    if zero_rows:
        ridx = lax.broadcasted_iota(jnp.int32, x.shape, 0)
        x = jnp.where(ridx >= zero_rows, x, 0.0)
    xb = x.astype(BF16)
    c0 = D_ATTN + 2 * D_KV
    zq = jnp.dot(xb, win_ref[:, 0:c0], preferred_element_type=F32)
    zc = jnp.dot(xb, win_ref[:, c0:], preferred_element_type=F32)
    part = rows // MIX_NORM_PARTS
    edges = []
    for ci in range(MIX_NORM_PARTS):
        rs = slice(ci * part, (ci + 1) * part)
        out = _layer_norm(r_scr[rs, :], g, b)
        xo_ref[rs, :] = out
        edges.append(_zero_after(_fold_rows(out)))
    k = zq[:, D_ATTN:D_ATTN + D_KV]
    v = zq[:, D_ATTN + D_KV:c0]
    k = jnp.concatenate([k[0:16] + sum(edges[0::2]), k[16:]], axis=0)
    v = jnp.concatenate([v[0:16] + sum(edges[1::2]), v[16:]], axis=0)
    u = zc[:, D_CONV:2 * D_CONV] * zc[:, 2 * D_CONV:3 * D_CONV]
    klast_ref[0] = k[rows - BLOCK:, :]
    vlast_ref[0] = v[rows - BLOCK:, :]
    ulast_ref[0] = u[rows - SUBLANES:, :]
    for kv, (kd, vd) in enumerate(zip(dup_heads(k), dup_heads(v))):
        kd_scr[kv, BLOCK:BLOCK + rows, :] = kd
        vd_scr[kv, BLOCK:BLOCK + rows, :] = vd
    uscr[SUBLANES:SUBLANES + rows, :] = u

    qs = zq[:, :D_ATTN] * Q_SCALE
    lane_q = lax.broadcasted_iota(jnp.int32, (1, D_ATTN), 1) % LANES
    q_even = jnp.where(lane_q < HEAD_DIM, qs, 0.0).astype(BF16)
    q_odd = jnp.where(lane_q < HEAD_DIM, 0.0, qs).astype(BF16)

    first = jnp.where(t == 0, fmask_ref[...], 0.0)
    ones_cols = jnp.ones((2 * BLOCK, LANES), BF16)
    for j in range(rows // BLOCK):
        r0 = j * BLOCK
        for kv in range(N_KV_HEADS):
            kd = kd_scr[kv, r0:r0 + 2 * BLOCK, :]
            vdx = jnp.concatenate([vd_scr[kv, r0:r0 + 2 * BLOCK, :], ones_cols], axis=1)
            heads = range(kv * GQA_GROUP, (kv + 1) * GQA_GROUP)
            q4 = jnp.concatenate(
                [(q_odd if h % 2 else q_even)[r0:r0 + BLOCK, (h // 2) * LANES:(h // 2 + 1) * LANES]
                 for h in heads], axis=0)
            s4 = lax.dot_general(q4, kd, (((1,), (1,)), ((), ())), preferred_element_type=F32)
            ps, es = [], []
            for g, h in enumerate(heads):
                s = s4[g * BLOCK:(g + 1) * BLOCK] + bias_ref[h]
                if j == 0:
                    s = s + first
                sl, sr = s[:, :LANES], s[:, LANES:]
                sk = sink_ref[layer, h]
                m1 = jnp.max(jnp.maximum(sl, sr), axis=-1, keepdims=True)
                mb = jnp.broadcast_to(jnp.maximum(m1, sk), (BLOCK, LANES))
                ps.append(jnp.concatenate([jnp.exp(sl - mb), jnp.exp(sr - mb)],
                                          axis=1).astype(BF16))
                es.append(jnp.exp(sk - mb))
            ox = jnp.dot(jnp.concatenate(ps, axis=0), vdx, preferred_element_type=F32)
            outs = []
            for g in range(GQA_GROUP):
                og = ox[g * BLOCK:(g + 1) * BLOCK]
                outs.append(og[:, :LANES] / (og[:, LANES:] + es[g]))
            for i in range(GQA_GROUP // 2):
                slab = jnp.where(low, outs[2 * i], outs[2 * i + 1])
                col = (kv * GQA_GROUP // 2 + i) * LANES
                ascr[r0:r0 + BLOCK, col:col + LANES] = slab.astype(BF16)

    um2 = uscr[SUBLANES - 2:SUBLANES - 2 + rows, :]
    um1 = uscr[SUBLANES - 1:SUBLANES - 1 + rows, :]
    conv = convw_ref[0:1, :] * um2 + convw_ref[1:2, :] * um1 + convw_ref[2:3, :] * u
    mix = zc[:, 0:D_CONV] * conv
    y = (jnp.dot(ascr[...], wout_ref[0:D_ATTN, :], preferred_element_type=F32)
         + jnp.dot(mix.astype(BF16), wout_ref[D_ATTN:, :], preferred_element_type=F32))
    r_scr[...] = alpha * x + y

    for kv in range(N_KV_HEADS):
        kd_scr[kv, 0:BLOCK, :] = kd_scr[kv, rows:rows + BLOCK, :]
        vd_scr[kv, 0:BLOCK, :] = vd_scr[kv, rows:rows + BLOCK, :]
    uscr[0:SUBLANES, :] = uscr[rows:rows + SUBLANES, :]


def _mixer(x, row0, n_seq, seq_len, rows, zero_rows, in_place, win, wout, convw, sink, bias_p,
           fmask, kinit, vinit, uinit, ln_g, ln_b, layer, alpha, name):
    assert seq_len % rows == 0 and rows % BLOCK == 0 and row0 % rows == 0
    assert rows % (16 * MIX_NORM_PARTS) == 0
    steps = seq_len // rows
    n_tiles = n_seq * steps
    blk0 = row0 // rows
    x_spec = pl.BlockSpec((rows, D_MODEL), lambda s: (blk0 + jnp.minimum(s, n_tiles - 1), 0))
    xo_spec = pl.BlockSpec((rows, D_MODEL), lambda s: (blk0 + jnp.maximum(s - 1, 0), 0))
    seq_spec = lambda shape: pl.BlockSpec(
        (1,) + shape, lambda s: (jnp.minimum(s // steps, n_seq - 1), 0, 0))
    kern = functools.partial(_mixer_kernel, rows=rows, steps=steps, n_tiles=n_tiles, alpha=alpha,
                             zero_rows=zero_rows, layer=layer)
    return pl.pallas_call(
        kern,
        out_shape=(jax.ShapeDtypeStruct(x.shape, F32),
                   jax.ShapeDtypeStruct((n_seq, BLOCK, D_KV), F32),
                   jax.ShapeDtypeStruct((n_seq, BLOCK, D_KV), F32),
                   jax.ShapeDtypeStruct((n_seq, SUBLANES, D_CONV), F32)),
        grid=(n_tiles + 1,),
        in_specs=[x_spec,
                  _layer_spec((D_MODEL, D_IN), layer),
                  _layer_spec((D_MODEL, D_MODEL), layer),
                  _layer_spec((CONV_WIDTH, D_CONV), layer),
                  pl.BlockSpec(memory_space=pltpu.SMEM),
                  _const_spec(bias_p.shape), _const_spec(fmask.shape),
                  _const_spec(kinit.shape), _const_spec(vinit.shape), _const_spec(uinit.shape),
                  _layer_spec((1, D_MODEL), layer, 1),
                  _layer_spec((1, D_MODEL), layer, 1)],
        out_specs=(xo_spec, seq_spec((BLOCK, D_KV)), seq_spec((BLOCK, D_KV)),
                   seq_spec((SUBLANES, D_CONV))),
        scratch_shapes=[pltpu.VMEM((N_KV_HEADS, BLOCK + rows, D_KV), BF16),
                        pltpu.VMEM((N_KV_HEADS, BLOCK + rows, D_KV), BF16),
                        pltpu.VMEM((SUBLANES + rows, D_CONV), F32),
                        pltpu.VMEM((rows, D_ATTN), BF16),
                        pltpu.VMEM((rows, D_MODEL), F32)],
        input_output_aliases={0: 0} if in_place else {},
        compiler_params=_params("arbitrary"),
        name=name,
    )(x, win, wout, convw, sink, bias_p, fmask, kinit, vinit, uinit, ln_g, ln_b)


def _sample_proj_kernel(x_ref, w_ref, state_ref, convw_ref,
                        q_ref, k_ref, v_ref, mix_ref, u_ref, *, n_new):
    z = jnp.dot(x_ref[...].astype(BF16), w_ref[...], preferred_element_type=F32)
    qw = N_HEADS * LANES
    q_ref[...] = z[:, :qw] * Q_SCALE
    k_ref[...] = z[:, qw:qw + D_KV]
    v_ref[...] = z[:, qw + D_KV:qw + 2 * D_KV]
    c0 = qw + 2 * D_KV
    u = z[:, c0 + D_CONV:c0 + 2 * D_CONV] * z[:, c0 + 2 * D_CONV:c0 + 3 * D_CONV]
    u_ref[...] = u
    n = u.shape[0]
    tok = lax.broadcasted_iota(jnp.int32, u.shape, 0) % n_new
    st = state_ref[...]
    um2 = jnp.where(tok < 2, st, pltpu.roll(u, 2, 0))
    um1 = jnp.where(tok < 1, pltpu.roll(st, n - 1, 0), pltpu.roll(u, 1, 0))
    conv = convw_ref[0:1, :] * um2 + convw_ref[1:2, :] * um1 + convw_ref[2:3, :] * u
    mix_ref[...] = z[:, c0:c0 + D_CONV] * conv


def _sample_proj(xs, n, w_exp, state_rows, convw, layer, n_new):
    qw = N_HEADS * LANES
    return pl.pallas_call(
        functools.partial(_sample_proj_kernel, n_new=n_new),
        out_shape=(jax.ShapeDtypeStruct((n, qw), F32),
                   jax.ShapeDtypeStruct((n, D_KV), F32),
                   jax.ShapeDtypeStruct((n, D_KV), F32),
                   jax.ShapeDtypeStruct((n, D_CONV), F32),
                   jax.ShapeDtypeStruct((n, D_CONV), F32)),
        grid=(1,),
        in_specs=[pl.BlockSpec((n, D_MODEL), lambda i: (0, 0)),
                  _layer_spec((D_MODEL, w_exp.shape[2]), layer),
                  _layer_spec((n, D_CONV), layer),
                  _layer_spec((CONV_WIDTH, D_CONV), layer)],
        out_specs=tuple(pl.BlockSpec((n, w), lambda i: (0, 0))
                        for w in (qw, D_KV, D_KV, D_CONV, D_CONV)),
        compiler_params=_params("arbitrary"),
        name="sample_proj",
    )(xs, w_exp, state_rows, convw)


def _sample_attn_kernel(q_ref, ck_ref, cv_ref, kn_ref, vn_ref, bias_ref, sink_ref,
                        o_ref, ok_ref, ov_ref, *, n_new):
    qb = q_ref[...].astype(BF16)
    ck = ck_ref[...]
    cv = cv_ref[...]
    kn = kn_ref[...]
    vn = vn_ref[...]
    bias = bias_ref[...]
    sink = sink_ref[...]
    s_c = jnp.einsum("bqd,bkd->bqk", qb, ck.astype(BF16),
                     preferred_element_type=F32) + bias[None, :, :WINDOW]
    qf = qb.astype(F32)
    knf = kn.astype(BF16).astype(F32)
    vnf = vn.astype(BF16).astype(F32)
    s_n = [jnp.sum(qf * knf[:, i:i + 1, :], axis=-1, keepdims=True)
           + bias[None, :, WINDOW + i:WINDOW + i + 1] for i in range(n_new)]
    m = jnp.maximum(jnp.max(s_c, axis=-1, keepdims=True), sink[None])
    for s in s_n:
        m = jnp.maximum(m, s)
    p_c = jnp.exp(s_c - m)
    den = jnp.sum(p_c, axis=-1, keepdims=True) + jnp.exp(sink[None] - m)
    o = jnp.einsum("bqk,bkd->bqd", p_c.astype(BF16), cv.astype(BF16),
                   preferred_element_type=F32)
    for i, s in enumerate(s_n):
        p = jnp.exp(s - m)
        den = den + p
        o = o + p.astype(BF16).astype(F32) * vnf[:, i:i + 1, :]
    o_ref[...] = o / den
    ok_ref[:, 0:WINDOW - n_new, :] = ck_ref[:, n_new:WINDOW, :]
    ok_ref[:, WINDOW - n_new:WINDOW, :] = kn
    ov_ref[:, 0:WINDOW - n_new, :] = cv_ref[:, n_new:WINDOW, :]
    ov_ref[:, WINDOW - n_new:WINDOW, :] = vn


def _sample_attn(q_rows, k_buf, v_buf, kn, vn, bias_s, sink_rows, layer, n_new):
    n_seq = k_buf.shape[1]
    assert n_seq % SAMPLE_SEQS == 0
    qr = n_new * N_HEADS
    seq_spec = lambda r, w: pl.BlockSpec((SAMPLE_SEQS, r, w), lambda i: (i, 0, 0))
    lay_spec = pl.BlockSpec((None, SAMPLE_SEQS, WINDOW, D_KV), lambda i: (layer, i, 0, 0))
    return pl.pallas_call(
        functools.partial(_sample_attn_kernel, n_new=n_new),
        out_shape=(jax.ShapeDtypeStruct((n_seq, qr, LANES), F32),
                   jax.ShapeDtypeStruct(k_buf.shape, F32), jax.ShapeDtypeStruct(v_buf.shape, F32)),
        grid=(n_seq // SAMPLE_SEQS,),
        in_specs=[seq_spec(qr, LANES), lay_spec, lay_spec,
                  seq_spec(n_new, D_KV), seq_spec(n_new, D_KV),
                  _layer_spec(bias_s.shape), _layer_spec(sink_rows.shape[1:], layer)],
        out_specs=(seq_spec(qr, LANES), lay_spec, lay_spec),
        input_output_aliases={1: 1, 2: 2},
        compiler_params=_params("arbitrary"),
        name="sample_attn",
    )(q_rows, k_buf, v_buf, kn, vn, bias_s, sink_rows)


def _sample_out_kernel(x_ref, a_ref, mix_ref, woa_ref, woc_ref, g_ref, b_ref, xo_ref, *, alpha):
    y = (jnp.dot(a_ref[...].astype(BF16), woa_ref[...], preferred_element_type=F32)
         + jnp.dot(mix_ref[...].astype(BF16), woc_ref[...], preferred_element_type=F32))
    xo_ref[...] = _layer_norm(alpha * x_ref[...] + y, g_ref[...], b_ref[...])


def _sample_out(xs, n, a_exp, mix, woa_exp, wout, ln_g, ln_b, layer, alpha):
    x_spec = pl.BlockSpec((n, D_MODEL), lambda i: (0, 0))
    return pl.pallas_call(
        functools.partial(_sample_out_kernel, alpha=alpha),
        out_shape=jax.ShapeDtypeStruct(xs.shape, F32),
        grid=(1,),
        in_specs=[x_spec, _const_spec(a_exp.shape), _const_spec(mix.shape),
                  _layer_spec((N_HEADS * LANES, D_MODEL), layer),
                  pl.BlockSpec((None, D_CONV, D_MODEL), lambda i: (layer, D_ATTN // D_CONV, 0),
                               pipeline_mode=pl.Buffered(1)),
                  _layer_spec((1, D_MODEL), layer, 1),
                  _layer_spec((1, D_MODEL), layer, 1)],
        out_specs=x_spec,
        input_output_aliases={0: 0},
        compiler_params=_params("arbitrary"),
        name="sample_out",
    )(xs, a_exp, mix, woa_exp, wout, ln_g, ln_b)


def _expand_q_cols(wq):
    depth, d = wq.shape[0], wq.shape[1]
    w = wq.reshape(depth, d, N_KV_HEADS, GQA_GROUP, HEAD_DIM)
    slabs = []
    for kv in range(N_KV_HEADS):
        parts = [jnp.zeros_like(w[:, :, kv])] * N_KV_HEADS
        parts[kv] = w[:, :, kv]
        slabs.append(jnp.concatenate(parts, axis=-1))
    return jnp.concatenate(slabs, axis=2).reshape(depth, d, N_HEADS * LANES)


def _expand_o_rows(wo):
    depth, d = wo.shape[0], wo.shape[2]
    w = wo.reshape(depth, N_KV_HEADS, GQA_GROUP, HEAD_DIM, d)
    slabs = []
    for kv in range(N_KV_HEADS):
        parts = [jnp.zeros_like(w[:, kv])] * N_KV_HEADS
        parts[kv] = w[:, kv]
        slabs.append(jnp.concatenate(parts, axis=2))
    return jnp.concatenate(slabs, axis=1).reshape(depth, N_HEADS * LANES, d)


def kernel(x_prompt, x_sample, cache_k, cache_v, state_conv, meta_tokens, rel_bias, w_in, conv_w,
           attn_sink, w_out, ffn_w_gate, ffn_w_up, ffn_w_down, ln_g, ln_b):
    depth = w_in.shape[0]
    alpha = float((2 * depth) ** 0.25)
    n_prompt, seq, d_model = x_prompt.shape
    n_sample, n_new, _ = x_sample.shape
    assert d_model == D_MODEL and seq % MIX_ROWS == 0
    assert n_new >= CONV_WIDTH - 1 and cache_k.shape[2] == WINDOW
    rows_p = n_prompt * seq
    rows_s = n_sample * n_new
    assert rows_s % BLOCK == 0

    xp = x_prompt.reshape(rows_p, D_MODEL)
    xs = jnp.concatenate([x_sample.reshape(rows_s, D_MODEL),
                          jnp.zeros((META_PAD, D_MODEL), F32), meta_tokens.astype(F32)], axis=0)

    bias_p, bias_s = _bias_tables(rel_bias, n_new)
    col = jnp.arange(2 * BLOCK)[None, :]
    fmask_prompt = jnp.where(col < META_PAD, NEG_INF, 0.0).astype(F32)
    fmask_meta = jnp.where(col < BLOCK + META_PAD, NEG_INF, 0.0).astype(F32)
    zeros_kv = jnp.zeros((1, BLOCK, D_KV), F32)
    zeros_u = jnp.zeros((1, SUBLANES, D_CONV), F32)

    wg = ffn_w_gate.astype(BF16)
    wu = ffn_w_up.astype(BF16)
    wd = ffn_w_down.astype(BF16)
    win = w_in.astype(BF16)
    wout = w_out.astype(BF16)
    w_exp = jnp.concatenate([_expand_q_cols(win[:, :, :D_ATTN]), win[:, :, D_ATTN:]], axis=2)
    woa_exp = _expand_o_rows(wout[:, :D_ATTN])
    ln_g4 = ln_g.reshape(depth, 3, 1, D_MODEL)
    ln_b4 = ln_b.reshape(depth, 3, 1, D_MODEL)
    k_buf = cache_k.reshape(depth, n_sample, WINDOW, D_KV)
    v_buf = cache_v.reshape(depth, n_sample, WINDOW, D_KV)
    state_rows = jnp.pad(state_conv, ((0, 0), (0, 0), (0, n_new - (CONV_WIDTH - 1)), (0, 0))
                         ).reshape(depth, rows_s, D_CONV)
    sink_rows = jnp.tile(attn_sink, (1, n_new)).reshape(depth, n_new * N_HEADS, 1)

    kp, vp, cp, cs = [], [], [], []
    for l in range(depth):
        xp, xs = _ffn(xp, xs, wg, wu, wd, ln_g4, ln_b4, l, 0, 0, alpha)

        xs, k_m, v_m, u_m = _mixer(
            xs, rows_s, 1, BLOCK, BLOCK, META_PAD, True, win, wout, conv_w, attn_sink, bias_p,
            fmask_meta, zeros_kv, zeros_kv, zeros_u, ln_g4, ln_b4, l, alpha, "mixer_meta")
        xp, k_p, v_p, u_p = _mixer(
            xp, 0, n_prompt, seq, MIX_ROWS, 0, False, win, wout, conv_w, attn_sink, bias_p,
            fmask_prompt, k_m, v_m, u_m, ln_g4, ln_b4, l, alpha, "mixer_prompt")

        q_s, k_s, v_s, mix_s, u_s = _sample_proj(xs, rows_s, w_exp, state_rows, conv_w, l, n_new)
        o_s, k_buf, v_buf = _sample_attn(
            q_s.reshape(n_sample, n_new * N_HEADS, LANES), k_buf, v_buf,
            k_s.reshape(n_sample, n_new, D_KV), v_s.reshape(n_sample, n_new, D_KV),
            bias_s, sink_rows, l, n_new)
        xs = _sample_out(xs, rows_s, o_s.reshape(rows_s, N_HEADS * LANES), mix_s, woa_exp, wout,
                         ln_g4, ln_b4, l, alpha)

        xp, xs = _ffn(xp, xs, wg, wu, wd, ln_g4, ln_b4, l, 1, 2, alpha)

        kp.append(k_p)
        vp.append(v_p)
        cp.append(u_p[:, SUBLANES - (CONV_WIDTH - 1):, :])
        cs.append(u_s.reshape(n_sample, n_new, D_CONV)[:, n_new - (CONV_WIDTH - 1):, :])

    kv_shape = (depth, -1, WINDOW, N_KV_HEADS, HEAD_DIM)
    return (xp.reshape(n_prompt, seq, D_MODEL),
            xs[:rows_s].reshape(n_sample, n_new, D_MODEL),
            jnp.stack(kp).reshape(kv_shape), jnp.stack(vp).reshape(kv_shape), jnp.stack(cp),
            k_buf.reshape(kv_shape), v_buf.reshape(kv_shape), jnp.stack(cs))
```

```python
import functools
import math

import jax
import jax.numpy as jnp
from jax import lax
from jax.experimental import pallas as pl
from jax.experimental.pallas import tpu as pltpu

F32 = jnp.float32
BF16 = jnp.bfloat16

D_MODEL = 1024
N_HEADS = 8
N_KV_HEADS = 2
HEAD_DIM = 64
GQA_GROUP = N_HEADS // N_KV_HEADS
D_ATTN = N_HEADS * HEAD_DIM
D_CONV = D_MODEL - D_ATTN
D_KV = N_KV_HEADS * HEAD_DIM
D_IN = D_ATTN + 2 * D_KV + 3 * D_CONV
D_FF = 2816
CONV_WIDTH = 3
WINDOW = 128
BLOCK = 128
N_META = 16
N_BUCKETS = 32
MAX_DISTANCE = 128
LN_EPS = 1e-5
Q_SCALE = HEAD_DIM ** -0.5

V7X_VMEM_LIMIT_BYTES = 56 * 1024 * 1024
V7X_MXU_COLUMNS = 256
SUBLANES = 8
LANES = 128

FFN_ROWS = 1024
FFN_NORM_PARTS = 8
FFN_CHUNK = V7X_MXU_COLUMNS
MIX_ROWS = 512
MIX_NORM_PARTS = 4
SAMPLE_SEQS = 16
META_PAD = BLOCK - N_META

NEG_INF = float("-inf")


def _const_spec(shape):
    nd = len(shape)
    return pl.BlockSpec(shape, lambda *_: (0,) * nd, pipeline_mode=pl.Buffered(1))


def _layer_spec(shape, *lead):
    block = (None,) * len(lead) + tuple(shape)
    idx = tuple(lead) + (0,) * len(shape)
    return pl.BlockSpec(block, lambda *_: idx, pipeline_mode=pl.Buffered(1))


def _params(*sem):
    return pltpu.CompilerParams(dimension_semantics=sem,
                                vmem_limit_bytes=V7X_VMEM_LIMIT_BYTES)


def _layer_norm(r, g, b):
    mu = jnp.mean(r, axis=-1, keepdims=True)
    rc = r - mu
    var = jnp.mean(rc * rc, axis=-1, keepdims=True)
    return rc * lax.rsqrt(var + LN_EPS) * g + b


def _t5_bucket(d):
    d = jnp.maximum(d, 0)
    max_exact = N_BUCKETS // 2
    df = jnp.maximum(d, 1).astype(F32)
    large = max_exact + (jnp.log(df / max_exact) / math.log(MAX_DISTANCE / max_exact)
                         * (N_BUCKETS - max_exact)).astype(jnp.int32)
    large = jnp.minimum(large, N_BUCKETS - 1)
    return jnp.where(d < max_exact, d, large)


def _bias_kernel(tab_ref, bp_ref, bs_ref, *, n_new):
    qi = lax.broadcasted_iota(jnp.int32, (BLOCK, 2 * BLOCK), 0)
    sj = lax.broadcasted_iota(jnp.int32, (BLOCK, 2 * BLOCK), 1)
    d = qi + BLOCK - sj
    valid = (d >= 0) & (d <= WINDOW)
    bk = _t5_bucket(d)
    for h in range(N_HEADS):
        acc = jnp.zeros(d.shape, F32)
        for b in range(N_BUCKETS):
            acc = jnp.where(bk == b, tab_ref[b, h], acc)
        bp_ref[h] = jnp.where(valid, acc, NEG_INF)
    rows, cols = bs_ref.shape
    r = lax.broadcasted_iota(jnp.int32, (rows, cols), 0)
    s = lax.broadcasted_iota(jnp.int32, (rows, cols), 1)
    ds = r // N_HEADS + WINDOW - s
    hs = r % N_HEADS
    valid_s = (ds >= 0) & (ds <= WINDOW) & (s < WINDOW + n_new)
    bks = _t5_bucket(ds)
    acc = jnp.zeros((rows, cols), F32)
    for h in range(N_HEADS):
        for b in range(N_BUCKETS):
            acc = jnp.where((bks == b) & (hs == h), tab_ref[b, h], acc)
    bs_ref[...] = jnp.where(valid_s, acc, NEG_INF)


def _bias_tables(rel_bias, n_new):
    rows = n_new * N_HEADS
    return pl.pallas_call(
        functools.partial(_bias_kernel, n_new=n_new),
        out_shape=(jax.ShapeDtypeStruct((N_HEADS, BLOCK, 2 * BLOCK), F32),
                   jax.ShapeDtypeStruct((rows, WINDOW + SUBLANES), F32)),
        in_specs=[pl.BlockSpec(memory_space=pltpu.SMEM)],
        name="bias_tables",
    )(rel_bias)


def _zero_after(v):
    u = lax.bitcast_convert_type(v, jnp.uint32)
    z = lax.shift_right_logical(lax.shift_right_logical(u, jnp.uint32(16)), jnp.uint32(16))
    return lax.bitcast_convert_type(z, F32)


def _fold_rows(a):
    a = a.reshape(a.shape[0] // 16, 16, a.shape[1]).sum(axis=0)
    return a.reshape(16, a.shape[1] // LANES, LANES).sum(axis=1)


def _ffn_pre_norm(x, wg_ref, wu_ref, wd_ref, act_ref, alpha, side_work):
    rows = x.shape[0]
    xb = x.astype(BF16)
    for ci, c in enumerate(range(0, D_FF, FFN_CHUNK)):
        gate = jnp.dot(xb, wg_ref[:, c:c + FFN_CHUNK], preferred_element_type=F32)
        up = jnp.dot(xb, wu_ref[:, c:c + FFN_CHUNK], preferred_element_type=F32)
        act = jax.nn.silu(gate) * up
        act_ref[0:rows, c:c + FFN_CHUNK] = act.astype(BF16)
        z = side_work(ci)
        if z is not None:
            act_ref[0:16, c:c + LANES] = (act[0:16, 0:LANES] + z).astype(BF16)
    y = jnp.dot(act_ref[0:rows, :], wd_ref[...], preferred_element_type=F32)
    return alpha * x + 0.5 * y


def _ffn_kernel(xp_ref, xs_ref, wg_ref, wu_ref, wd_ref, g_ref, b_ref, *rest, alpha, n_big,
                convert):
    if convert:
        nxt_refs, rest = rest[:3], rest[3:]
        op_ref, os_ref = rest[:2]
        cvt_refs, (act_ref, r_ref) = rest[2:5], rest[5:]
    else:
        op_ref, os_ref, act_ref, r_ref = rest
    i = pl.program_id(0)
    w = (wg_ref, wu_ref, wd_ref)
    g, b = g_ref[...], b_ref[...]
    part = FFN_ROWS // FFN_NORM_PARTS

    def norm_previous_tile(ci):
        if ci >= FFN_NORM_PARTS:
            return None
        rs = slice(ci * part, (ci + 1) * part)
        out = _layer_norm(r_ref[rs, :], g, b)
        op_ref[rs, :] = out
        return _zero_after(_fold_rows(out))

    def norm_and_convert(ci):
        k = ci - FFN_NORM_PARTS
        if convert and 0 <= k < len(cvt_refs):
            cvt_refs[k][...] = nxt_refs[k][...].astype(BF16)
            return None
        return norm_previous_tile(ci)

    @pl.when(i == 0)
    def _():
        r_ref[...] = jnp.zeros(r_ref.shape, F32)

    @pl.when(i < n_big)
    def _():
        r_ref[...] = _ffn_pre_norm(xp_ref[...], *w, act_ref, alpha, norm_and_convert)

    @pl.when(i == n_big)
    def _():
        os_ref[...] = _ffn_pre_norm(xs_ref[...], *w, act_ref, alpha, norm_previous_tile)

    @pl.when(i == n_big + 1)
    def _():
        os_ref[...] = _layer_norm(os_ref[...], g, b)


def _ffn(xp, xs, w_cur, w_next, ln_g, ln_b, layer, ln_idx, alpha):
    assert xp.shape[0] % FFN_ROWS == 0 and D_FF % FFN_CHUNK == 0
    assert FFN_NORM_PARTS + 3 <= D_FF // FFN_CHUNK and FFN_ROWS % (16 * FFN_NORM_PARTS) == 0
    n_big = xp.shape[0] // FFN_ROWS
    rows_s = xs.shape[0]
    assert rows_s <= FFN_ROWS
    last = lambda i: jnp.minimum(i, n_big - 1)
    in_spec = pl.BlockSpec((FFN_ROWS, D_MODEL), lambda i: (last(i), 0))
    out_spec = pl.BlockSpec((FFN_ROWS, D_MODEL), lambda i: (jnp.clip(i - 1, 0, n_big - 1), 0))
    small_in = pl.BlockSpec((rows_s, D_MODEL), lambda i: (0, 0), pipeline_mode=pl.Buffered(1))
    small_out = pl.BlockSpec((rows_s, D_MODEL), lambda i: (0, 0), pipeline_mode=pl.Buffered(1))
    in_specs = [in_spec, small_in] + [_const_spec(a.shape) for a in w_cur] + [
        _layer_spec((1, D_MODEL), layer, ln_idx), _layer_spec((1, D_MODEL), layer, ln_idx)]
    out_specs = [out_spec, small_out]
    out_shape = [jax.ShapeDtypeStruct(xp.shape, F32), jax.ShapeDtypeStruct(xs.shape, F32)]
    args = [xp, xs, *w_cur, ln_g, ln_b]
    if w_next is not None:
        *w_f32, nl, nw = w_next
        for a in w_f32:
            r, c = a.shape[2] // n_big, a.shape[3]
            assert a.shape[2] % (16 * n_big) == 0
            in_specs.append(pl.BlockSpec((None, None, r, c), lambda i: (nl, nw, last(i), 0)))
            out_specs.append(pl.BlockSpec((r, c), lambda i: (last(i), 0)))
            out_shape.append(jax.ShapeDtypeStruct(a.shape[2:], BF16))
        args += w_f32
    return pl.pallas_call(
        functools.partial(_ffn_kernel, alpha=alpha, n_big=n_big, convert=w_next is not None),
        out_shape=out_shape,
        grid=(n_big + 2,),
        in_specs=in_specs,
        out_specs=out_specs,
        scratch_shapes=[pltpu.VMEM((FFN_ROWS, D_FF), BF16), pltpu.VMEM((FFN_ROWS, D_MODEL), F32)],
        compiler_params=_params("arbitrary"),
        name="ffn_ln",
    )(*args)


def _mixer_kernel(x_ref, win_ref, wout_ref, convw_ref, sink_ref, bias_ref, fmask_ref,
                  kinit_ref, vinit_ref, uinit_ref, g_ref, b_ref,
                  xo_ref, klast_ref, vlast_ref, ulast_ref,
                  kd_scr, vd_scr, uscr, ascr, r_scr, *, rows, steps, n_tiles, alpha, zero_rows,
                  layer):
    step = pl.program_id(0)
    g, b = g_ref[...], b_ref[...]

    @pl.when(step == 0)
    def _():
        r_scr[...] = jnp.zeros(r_scr.shape, F32)

    @pl.when(step == n_tiles)
    def _():
        xo_ref[...] = _layer_norm(r_scr[...], g, b)

    @pl.when(step < n_tiles)
    def _():
        _mixer_tile(x_ref, win_ref, wout_ref, convw_ref, sink_ref, bias_ref, fmask_ref,
                    kinit_ref, vinit_ref, uinit_ref, g, b, xo_ref, klast_ref, vlast_ref,
                    ulast_ref, kd_scr, vd_scr, uscr, ascr, r_scr, t=step % steps, rows=rows,
                    alpha=alpha, zero_rows=zero_rows, layer=layer)


def _mixer_tile(x_ref, win_ref, wout_ref, convw_ref, sink_ref, bias_ref, fmask_ref,
                kinit_ref, vinit_ref, uinit_ref, g, b, xo_ref, klast_ref, vlast_ref, ulast_ref,
                kd_scr, vd_scr, uscr, ascr, r_scr, *, t, rows, alpha, zero_rows, layer):
    lane = lax.broadcasted_iota(jnp.int32, (1, LANES), 1)
    low = lane < HEAD_DIM

    def dup_heads(a):
        sw = pltpu.roll(a, HEAD_DIM, 1)
        return (jnp.where(low, a, sw).astype(BF16), jnp.where(low, sw, a).astype(BF16))

    @pl.when(t == 0)
    def _():
        for kv, (kd, vd) in enumerate(zip(dup_heads(kinit_ref[0]), dup_heads(vinit_ref[0]))):
            kd_scr[kv, 0:BLOCK, :] = kd
            vd_scr[kv, 0:BLOCK, :] = vd
        uscr[0:SUBLANES, :] = uinit_ref[0]

    x = x_ref[...]
    if zero_rows:
        ridx = lax.broadcasted_iota(jnp.int32, x.shape, 0)
        x = jnp.where(ridx >= zero_rows, x, 0.0)
    xb = x.astype(BF16)
    c0 = D_ATTN + 2 * D_KV
    zq = jnp.dot(xb, win_ref[:, 0:c0], preferred_element_type=F32)
    zc = jnp.dot(xb, win_ref[:, c0:], preferred_element_type=F32)
    part = rows // MIX_NORM_PARTS
    edges = []
    for ci in range(MIX_NORM_PARTS):
        rs = slice(ci * part, (ci + 1) * part)
        out = _layer_norm(r_scr[rs, :], g, b)
        xo_ref[rs, :] = out
        edges.append(_zero_after(_fold_rows(out)))
    k = zq[:, D_ATTN:D_ATTN + D_KV]
    v = zq[:, D_ATTN + D_KV:c0]
    k = jnp.concatenate([k[0:16] + sum(edges[0::2]), k[16:]], axis=0)
    v = jnp.concatenate([v[0:16] + sum(edges[1::2]), v[16:]], axis=0)
    u = zc[:, D_CONV:2 * D_CONV] * zc[:, 2 * D_CONV:3 * D_CONV]
    klast_ref[0] = k[rows - BLOCK:, :]
    vlast_ref[0] = v[rows - BLOCK:, :]
    ulast_ref[0] = u[rows - SUBLANES:, :]
    for kv, (kd, vd) in enumerate(zip(dup_heads(k), dup_heads(v))):
        kd_scr[kv, BLOCK:BLOCK + rows, :] = kd
        vd_scr[kv, BLOCK:BLOCK + rows, :] = vd
    uscr[SUBLANES:SUBLANES + rows, :] = u

    qs = zq[:, :D_ATTN] * Q_SCALE
    lane_q = lax.broadcasted_iota(jnp.int32, (1, D_ATTN), 1) % LANES
    q_even = jnp.where(lane_q < HEAD_DIM, qs, 0.0).astype(BF16)
    q_odd = jnp.where(lane_q < HEAD_DIM, 0.0, qs).astype(BF16)

    first = jnp.where(t == 0, fmask_ref[...], 0.0)
    ones_cols = jnp.ones((2 * BLOCK, LANES), BF16)
    for j in range(rows // BLOCK):
        r0 = j * BLOCK
        for kv in range(N_KV_HEADS):
            kd = kd_scr[kv, r0:r0 + 2 * BLOCK, :]
            vdx = jnp.concatenate([vd_scr[kv, r0:r0 + 2 * BLOCK, :], ones_cols], axis=1)
            heads = range(kv * GQA_GROUP, (kv + 1) * GQA_GROUP)
            q4 = jnp.concatenate(
                [(q_odd if h % 2 else q_even)[r0:r0 + BLOCK, (h // 2) * LANES:(h // 2 + 1) * LANES]
                 for h in heads], axis=0)
            s4 = lax.dot_general(q4, kd, (((1,), (1,)), ((), ())), preferred_element_type=F32)
            ps, es = [], []
            for g, h in enumerate(heads):
                s = s4[g * BLOCK:(g + 1) * BLOCK] + bias_ref[h]
                if j == 0:
                    s = s + first
                sl, sr = s[:, :LANES], s[:, LANES:]
                sk = sink_ref[layer, h]
                m1 = jnp.max(jnp.maximum(sl, sr), axis=-1, keepdims=True)
                mb = jnp.broadcast_to(jnp.maximum(m1, sk), (BLOCK, LANES))
                ps.append(jnp.concatenate([jnp.exp(sl - mb), jnp.exp(sr - mb)],
                                          axis=1).astype(BF16))
                es.append(jnp.exp(sk - mb))
            ox = jnp.dot(jnp.concatenate(ps, axis=0), vdx, preferred_element_type=F32)
            outs = []
            for g in range(GQA_GROUP):
                og = ox[g * BLOCK:(g + 1) * BLOCK]
                outs.append(og[:, :LANES] / (og[:, LANES:] + es[g]))
            for i in range(GQA_GROUP // 2):
                slab = jnp.where(low, outs[2 * i], outs[2 * i + 1])
                col = (kv * GQA_GROUP // 2 + i) * LANES
                ascr[r0:r0 + BLOCK, col:col + LANES] = slab.astype(BF16)

    um2 = uscr[SUBLANES - 2:SUBLANES - 2 + rows, :]
    um1 = uscr[SUBLANES - 1:SUBLANES - 1 + rows, :]
    conv = convw_ref[0:1, :] * um2 + convw_ref[1:2, :] * um1 + convw_ref[2:3, :] * u
    mix = zc[:, 0:D_CONV] * conv
    y = (jnp.dot(ascr[...], wout_ref[0:D_ATTN, :], preferred_element_type=F32)
         + jnp.dot(mix.astype(BF16), wout_ref[D_ATTN:, :], preferred_element_type=F32))
    r_scr[...] = alpha * x + y

    for kv in range(N_KV_HEADS):
        kd_scr[kv, 0:BLOCK, :] = kd_scr[kv, rows:rows + BLOCK, :]
        vd_scr[kv, 0:BLOCK, :] = vd_scr[kv, rows:rows + BLOCK, :]
    uscr[0:SUBLANES, :] = uscr[rows:rows + SUBLANES, :]


def _mixer(x, row0, n_seq, seq_len, rows, zero_rows, in_place, win, wout, convw, sink, bias_p,
           fmask, kinit, vinit, uinit, ln_g, ln_b, layer, alpha, name):
    assert seq_len % rows == 0 and rows % BLOCK == 0 and row0 % rows == 0
    assert rows % (16 * MIX_NORM_PARTS) == 0
    steps = seq_len // rows
    n_tiles = n_seq * steps
    blk0 = row0 // rows
    x_spec = pl.BlockSpec((rows, D_MODEL), lambda s: (blk0 + jnp.minimum(s, n_tiles - 1), 0))
    xo_spec = pl.BlockSpec((rows, D_MODEL), lambda s: (blk0 + jnp.maximum(s - 1, 0), 0))
    seq_spec = lambda shape: pl.BlockSpec(
        (1,) + shape, lambda s: (jnp.minimum(s // steps, n_seq - 1), 0, 0))
    kern = functools.partial(_mixer_kernel, rows=rows, steps=steps, n_tiles=n_tiles, alpha=alpha,
                             zero_rows=zero_rows, layer=layer)
    return pl.pallas_call(
        kern,
        out_shape=(jax.ShapeDtypeStruct(x.shape, F32),
                   jax.ShapeDtypeStruct((n_seq, BLOCK, D_KV), F32),
                   jax.ShapeDtypeStruct((n_seq, BLOCK, D_KV), F32),
                   jax.ShapeDtypeStruct((n_seq, SUBLANES, D_CONV), F32)),
        grid=(n_tiles + 1,),
        in_specs=[x_spec,
                  _layer_spec((D_MODEL, D_IN), layer),
                  _layer_spec((D_MODEL, D_MODEL), layer),
                  _layer_spec((CONV_WIDTH, D_CONV), layer),
                  pl.BlockSpec(memory_space=pltpu.SMEM),
                  _const_spec(bias_p.shape), _const_spec(fmask.shape),
                  _const_spec(kinit.shape), _const_spec(vinit.shape), _const_spec(uinit.shape),
                  _layer_spec((1, D_MODEL), layer, 1),
                  _layer_spec((1, D_MODEL), layer, 1)],
        out_specs=(xo_spec, seq_spec((BLOCK, D_KV)), seq_spec((BLOCK, D_KV)),
                   seq_spec((SUBLANES, D_CONV))),
        scratch_shapes=[pltpu.VMEM((N_KV_HEADS, BLOCK + rows, D_KV), BF16),
                        pltpu.VMEM((N_KV_HEADS, BLOCK + rows, D_KV), BF16),
                        pltpu.VMEM((SUBLANES + rows, D_CONV), F32),
                        pltpu.VMEM((rows, D_ATTN), BF16),
                        pltpu.VMEM((rows, D_MODEL), F32)],
        input_output_aliases={0: 0} if in_place else {},
        compiler_params=_params("arbitrary"),
        name=name,
    )(x, win, wout, convw, sink, bias_p, fmask, kinit, vinit, uinit, ln_g, ln_b)


def _sample_proj_kernel(x_ref, w_ref, state_ref, convw_ref,
                        q_ref, k_ref, v_ref, mix_ref, u_ref, *, n_new):
    z = jnp.dot(x_ref[...].astype(BF16), w_ref[...], preferred_element_type=F32)
    qw = N_HEADS * LANES
    q_ref[...] = z[:, :qw] * Q_SCALE
    k_ref[...] = z[:, qw:qw + D_KV]
    v_ref[...] = z[:, qw + D_KV:qw + 2 * D_KV]
    c0 = qw + 2 * D_KV
    u = z[:, c0 + D_CONV:c0 + 2 * D_CONV] * z[:, c0 + 2 * D_CONV:c0 + 3 * D_CONV]
    u_ref[...] = u
    n = u.shape[0]
    tok = lax.broadcasted_iota(jnp.int32, u.shape, 0) % n_new
    st = state_ref[...]
    um2 = jnp.where(tok < 2, st, pltpu.roll(u, 2, 0))
    um1 = jnp.where(tok < 1, pltpu.roll(st, n - 1, 0), pltpu.roll(u, 1, 0))
    conv = convw_ref[0:1, :] * um2 + convw_ref[1:2, :] * um1 + convw_ref[2:3, :] * u
    mix_ref[...] = z[:, c0:c0 + D_CONV] * conv


def _sample_proj(xs, n, w_exp, state_rows, convw, layer, n_new):
    qw = N_HEADS * LANES
    return pl.pallas_call(
        functools.partial(_sample_proj_kernel, n_new=n_new),
        out_shape=(jax.ShapeDtypeStruct((n, qw), F32),
                   jax.ShapeDtypeStruct((n, D_KV), F32),
                   jax.ShapeDtypeStruct((n, D_KV), F32),
                   jax.ShapeDtypeStruct((n, D_CONV), F32),
                   jax.ShapeDtypeStruct((n, D_CONV), F32)),
        grid=(1,),
        in_specs=[pl.BlockSpec((n, D_MODEL), lambda i: (0, 0)),
                  _layer_spec((D_MODEL, w_exp.shape[2]), layer),
                  _layer_spec((n, D_CONV), layer),
                  _layer_spec((CONV_WIDTH, D_CONV), layer)],
        out_specs=tuple(pl.BlockSpec((n, w), lambda i: (0, 0))
                        for w in (qw, D_KV, D_KV, D_CONV, D_CONV)),
        compiler_params=_params("arbitrary"),
        name="sample_proj",
    )(xs, w_exp, state_rows, convw)


def _sample_attn_kernel(q_ref, ck_ref, cv_ref, kn_ref, vn_ref, bias_ref, sink_ref,
                        o_ref, ok_ref, ov_ref, *, n_new):
    qb = q_ref[...].astype(BF16)
    ck = ck_ref[...]
    cv = cv_ref[...]
    kn = kn_ref[...]
    vn = vn_ref[...]
    bias = bias_ref[...]
    sink = sink_ref[...]
    s_c = jnp.einsum("bqd,bkd->bqk", qb, ck.astype(BF16),
                     preferred_element_type=F32) + bias[None, :, :WINDOW]
    qf = qb.astype(F32)
    knf = kn.astype(BF16).astype(F32)
    vnf = vn.astype(BF16).astype(F32)
    s_n = [jnp.sum(qf * knf[:, i:i + 1, :], axis=-1, keepdims=True)
           + bias[None, :, WINDOW + i:WINDOW + i + 1] for i in range(n_new)]
    m = jnp.maximum(jnp.max(s_c, axis=-1, keepdims=True), sink[None])
    for s in s_n:
        m = jnp.maximum(m, s)
    p_c = jnp.exp(s_c - m)
    den = jnp.sum(p_c, axis=-1, keepdims=True) + jnp.exp(sink[None] - m)
    o = jnp.einsum("bqk,bkd->bqd", p_c.astype(BF16), cv.astype(BF16),
                   preferred_element_type=F32)
    for i, s in enumerate(s_n):
        p = jnp.exp(s - m)
        den = den + p
        o = o + p.astype(BF16).astype(F32) * vnf[:, i:i + 1, :]
    o_ref[...] = o / den
    ok_ref[:, 0:WINDOW - n_new, :] = ck_ref[:, n_new:WINDOW, :]
    ok_ref[:, WINDOW - n_new:WINDOW, :] = kn
    ov_ref[:, 0:WINDOW - n_new, :] = cv_ref[:, n_new:WINDOW, :]
    ov_ref[:, WINDOW - n_new:WINDOW, :] = vn


def _sample_attn(q_rows, k_buf, v_buf, kn, vn, bias_s, sink_rows, layer, n_new):
    n_seq = k_buf.shape[1]
    assert n_seq % SAMPLE_SEQS == 0
    qr = n_new * N_HEADS
    seq_spec = lambda r, w: pl.BlockSpec((SAMPLE_SEQS, r, w), lambda i: (i, 0, 0))
    lay_spec = pl.BlockSpec((None, SAMPLE_SEQS, WINDOW, D_KV), lambda i: (layer, i, 0, 0))
    return pl.pallas_call(
        functools.partial(_sample_attn_kernel, n_new=n_new),
        out_shape=(jax.ShapeDtypeStruct((n_seq, qr, LANES), F32),
                   jax.ShapeDtypeStruct(k_buf.shape, F32), jax.ShapeDtypeStruct(v_buf.shape, F32)),
        grid=(n_seq // SAMPLE_SEQS,),
        in_specs=[seq_spec(qr, LANES), lay_spec, lay_spec,
                  seq_spec(n_new, D_KV), seq_spec(n_new, D_KV),
                  _layer_spec(bias_s.shape), _layer_spec(sink_rows.shape[1:], layer)],
        out_specs=(seq_spec(qr, LANES), lay_spec, lay_spec),
        input_output_aliases={1: 1, 2: 2},
        compiler_params=_params("arbitrary"),
        name="sample_attn",
    )(q_rows, k_buf, v_buf, kn, vn, bias_s, sink_rows)


def _sample_out_kernel(x_ref, a_ref, mix_ref, woa_ref, woc_ref, g_ref, b_ref, xo_ref, *, alpha):
    y = (jnp.dot(a_ref[...].astype(BF16), woa_ref[...], preferred_element_type=F32)
         + jnp.dot(mix_ref[...].astype(BF16), woc_ref[...], preferred_element_type=F32))
    xo_ref[...] = _layer_norm(alpha * x_ref[...] + y, g_ref[...], b_ref[...])


def _sample_out(xs, n, a_exp, mix, woa_exp, wout, ln_g, ln_b, layer, alpha):
    x_spec = pl.BlockSpec((n, D_MODEL), lambda i: (0, 0))
    return pl.pallas_call(
        functools.partial(_sample_out_kernel, alpha=alpha),
        out_shape=jax.ShapeDtypeStruct(xs.shape, F32),
        grid=(1,),
        in_specs=[x_spec, _const_spec(a_exp.shape), _const_spec(mix.shape),
                  _layer_spec((N_HEADS * LANES, D_MODEL), layer),
                  pl.BlockSpec((None, D_CONV, D_MODEL), lambda i: (layer, D_ATTN // D_CONV, 0),
                               pipeline_mode=pl.Buffered(1)),
                  _layer_spec((1, D_MODEL), layer, 1),
                  _layer_spec((1, D_MODEL), layer, 1)],
        out_specs=x_spec,
        input_output_aliases={0: 0},
        compiler_params=_params("arbitrary"),
        name="sample_out",
    )(xs, a_exp, mix, woa_exp, wout, ln_g, ln_b)


def _expand_q_cols(wq):
    depth, d = wq.shape[0], wq.shape[1]
    w = wq.reshape(depth, d, N_KV_HEADS, GQA_GROUP, HEAD_DIM)
    slabs = []
    for kv in range(N_KV_HEADS):
        parts = [jnp.zeros_like(w[:, :, kv])] * N_KV_HEADS
        parts[kv] = w[:, :, kv]
        slabs.append(jnp.concatenate(parts, axis=-1))
    return jnp.concatenate(slabs, axis=2).reshape(depth, d, N_HEADS * LANES)


def _expand_o_rows(wo):
    depth, d = wo.shape[0], wo.shape[2]
    w = wo.reshape(depth, N_KV_HEADS, GQA_GROUP, HEAD_DIM, d)
    slabs = []
    for kv in range(N_KV_HEADS):
        parts = [jnp.zeros_like(w[:, kv])] * N_KV_HEADS
        parts[kv] = w[:, kv]
        slabs.append(jnp.concatenate(parts, axis=2))
    return jnp.concatenate(slabs, axis=1).reshape(depth, N_HEADS * LANES, d)


def kernel(x_prompt, x_sample, cache_k, cache_v, state_conv, meta_tokens, rel_bias, w_in, conv_w,
           attn_sink, w_out, ffn_w_gate, ffn_w_up, ffn_w_down, ln_g, ln_b):
    depth = w_in.shape[0]
    alpha = float((2 * depth) ** 0.25)
    n_prompt, seq, d_model = x_prompt.shape
    n_sample, n_new, _ = x_sample.shape
    assert d_model == D_MODEL and seq % MIX_ROWS == 0
    assert n_new >= CONV_WIDTH - 1 and cache_k.shape[2] == WINDOW
    rows_p = n_prompt * seq
    rows_s = n_sample * n_new
    assert rows_s % BLOCK == 0

    xp = x_prompt.reshape(rows_p, D_MODEL)
    xs = jnp.concatenate([x_sample.reshape(rows_s, D_MODEL),
                          jnp.zeros((META_PAD, D_MODEL), F32), meta_tokens.astype(F32)], axis=0)

    bias_p, bias_s = _bias_tables(rel_bias, n_new)
    col = jnp.arange(2 * BLOCK)[None, :]
    fmask_prompt = jnp.where(col < META_PAD, NEG_INF, 0.0).astype(F32)
    fmask_meta = jnp.where(col < BLOCK + META_PAD, NEG_INF, 0.0).astype(F32)
    zeros_kv = jnp.zeros((1, BLOCK, D_KV), F32)
    zeros_u = jnp.zeros((1, SUBLANES, D_CONV), F32)

    ffn_f32 = (ffn_w_gate, ffn_w_up, ffn_w_down)
    w_ffn = tuple(a[0, 0].astype(BF16) for a in ffn_f32)

    def ffn(xp, xs, w_cur, layer, which):
        nxt = (layer, 1) if which == 0 else (layer + 1, 0)
        w_next = ffn_f32 + nxt if nxt[0] < depth else None
        xp, xs, *w_new = _ffn(xp, xs, w_cur, w_next, ln_g4, ln_b4, layer, 2 * which, alpha)
        return xp, xs, tuple(w_new)

    win = w_in.astype(BF16)
    wout = w_out.astype(BF16)
    w_exp = jnp.concatenate([_expand_q_cols(win[:, :, :D_ATTN]), win[:, :, D_ATTN:]], axis=2)
    woa_exp = _expand_o_rows(wout[:, :D_ATTN])
    ln_g4 = ln_g.reshape(depth, 3, 1, D_MODEL)
    ln_b4 = ln_b.reshape(depth, 3, 1, D_MODEL)
    k_buf = cache_k.reshape(depth, n_sample, WINDOW, D_KV)
    v_buf = cache_v.reshape(depth, n_sample, WINDOW, D_KV)
    state_rows = jnp.pad(state_conv, ((0, 0), (0, 0), (0, n_new - (CONV_WIDTH - 1)), (0, 0))
                         ).reshape(depth, rows_s, D_CONV)
    sink_rows = jnp.tile(attn_sink, (1, n_new)).reshape(depth, n_new * N_HEADS, 1)

    kp, vp, cp, cs = [], [], [], []
    for l in range(depth):
        xp, xs, w_ffn = ffn(xp, xs, w_ffn, l, 0)

        xs, k_m, v_m, u_m = _mixer(
            xs, rows_s, 1, BLOCK, BLOCK, META_PAD, True, win, wout, conv_w, attn_sink, bias_p,
            fmask_meta, zeros_kv, zeros_kv, zeros_u, ln_g4, ln_b4, l, alpha, "mixer_meta")
        xp, k_p, v_p, u_p = _mixer(
            xp, 0, n_prompt, seq, MIX_ROWS, 0, False, win, wout, conv_w, attn_sink, bias_p,
            fmask_prompt, k_m, v_m, u_m, ln_g4, ln_b4, l, alpha, "mixer_prompt")

        q_s, k_s, v_s, mix_s, u_s = _sample_proj(xs, rows_s, w_exp, state_rows, conv_w, l, n_new)
        o_s, k_buf, v_buf = _sample_attn(
            q_s.reshape(n_sample, n_new * N_HEADS, LANES), k_buf, v_buf,
            k_s.reshape(n_sample, n_new, D_KV), v_s.reshape(n_sample, n_new, D_KV),
            bias_s, sink_rows, l, n_new)
        xs = _sample_out(xs, rows_s, o_s.reshape(rows_s, N_HEADS * LANES), mix_s, woa_exp, wout,
                         ln_g4, ln_b4, l, alpha)

        xp, xs, w_ffn = ffn(xp, xs, w_ffn, l, 1)

        kp.append(k_p)
        vp.append(v_p)
        cp.append(u_p[:, SUBLANES - (CONV_WIDTH - 1):, :])
        cs.append(u_s.reshape(n_sample, n_new, D_CONV)[:, n_new - (CONV_WIDTH - 1):, :])

    kv_shape = (depth, -1, WINDOW, N_KV_HEADS, HEAD_DIM)
    return (xp.reshape(n_prompt, seq, D_MODEL),
            xs[:rows_s].reshape(n_sample, n_new, D_MODEL),
            jnp.stack(kp).reshape(kv_shape), jnp.stack(vp).reshape(kv_shape), jnp.stack(cp),
            k_buf.reshape(kv_shape), v_buf.reshape(kv_shape), jnp.stack(cs))
```

```python
import functools
import math

import jax
import jax.numpy as jnp
from jax import lax
from jax.experimental import pallas as pl
from jax.experimental.pallas import tpu as pltpu

F32 = jnp.float32
BF16 = jnp.bfloat16

D_MODEL = 1024
N_HEADS = 8
N_KV_HEADS = 2
HEAD_DIM = 64
GQA_GROUP = N_HEADS // N_KV_HEADS
D_ATTN = N_HEADS * HEAD_DIM
D_CONV = D_MODEL - D_ATTN
D_KV = N_KV_HEADS * HEAD_DIM
D_IN = D_ATTN + 2 * D_KV + 3 * D_CONV
D_FF = 2816
CONV_WIDTH = 3
WINDOW = 128
BLOCK = 128
N_META = 16
N_BUCKETS = 32
MAX_DISTANCE = 128
LN_EPS = 1e-5
Q_SCALE = HEAD_DIM ** -0.5

V7X_VMEM_LIMIT_BYTES = 56 * 1024 * 1024
V7X_MXU_COLUMNS = 256
SUBLANES = 8
LANES = 128

FFN_ROWS = 1024
FFN_NORM_PARTS = 8
FFN_CHUNK = V7X_MXU_COLUMNS
MIX_ROWS = 512
MIX_NORM_PARTS = 4
SAMPLE_SEQS = 16
META_PAD = BLOCK - N_META

NEG_INF = float("-inf")


def _const_spec(shape):
    nd = len(shape)
    return pl.BlockSpec(shape, lambda *_: (0,) * nd, pipeline_mode=pl.Buffered(1))


def _layer_spec(shape, *lead):
    block = (None,) * len(lead) + tuple(shape)
    idx = tuple(lead) + (0,) * len(shape)
    return pl.BlockSpec(block, lambda *_: idx, pipeline_mode=pl.Buffered(1))


def _params(*sem):
    return pltpu.CompilerParams(dimension_semantics=sem,
                                vmem_limit_bytes=V7X_VMEM_LIMIT_BYTES)


def _layer_norm(r, g, b):
    mu = jnp.mean(r, axis=-1, keepdims=True)
    rc = r - mu
    var = jnp.mean(rc * rc, axis=-1, keepdims=True)
    return rc * lax.rsqrt(var + LN_EPS) * g + b


def _t5_bucket(d):
    d = jnp.maximum(d, 0)
    max_exact = N_BUCKETS // 2
    df = jnp.maximum(d, 1).astype(F32)
    large = max_exact + (jnp.log(df / max_exact) / math.log(MAX_DISTANCE / max_exact)
                         * (N_BUCKETS - max_exact)).astype(jnp.int32)
    large = jnp.minimum(large, N_BUCKETS - 1)
    return jnp.where(d < max_exact, d, large)


def _bias_kernel(tab_ref, bp_ref, bs_ref, *, n_new):
    qi = lax.broadcasted_iota(jnp.int32, (BLOCK, 2 * BLOCK), 0)
    sj = lax.broadcasted_iota(jnp.int32, (BLOCK, 2 * BLOCK), 1)
    d = qi + BLOCK - sj
    valid = (d >= 0) & (d <= WINDOW)
    bk = _t5_bucket(d)
    for h in range(N_HEADS):
        acc = jnp.zeros(d.shape, F32)
        for b in range(N_BUCKETS):
            acc = jnp.where(bk == b, tab_ref[b, h], acc)
        bp_ref[h] = jnp.where(valid, acc, NEG_INF)
    rows, cols = bs_ref.shape
    r = lax.broadcasted_iota(jnp.int32, (rows, cols), 0)
    s = lax.broadcasted_iota(jnp.int32, (rows, cols), 1)
    ds = r // N_HEADS + WINDOW - s
    hs = r % N_HEADS
    valid_s = (ds >= 0) & (ds <= WINDOW) & (s < WINDOW + n_new)
    bks = _t5_bucket(ds)
    acc = jnp.zeros((rows, cols), F32)
    for h in range(N_HEADS):
        for b in range(N_BUCKETS):
            acc = jnp.where((bks == b) & (hs == h), tab_ref[b, h], acc)
    bs_ref[...] = jnp.where(valid_s, acc, NEG_INF)


def _bias_tables(rel_bias, n_new):
    rows = n_new * N_HEADS
    return pl.pallas_call(
        functools.partial(_bias_kernel, n_new=n_new),
        out_shape=(jax.ShapeDtypeStruct((N_HEADS, BLOCK, 2 * BLOCK), F32),
                   jax.ShapeDtypeStruct((rows, WINDOW + SUBLANES), F32)),
        in_specs=[pl.BlockSpec(memory_space=pltpu.SMEM)],
        name="bias_tables",
    )(rel_bias)


def _zero_after(v):
    u = lax.bitcast_convert_type(v, jnp.uint32)
    z = lax.shift_right_logical(lax.shift_right_logical(u, jnp.uint32(16)), jnp.uint32(16))
    return lax.bitcast_convert_type(z, F32)


def _fold_rows(a):
    a = a.reshape(a.shape[0] // 16, 16, a.shape[1]).sum(axis=0)
    return a.reshape(16, a.shape[1] // LANES, LANES).sum(axis=1)


def _ffn_pre_norm(x, wg_ref, wu_ref, wd_ref, act_ref, alpha, side_work):
    rows = x.shape[0]
    xb = x.astype(BF16)
    for ci, c in enumerate(range(0, D_FF, FFN_CHUNK)):
        gate = jnp.dot(xb, wg_ref[:, c:c + FFN_CHUNK], preferred_element_type=F32)
        up = jnp.dot(xb, wu_ref[:, c:c + FFN_CHUNK], preferred_element_type=F32)
        act = jax.nn.silu(gate) * up
        act_ref[0:rows, c:c + FFN_CHUNK] = act.astype(BF16)
        z = side_work(ci)
        if z is not None:
            act_ref[0:16, c:c + LANES] = (act[0:16, 0:LANES] + z).astype(BF16)
    y = jnp.dot(act_ref[0:rows, :], wd_ref[...], preferred_element_type=F32)
    return alpha * x + 0.5 * y


def _ffn_kernel(xp_ref, xs_ref, wg_ref, wu_ref, wd_ref, g_ref, b_ref, *rest, alpha, n_big,
                convert):
    if convert:
        nxt_refs, rest = rest[:3], rest[3:]
        op_ref, os_ref = rest[:2]
        cvt_refs, (act_ref, r_ref) = rest[2:5], rest[5:]
    else:
        op_ref, os_ref, act_ref, r_ref = rest
    i = pl.program_id(0)
    w = (wg_ref, wu_ref, wd_ref)
    g, b = g_ref[...], b_ref[...]
    part = FFN_ROWS // FFN_NORM_PARTS

    def norm_previous_tile(ci):
        if ci >= FFN_NORM_PARTS:
            return None
        rs = slice(ci * part, (ci + 1) * part)
        out = _layer_norm(r_ref[rs, :], g, b)
        op_ref[rs, :] = out
        return _zero_after(_fold_rows(out))

    def norm_and_convert(ci):
        k = ci - FFN_NORM_PARTS
        if convert and 0 <= k < len(cvt_refs):
            cvt_refs[k][...] = nxt_refs[k][...].astype(BF16)
            return None
        return norm_previous_tile(ci)

    @pl.when(i == 0)
    def _():
        r_ref[...] = jnp.zeros(r_ref.shape, F32)

    @pl.when(i < n_big)
    def _():
        r_ref[...] = _ffn_pre_norm(xp_ref[...], *w, act_ref, alpha, norm_and_convert)

    @pl.when(i == n_big)
    def _():
        os_ref[...] = _ffn_pre_norm(xs_ref[...], *w, act_ref, alpha, norm_previous_tile)

    @pl.when(i == n_big + 1)
    def _():
        os_ref[...] = _layer_norm(os_ref[...], g, b)


def _ffn(xp, xs, w_cur, w_next, ln_g, ln_b, layer, ln_idx, alpha):
    assert xp.shape[0] % FFN_ROWS == 0 and D_FF % FFN_CHUNK == 0
    assert FFN_NORM_PARTS + 3 <= D_FF // FFN_CHUNK and FFN_ROWS % (16 * FFN_NORM_PARTS) == 0
    n_big = xp.shape[0] // FFN_ROWS
    rows_s = xs.shape[0]
    assert rows_s <= FFN_ROWS
    last = lambda i: jnp.minimum(i, n_big - 1)
    in_spec = pl.BlockSpec((FFN_ROWS, D_MODEL), lambda i: (last(i), 0))
    out_spec = pl.BlockSpec((FFN_ROWS, D_MODEL), lambda i: (jnp.clip(i - 1, 0, n_big - 1), 0))
    small_in = pl.BlockSpec((rows_s, D_MODEL), lambda i: (0, 0), pipeline_mode=pl.Buffered(1))
    small_out = pl.BlockSpec((rows_s, D_MODEL), lambda i: (0, 0), pipeline_mode=pl.Buffered(1))
    in_specs = [in_spec, small_in] + [_const_spec(a.shape) for a in w_cur] + [
        _layer_spec((1, D_MODEL), layer, ln_idx), _layer_spec((1, D_MODEL), layer, ln_idx)]
    out_specs = [out_spec, small_out]
    out_shape = [jax.ShapeDtypeStruct(xp.shape, F32), jax.ShapeDtypeStruct(xs.shape, F32)]
    args = [xp, xs, *w_cur, ln_g, ln_b]
    if w_next is not None:
        *w_f32, nl, nw = w_next
        for a in w_f32:
            r, c = a.shape[2] // n_big, a.shape[3]
            assert a.shape[2] % (16 * n_big) == 0
            in_specs.append(pl.BlockSpec((None, None, r, c), lambda i: (nl, nw, last(i), 0)))
            out_specs.append(pl.BlockSpec((r, c), lambda i: (last(i), 0)))
            out_shape.append(jax.ShapeDtypeStruct(a.shape[2:], BF16))
        args += w_f32
    return pl.pallas_call(
        functools.partial(_ffn_kernel, alpha=alpha, n_big=n_big, convert=w_next is not None),
        out_shape=out_shape,
        grid=(n_big + 2,),
        in_specs=in_specs,
        out_specs=out_specs,
        scratch_shapes=[pltpu.VMEM((FFN_ROWS, D_FF), BF16), pltpu.VMEM((FFN_ROWS, D_MODEL), F32)],
        compiler_params=_params("arbitrary"),
        name="ffn_ln",
    )(*args)


def _mixer_kernel(x_ref, win_ref, wout_ref, convw_ref, sink_ref, bias_ref, fmask_ref,
                  kinit_ref, vinit_ref, uinit_ref, g_ref, b_ref,
                  xo_ref, klast_ref, vlast_ref, ulast_ref,
                  kd_scr, vd_scr, uscr, ascr, r_scr, *, rows, steps, n_tiles, alpha, zero_rows,
                  layer):
    step = pl.program_id(0)
    g, b = g_ref[...], b_ref[...]

    @pl.when(step == 0)
    def _():
        r_scr[...] = jnp.zeros(r_scr.shape, F32)

    @pl.when(step == n_tiles)
    def _():
        xo_ref[...] = _layer_norm(r_scr[...], g, b)

    @pl.when(step < n_tiles)
    def _():
        _mixer_tile(x_ref, win_ref, wout_ref, convw_ref, sink_ref, bias_ref, fmask_ref,
                    kinit_ref, vinit_ref, uinit_ref, g, b, xo_ref, klast_ref, vlast_ref,
                    ulast_ref, kd_scr, vd_scr, uscr, ascr, r_scr, t=step % steps, rows=rows,
                    alpha=alpha, zero_rows=zero_rows, layer=layer)


def _mixer_tile(x_ref, win_ref, wout_ref, convw_ref, sink_ref, bias_ref, fmask_ref,
                kinit_ref, vinit_ref, uinit_ref, g, b, xo_ref, klast_ref, vlast_ref, ulast_ref,
                kd_scr, vd_scr, uscr, ascr, r_scr, *, t, rows, alpha, zero_rows, layer):
    lane = lax.broadcasted_iota(jnp.int32, (1, LANES), 1)
    low = lane < HEAD_DIM

    def dup_heads(a):
        sw = pltpu.roll(a, HEAD_DIM, 1)
        return (jnp.where(low, a, sw).astype(BF16), jnp.where(low, sw, a).astype(BF16))

    @pl.when(t == 0)
    def _():
        for kv, (kd, vd) in enumerate(zip(dup_heads(kinit_ref[0]), dup_heads(vinit_ref[0]))):
            kd_scr[kv, 0:BLOCK, :] = kd
            vd_scr[kv, 0:BLOCK, :] = vd
        uscr[0:SUBLANES, :] = uinit_ref[0]

    x = x_ref[...]
    if zero_rows:
        ridx = lax.broadcasted_iota(jnp.int32, x.shape, 0)
        x = jnp.where(ridx >= zero_rows, x, 0.0)
    xb = x.astype(BF16)
    c0 = D_ATTN + 2 * D_KV
    zq = jnp.dot(xb, win_ref[:, 0:c0], preferred_element_type=F32)
    zc = jnp.dot(xb, win_ref[:, c0:], preferred_element_type=F32)
    part = rows // MIX_NORM_PARTS
    edges = []
    for ci in range(MIX_NORM_PARTS):
        rs = slice(ci * part, (ci + 1) * part)
        out = _layer_norm(r_scr[rs, :], g, b)
        xo_ref[rs, :] = out
        edges.append(_zero_after(_fold_rows(out)))
    k = zq[:, D_ATTN:D_ATTN + D_KV]
    v = zq[:, D_ATTN + D_KV:c0]
    k = jnp.concatenate([k[0:16] + sum(edges[0::2]), k[16:]], axis=0)
    v = jnp.concatenate([v[0:16] + sum(edges[1::2]), v[16:]], axis=0)
    u = zc[:, D_CONV:2 * D_CONV] * zc[:, 2 * D_CONV:3 * D_CONV]
    klast_ref[0] = k[rows - BLOCK:, :]
    vlast_ref[0] = v[rows - BLOCK:, :]
    ulast_ref[0] = u[rows - SUBLANES:, :]
    for kv, (kd, vd) in enumerate(zip(dup_heads(k), dup_heads(v))):
        kd_scr[kv, BLOCK:BLOCK + rows, :] = kd
        vd_scr[kv, BLOCK:BLOCK + rows, :] = vd
    uscr[SUBLANES:SUBLANES + rows, :] = u

    qs = zq[:, :D_ATTN] * Q_SCALE
    lane_q = lax.broadcasted_iota(jnp.int32, (1, D_ATTN), 1) % LANES
    q_even = jnp.where(lane_q < HEAD_DIM, qs, 0.0).astype(BF16)
    q_odd = jnp.where(lane_q < HEAD_DIM, 0.0, qs).astype(BF16)

    first = jnp.where(t == 0, fmask_ref[...], 0.0)
    ones_cols = jnp.ones((2 * BLOCK, LANES), BF16)
    for j in range(rows // BLOCK):
        r0 = j * BLOCK
        for kv in range(N_KV_HEADS):
            kd = kd_scr[kv, r0:r0 + 2 * BLOCK, :]
            vdx = jnp.concatenate([vd_scr[kv, r0:r0 + 2 * BLOCK, :], ones_cols], axis=1)
            heads = range(kv * GQA_GROUP, (kv + 1) * GQA_GROUP)
            q4 = jnp.concatenate(
                [(q_odd if h % 2 else q_even)[r0:r0 + BLOCK, (h // 2) * LANES:(h // 2 + 1) * LANES]
                 for h in heads], axis=0)
            s4 = lax.dot_general(q4, kd, (((1,), (1,)), ((), ())), preferred_element_type=F32)
            ps, es = [], []
            for g, h in enumerate(heads):
                s = s4[g * BLOCK:(g + 1) * BLOCK] + bias_ref[h]
                if j == 0:
                    s = s + first
                sl, sr = s[:, :LANES], s[:, LANES:]
                sk = sink_ref[layer, h]
                m1 = jnp.max(jnp.maximum(sl, sr), axis=-1, keepdims=True)
                mb = jnp.broadcast_to(jnp.maximum(m1, sk), (BLOCK, LANES))
                ps.append(jnp.concatenate([jnp.exp(sl - mb), jnp.exp(sr - mb)],
                                          axis=1).astype(BF16))
                es.append(jnp.exp(sk - mb))
            ox = jnp.dot(jnp.concatenate(ps, axis=0), vdx, preferred_element_type=F32)
            outs = []
            for g in range(GQA_GROUP):
                og = ox[g * BLOCK:(g + 1) * BLOCK]
                outs.append(og[:, :LANES] / (og[:, LANES:] + es[g]))
            for i in range(GQA_GROUP // 2):
                slab = jnp.where(low, outs[2 * i], outs[2 * i + 1])
                col = (kv * GQA_GROUP // 2 + i) * LANES
                ascr[r0:r0 + BLOCK, col:col + LANES] = slab.astype(BF16)

    um2 = uscr[SUBLANES - 2:SUBLANES - 2 + rows, :]
    um1 = uscr[SUBLANES - 1:SUBLANES - 1 + rows, :]
    conv = convw_ref[0:1, :] * um2 + convw_ref[1:2, :] * um1 + convw_ref[2:3, :] * u
    mix = zc[:, 0:D_CONV] * conv
    y = (jnp.dot(ascr[...], wout_ref[0:D_ATTN, :], preferred_element_type=F32)
         + jnp.dot(mix.astype(BF16), wout_ref[D_ATTN:, :], preferred_element_type=F32))
    r_scr[...] = alpha * x + y

    for kv in range(N_KV_HEADS):
        kd_scr[kv, 0:BLOCK, :] = kd_scr[kv, rows:rows + BLOCK, :]
        vd_scr[kv, 0:BLOCK, :] = vd_scr[kv, rows:rows + BLOCK, :]
    uscr[0:SUBLANES, :] = uscr[rows:rows + SUBLANES, :]


def _mixer(x, row0, n_seq, seq_len, rows, zero_rows, in_place, win, wout, convw, sink, bias_p,
           fmask, kinit, vinit, uinit, ln_g, ln_b, layer, alpha, name):
    assert seq_len % rows == 0 and rows % BLOCK == 0 and row0 % rows == 0
    assert rows % (16 * MIX_NORM_PARTS) == 0
    steps = seq_len // rows
    n_tiles = n_seq * steps
    blk0 = row0 // rows
    x_spec = pl.BlockSpec((rows, D_MODEL), lambda s: (blk0 + jnp.minimum(s, n_tiles - 1), 0))
    xo_spec = pl.BlockSpec((rows, D_MODEL), lambda s: (blk0 + jnp.maximum(s - 1, 0), 0))
    seq_spec = lambda shape: pl.BlockSpec(
        (1,) + shape, lambda s: (jnp.minimum(s // steps, n_seq - 1), 0, 0))
    kern = functools.partial(_mixer_kernel, rows=rows, steps=steps, n_tiles=n_tiles, alpha=alpha,
                             zero_rows=zero_rows, layer=layer)
    return pl.pallas_call(
        kern,
        out_shape=(jax.ShapeDtypeStruct(x.shape, F32),
                   jax.ShapeDtypeStruct((n_seq, BLOCK, D_KV), F32),
                   jax.ShapeDtypeStruct((n_seq, BLOCK, D_KV), F32),
                   jax.ShapeDtypeStruct((n_seq, SUBLANES, D_CONV), F32)),
        grid=(n_tiles + 1,),
        in_specs=[x_spec,
                  _layer_spec((D_MODEL, D_IN), layer),
                  _layer_spec((D_MODEL, D_MODEL), layer),
                  _layer_spec((CONV_WIDTH, D_CONV), layer),
                  pl.BlockSpec(memory_space=pltpu.SMEM),
                  _const_spec(bias_p.shape), _const_spec(fmask.shape),
                  _const_spec(kinit.shape), _const_spec(vinit.shape), _const_spec(uinit.shape),
                  _layer_spec((1, D_MODEL), layer, 1),
                  _layer_spec((1, D_MODEL), layer, 1)],
        out_specs=(xo_spec, seq_spec((BLOCK, D_KV)), seq_spec((BLOCK, D_KV)),
                   seq_spec((SUBLANES, D_CONV))),
        scratch_shapes=[pltpu.VMEM((N_KV_HEADS, BLOCK + rows, D_KV), BF16),
                        pltpu.VMEM((N_KV_HEADS, BLOCK + rows, D_KV), BF16),
                        pltpu.VMEM((SUBLANES + rows, D_CONV), F32),
                        pltpu.VMEM((rows, D_ATTN), BF16),
                        pltpu.VMEM((rows, D_MODEL), F32)],
        input_output_aliases={0: 0} if in_place else {},
        compiler_params=_params("arbitrary"),
        name=name,
    )(x, win, wout, convw, sink, bias_p, fmask, kinit, vinit, uinit, ln_g, ln_b)


def _heads_to_rows(q, n_seq):
    lane = lax.broadcasted_iota(jnp.int32, (1, LANES), 1)
    slabs = []
    for h in range(N_HEADS):
        pair = q[:, (h // 2) * LANES:(h // 2 + 1) * LANES]
        kv = h // GQA_GROUP
        data = pair if h % 2 == kv else pltpu.roll(pair, HEAD_DIM, 1)
        on_kv = (lane >= HEAD_DIM) if kv else (lane < HEAD_DIM)
        slabs.append(jnp.where(on_kv, data, 0.0))
    rows = jnp.concatenate(slabs, axis=1).reshape(q.shape[0], N_HEADS, LANES)
    return rows.reshape(n_seq, q.shape[0] // n_seq * N_HEADS, LANES)


def _rows_to_heads(o):
    n = o.shape[0] * o.shape[1] // N_HEADS
    wide = o.reshape(n, N_HEADS, LANES).reshape(n, N_HEADS * LANES)
    lane = lax.broadcasted_iota(jnp.int32, (1, LANES), 1)
    pairs = []
    for p in range(N_HEADS // 2):
        halves = []
        for e in range(2):
            h = 2 * p + e
            slab = wide[:, h * LANES:(h + 1) * LANES]
            halves.append(slab if h // GQA_GROUP == e else pltpu.roll(slab, HEAD_DIM, 1))
        pairs.append(jnp.where(lane < HEAD_DIM, halves[0], halves[1]))
    return jnp.concatenate(pairs, axis=1)


def _sample_mixer_kernel(x_ref, win_ref, wout_ref, convw_ref, state_ref, bias_ref, sink_ref,
                         g_ref, b_ref, ck_ref, cv_ref,
                         xo_ref, u_ref, ok_ref, ov_ref,
                         q_scr, kn_scr, vn_scr, o_scr, mix_scr, *, n_new, n_seq, chunk, alpha):
    i = pl.program_id(0)
    n_chunks = n_seq // chunk
    c0 = D_ATTN + 2 * D_KV

    @pl.when(i == 0)
    def _():
        z = jnp.dot(x_ref[...].astype(BF16), win_ref[...], preferred_element_type=F32)
        q_scr[...] = _heads_to_rows(z[:, :D_ATTN] * Q_SCALE, n_seq).astype(BF16)
        kn_scr[...] = z[:, D_ATTN:D_ATTN + D_KV].reshape(n_seq, n_new, D_KV)
        vn_scr[...] = z[:, D_ATTN + D_KV:c0].reshape(n_seq, n_new, D_KV)
        u = z[:, c0 + D_CONV:c0 + 2 * D_CONV] * z[:, c0 + 2 * D_CONV:c0 + 3 * D_CONV]
        u_ref[...] = u
        n = u.shape[0]
        tok = lax.broadcasted_iota(jnp.int32, u.shape, 0) % n_new
        st = state_ref[...]
        um2 = jnp.where(tok < 2, st, pltpu.roll(u, 2, 0))
        um1 = jnp.where(tok < 1, pltpu.roll(st, n - 1, 0), pltpu.roll(u, 1, 0))
        conv = convw_ref[0:1, :] * um2 + convw_ref[1:2, :] * um1 + convw_ref[2:3, :] * u
        mix_scr[...] = (z[:, c0:c0 + D_CONV] * conv).astype(BF16)

    @pl.when((i >= 1) & (i <= n_chunks))
    def _():
        s0 = pl.multiple_of((i - 1) * chunk, chunk)
        qb = q_scr[pl.ds(s0, chunk)]
        kn = kn_scr[pl.ds(s0, chunk)]
        vn = vn_scr[pl.ds(s0, chunk)]
        ck = ck_ref[...]
        cv = cv_ref[...]
        bias = bias_ref[...]
        sink = sink_ref[...]
        s_c = jnp.einsum("bqd,bkd->bqk", qb, ck.astype(BF16),
                         preferred_element_type=F32) + bias[None, :, :WINDOW]
        qf = qb.astype(F32)
        knf = kn.astype(BF16).astype(F32)
        vnf = vn.astype(BF16).astype(F32)
        s_n = [jnp.sum(qf * knf[:, t:t + 1, :], axis=-1, keepdims=True)
               + bias[None, :, WINDOW + t:WINDOW + t + 1] for t in range(n_new)]
        m = jnp.maximum(jnp.max(s_c, axis=-1, keepdims=True), sink[None])
        for s in s_n:
            m = jnp.maximum(m, s)
        p_c = jnp.exp(s_c - m)
        den = jnp.sum(p_c, axis=-1, keepdims=True) + jnp.exp(sink[None] - m)
        o = jnp.einsum("bqk,bkd->bqd", p_c.astype(BF16), cv.astype(BF16),
                       preferred_element_type=F32)
        for t, s in enumerate(s_n):
            p = jnp.exp(s - m)
            den = den + p
            o = o + p.astype(BF16).astype(F32) * vnf[:, t:t + 1, :]
        o_scr[pl.ds(s0, chunk)] = o / den
        ok_ref[:, 0:WINDOW - n_new, :] = ck_ref[:, n_new:WINDOW, :]
        ok_ref[:, WINDOW - n_new:WINDOW, :] = kn
        ov_ref[:, 0:WINDOW - n_new, :] = cv_ref[:, n_new:WINDOW, :]
        ov_ref[:, WINDOW - n_new:WINDOW, :] = vn

    @pl.when(i == n_chunks + 1)
    def _():
        a = _rows_to_heads(o_scr[...]).astype(BF16)
        y = (jnp.dot(a, wout_ref[0:D_ATTN, :], preferred_element_type=F32)
             + jnp.dot(mix_scr[...], wout_ref[D_ATTN:, :], preferred_element_type=F32))
        xo_ref[...] = _layer_norm(alpha * x_ref[...] + y, g_ref[...], b_ref[...])


def _sample_mixer(xs, n_seq, n_new, win, wout, convw, state_rows, bias_s, sink_rows, ln_g, ln_b,
                  k_buf, v_buf, layer, alpha):
    assert n_seq % SAMPLE_SEQS == 0
    n = n_seq * n_new
    qr = n_new * N_HEADS
    n_chunks = n_seq // SAMPLE_SEQS
    x_spec = pl.BlockSpec((n, D_MODEL), lambda i: (0, 0))
    lay_spec = pl.BlockSpec((None, SAMPLE_SEQS, WINDOW, D_KV),
                            lambda i: (layer, jnp.clip(i - 1, 0, n_chunks - 1), 0, 0))
    kern = functools.partial(_sample_mixer_kernel, n_new=n_new, n_seq=n_seq, chunk=SAMPLE_SEQS,
                             alpha=alpha)
    return pl.pallas_call(
        kern,
        out_shape=(jax.ShapeDtypeStruct(xs.shape, F32),
                   jax.ShapeDtypeStruct((n, D_CONV), F32),
                   jax.ShapeDtypeStruct(k_buf.shape, F32), jax.ShapeDtypeStruct(v_buf.shape, F32)),
        grid=(n_chunks + 2,),
        in_specs=[x_spec,
                  _layer_spec((D_MODEL, D_IN), layer),
                  _layer_spec((D_MODEL, D_MODEL), layer),
                  _layer_spec((CONV_WIDTH, D_CONV), layer),
                  _layer_spec((n, D_CONV), layer),
                  _layer_spec(bias_s.shape), _layer_spec(sink_rows.shape[1:], layer),
                  _layer_spec((1, D_MODEL), layer, 1), _layer_spec((1, D_MODEL), layer, 1),
                  lay_spec, lay_spec],
        out_specs=(x_spec, pl.BlockSpec((n, D_CONV), lambda i: (0, 0)), lay_spec, lay_spec),
        scratch_shapes=[pltpu.VMEM((n_seq, qr, LANES), BF16),
                        pltpu.VMEM((n_seq, n_new, D_KV), F32),
                        pltpu.VMEM((n_seq, n_new, D_KV), F32),
                        pltpu.VMEM((n_seq, qr, LANES), F32),
                        pltpu.VMEM((n, D_CONV), BF16)],
        input_output_aliases={0: 0, 9: 2, 10: 3},
        compiler_params=_params("arbitrary"),
        name="sample_mixer",
    )(xs, win, wout, convw, state_rows, bias_s, sink_rows, ln_g, ln_b, k_buf, v_buf)


def kernel(x_prompt, x_sample, cache_k, cache_v, state_conv, meta_tokens, rel_bias, w_in, conv_w,
           attn_sink, w_out, ffn_w_gate, ffn_w_up, ffn_w_down, ln_g, ln_b):
    depth = w_in.shape[0]
    alpha = float((2 * depth) ** 0.25)
    n_prompt, seq, d_model = x_prompt.shape
    n_sample, n_new, _ = x_sample.shape
    assert d_model == D_MODEL and seq % MIX_ROWS == 0
    assert n_new >= CONV_WIDTH - 1 and cache_k.shape[2] == WINDOW
    rows_p = n_prompt * seq
    rows_s = n_sample * n_new
    assert rows_s % BLOCK == 0

    xp = x_prompt.reshape(rows_p, D_MODEL)
    xs = jnp.concatenate([x_sample.reshape(rows_s, D_MODEL),
                          jnp.zeros((META_PAD, D_MODEL), F32), meta_tokens.astype(F32)], axis=0)

    bias_p, bias_s = _bias_tables(rel_bias, n_new)
    col = jnp.arange(2 * BLOCK)[None, :]
    fmask_prompt = jnp.where(col < META_PAD, NEG_INF, 0.0).astype(F32)
    fmask_meta = jnp.where(col < BLOCK + META_PAD, NEG_INF, 0.0).astype(F32)
    zeros_kv = jnp.zeros((1, BLOCK, D_KV), F32)
    zeros_u = jnp.zeros((1, SUBLANES, D_CONV), F32)

    ffn_f32 = (ffn_w_gate, ffn_w_up, ffn_w_down)
    w_ffn = tuple(a[0, 0].astype(BF16) for a in ffn_f32)

    def ffn(xp, xs, w_cur, layer, which):
        nxt = (layer, 1) if which == 0 else (layer + 1, 0)
        w_next = ffn_f32 + nxt if nxt[0] < depth else None
        xp, xs, *w_new = _ffn(xp, xs, w_cur, w_next, ln_g4, ln_b4, layer, 2 * which, alpha)
        return xp, xs, tuple(w_new)

    win = w_in.astype(BF16)
    wout = w_out.astype(BF16)
    ln_g4 = ln_g.reshape(depth, 3, 1, D_MODEL)
    ln_b4 = ln_b.reshape(depth, 3, 1, D_MODEL)
    k_buf = cache_k.reshape(depth, n_sample, WINDOW, D_KV)
    v_buf = cache_v.reshape(depth, n_sample, WINDOW, D_KV)
    state_rows = jnp.pad(state_conv, ((0, 0), (0, 0), (0, n_new - (CONV_WIDTH - 1)), (0, 0))
                         ).reshape(depth, rows_s, D_CONV)
    sink_rows = jnp.tile(attn_sink, (1, n_new)).reshape(depth, n_new * N_HEADS, 1)

    kp, vp, cp, cs = [], [], [], []
    for l in range(depth):
        xp, xs, w_ffn = ffn(xp, xs, w_ffn, l, 0)

        xs, k_m, v_m, u_m = _mixer(
            xs, rows_s, 1, BLOCK, BLOCK, META_PAD, True, win, wout, conv_w, attn_sink, bias_p,
            fmask_meta, zeros_kv, zeros_kv, zeros_u, ln_g4, ln_b4, l, alpha, "mixer_meta")
        xp, k_p, v_p, u_p = _mixer(
            xp, 0, n_prompt, seq, MIX_ROWS, 0, False, win, wout, conv_w, attn_sink, bias_p,
            fmask_prompt, k_m, v_m, u_m, ln_g4, ln_b4, l, alpha, "mixer_prompt")

        xs, u_s, k_buf, v_buf = _sample_mixer(
            xs, n_sample, n_new, win, wout, conv_w, state_rows, bias_s, sink_rows, ln_g4, ln_b4,
            k_buf, v_buf, l, alpha)

        xp, xs, w_ffn = ffn(xp, xs, w_ffn, l, 1)

        kp.append(k_p)
        vp.append(v_p)
        cp.append(u_p[:, SUBLANES - (CONV_WIDTH - 1):, :])
        cs.append(u_s.reshape(n_sample, n_new, D_CONV)[:, n_new - (CONV_WIDTH - 1):, :])

    kv_shape = (depth, -1, WINDOW, N_KV_HEADS, HEAD_DIM)
    return (xp.reshape(n_prompt, seq, D_MODEL),
            xs[:rows_s].reshape(n_sample, n_new, D_MODEL),
            jnp.stack(kp).reshape(kv_shape), jnp.stack(vp).reshape(kv_shape), jnp.stack(cp),
            k_buf.reshape(kv_shape), v_buf.reshape(kv_shape), jnp.stack(cs))
```

```python
import functools
import math

import jax
import jax.numpy as jnp
from jax import lax
from jax.experimental import pallas as pl
from jax.experimental.pallas import tpu as pltpu

F32 = jnp.float32
BF16 = jnp.bfloat16

D_MODEL = 1024
N_HEADS = 8
N_KV_HEADS = 2
HEAD_DIM = 64
GQA_GROUP = N_HEADS // N_KV_HEADS
D_ATTN = N_HEADS * HEAD_DIM
D_CONV = D_MODEL - D_ATTN
D_KV = N_KV_HEADS * HEAD_DIM
D_IN = D_ATTN + 2 * D_KV + 3 * D_CONV
D_FF = 2816
CONV_WIDTH = 3
WINDOW = 128
BLOCK = 128
N_META = 16
N_BUCKETS = 32
MAX_DISTANCE = 128
LN_EPS = 1e-5
Q_SCALE = HEAD_DIM ** -0.5

V7X_VMEM_LIMIT_BYTES = 60 * 1024 * 1024
V7X_MXU_COLUMNS = 256
SUBLANES = 8
LANES = 128

FFN_ROWS = 1024
FFN_NORM_PARTS = 8
FFN_CHUNK = V7X_MXU_COLUMNS
MIX_ROWS = 512
MIX_NORM_PARTS = 4
SAMPLE_SEQS = 16
META_PAD = BLOCK - N_META

NEG_INF = float("-inf")


def _const_spec(shape):
    nd = len(shape)
    return pl.BlockSpec(shape, lambda *_: (0,) * nd, pipeline_mode=pl.Buffered(1))


def _layer_spec(shape, *lead):
    block = (None,) * len(lead) + tuple(shape)
    idx = tuple(lead) + (0,) * len(shape)
    return pl.BlockSpec(block, lambda *_: idx, pipeline_mode=pl.Buffered(1))


def _params(*sem):
    return pltpu.CompilerParams(dimension_semantics=sem,
                                vmem_limit_bytes=V7X_VMEM_LIMIT_BYTES)


def _layer_norm(r, g, b):
    mu = jnp.mean(r, axis=-1, keepdims=True)
    rc = r - mu
    var = jnp.mean(rc * rc, axis=-1, keepdims=True)
    return rc * lax.rsqrt(var + LN_EPS) * g + b


def _t5_bucket(d):
    d = jnp.maximum(d, 0)
    max_exact = N_BUCKETS // 2
    df = jnp.maximum(d, 1).astype(F32)
    large = max_exact + (jnp.log(df / max_exact) / math.log(MAX_DISTANCE / max_exact)
                         * (N_BUCKETS - max_exact)).astype(jnp.int32)
    large = jnp.minimum(large, N_BUCKETS - 1)
    return jnp.where(d < max_exact, d, large)


def _bias_kernel(tab_ref, bp_ref, bs_ref, *, n_new):
    qi = lax.broadcasted_iota(jnp.int32, (BLOCK, 2 * BLOCK), 0)
    sj = lax.broadcasted_iota(jnp.int32, (BLOCK, 2 * BLOCK), 1)
    d = qi + BLOCK - sj
    valid = (d >= 0) & (d <= WINDOW)
    bk = _t5_bucket(d)
    for h in range(N_HEADS):
        acc = jnp.zeros(d.shape, F32)
        for b in range(N_BUCKETS):
            acc = jnp.where(bk == b, tab_ref[b, h], acc)
        bp_ref[h] = jnp.where(valid, acc, NEG_INF)
    rows, cols = bs_ref.shape
    r = lax.broadcasted_iota(jnp.int32, (rows, cols), 0)
    s = lax.broadcasted_iota(jnp.int32, (rows, cols), 1)
    ds = r // N_HEADS + WINDOW - s
    hs = r % N_HEADS
    valid_s = (ds >= 0) & (ds <= WINDOW) & (s < WINDOW + n_new)
    bks = _t5_bucket(ds)
    acc = jnp.zeros((rows, cols), F32)
    for h in range(N_HEADS):
        for b in range(N_BUCKETS):
            acc = jnp.where((bks == b) & (hs == h), tab_ref[b, h], acc)
    bs_ref[...] = jnp.where(valid_s, acc, NEG_INF)


def _bias_tables(rel_bias, n_new):
    rows = n_new * N_HEADS
    return pl.pallas_call(
        functools.partial(_bias_kernel, n_new=n_new),
        out_shape=(jax.ShapeDtypeStruct((N_HEADS, BLOCK, 2 * BLOCK), F32),
                   jax.ShapeDtypeStruct((rows, WINDOW + SUBLANES), F32)),
        in_specs=[pl.BlockSpec(memory_space=pltpu.SMEM)],
        name="bias_tables",
    )(rel_bias)


def _zero_after(v):
    u = lax.bitcast_convert_type(v, jnp.uint32)
    z = lax.shift_right_logical(lax.shift_right_logical(u, jnp.uint32(16)), jnp.uint32(16))
    return lax.bitcast_convert_type(z, F32)


def _fold_rows(a):
    a = a.reshape(a.shape[0] // 16, 16, a.shape[1]).sum(axis=0)
    return a.reshape(16, a.shape[1] // LANES, LANES).sum(axis=1)


def _ffn_pre_norm(x, wg_ref, wu_ref, wd_ref, act_ref, alpha, side_work):
    rows = x.shape[0]
    xb = x.astype(BF16)
    for ci, c in enumerate(range(0, D_FF, FFN_CHUNK)):
        gate = jnp.dot(xb, wg_ref[:, c:c + FFN_CHUNK], preferred_element_type=F32)
        up = jnp.dot(xb, wu_ref[:, c:c + FFN_CHUNK], preferred_element_type=F32)
        act = jax.nn.silu(gate) * up
        act_ref[0:rows, c:c + FFN_CHUNK] = act.astype(BF16)
        z = side_work(ci)
        if z is not None:
            act_ref[0:16, c:c + LANES] = (act[0:16, 0:LANES] + z).astype(BF16)
    y = jnp.dot(act_ref[0:rows, :], wd_ref[...], preferred_element_type=F32)
    return alpha * x + 0.5 * y


def _ffn_kernel(xp_ref, xs_ref, wg_ref, wu_ref, wd_ref, g_ref, b_ref, *rest, alpha, n_big,
                n_cast):
    f32_refs, rest = rest[:n_cast], rest[n_cast:]
    op_ref, os_ref = rest[:2]
    bf16_refs, (act_ref, r_ref) = rest[2:2 + n_cast], rest[2 + n_cast:]
    i = pl.program_id(0)
    w = (wg_ref, wu_ref, wd_ref)
    g, b = g_ref[...], b_ref[...]
    part = FFN_ROWS // FFN_NORM_PARTS

    def norm_previous_tile(ci):
        if ci >= FFN_NORM_PARTS:
            return None
        rs = slice(ci * part, (ci + 1) * part)
        out = _layer_norm(r_ref[rs, :], g, b)
        op_ref[rs, :] = out
        return _zero_after(_fold_rows(out))

    def norm_and_convert(ci):
        if ci == FFN_NORM_PARTS:
            for src, dst in zip(f32_refs, bf16_refs):
                dst[...] = src[...].astype(BF16)
        return norm_previous_tile(ci)

    @pl.when(i == 0)
    def _():
        r_ref[...] = jnp.zeros(r_ref.shape, F32)

    @pl.when(i < n_big)
    def _():
        r_ref[...] = _ffn_pre_norm(xp_ref[...], *w, act_ref, alpha, norm_and_convert)

    @pl.when(i == n_big)
    def _():
        os_ref[...] = _ffn_pre_norm(xs_ref[...], *w, act_ref, alpha, norm_previous_tile)

    @pl.when(i == n_big + 1)
    def _():
        os_ref[...] = _layer_norm(os_ref[...], g, b)


def _ffn(xp, xs, w_cur, to_cast, ln_g, ln_b, layer, ln_idx, alpha):
    assert xp.shape[0] % FFN_ROWS == 0 and D_FF % FFN_CHUNK == 0
    assert FFN_NORM_PARTS < D_FF // FFN_CHUNK and FFN_ROWS % (16 * FFN_NORM_PARTS) == 0
    n_big = xp.shape[0] // FFN_ROWS
    rows_s = xs.shape[0]
    assert rows_s <= FFN_ROWS
    last = lambda i: jnp.minimum(i, n_big - 1)
    in_spec = pl.BlockSpec((FFN_ROWS, D_MODEL), lambda i: (last(i), 0))
    out_spec = pl.BlockSpec((FFN_ROWS, D_MODEL), lambda i: (jnp.clip(i - 1, 0, n_big - 1), 0))
    small_in = pl.BlockSpec((rows_s, D_MODEL), lambda i: (0, 0), pipeline_mode=pl.Buffered(1))
    small_out = pl.BlockSpec((rows_s, D_MODEL), lambda i: (0, 0), pipeline_mode=pl.Buffered(1))
    in_specs = [in_spec, small_in] + [_const_spec(a.shape) for a in w_cur] + [
        _layer_spec((1, D_MODEL), layer, ln_idx), _layer_spec((1, D_MODEL), layer, ln_idx)]
    out_specs = [out_spec, small_out]
    out_shape = [jax.ShapeDtypeStruct(xp.shape, F32), jax.ShapeDtypeStruct(xs.shape, F32)]
    args = [xp, xs, *w_cur, ln_g, ln_b]
    for a, lead in to_cast:
        r, c = a.shape[-2] // n_big, a.shape[-1]
        assert a.shape[-2] % (16 * n_big) == 0 and len(lead) == a.ndim - 2
        in_specs.append(pl.BlockSpec((None,) * len(lead) + (r, c),
                                     lambda i, lead=lead: lead + (last(i), 0)))
        out_specs.append(pl.BlockSpec((r, c), lambda i: (last(i), 0)))
        out_shape.append(jax.ShapeDtypeStruct(a.shape[-2:], BF16))
        args.append(a)
    return pl.pallas_call(
        functools.partial(_ffn_kernel, alpha=alpha, n_big=n_big, n_cast=len(to_cast)),
        out_shape=out_shape,
        grid=(n_big + 2,),
        in_specs=in_specs,
        out_specs=out_specs,
        scratch_shapes=[pltpu.VMEM((FFN_ROWS, D_FF), BF16), pltpu.VMEM((FFN_ROWS, D_MODEL), F32)],
        compiler_params=_params("arbitrary"),
        name="ffn_ln",
    )(*args)


def _mixer_kernel(x_ref, win_ref, wout_ref, convw_ref, sink_ref, bias_ref, fmask_ref,
                  kinit_ref, vinit_ref, uinit_ref, g_ref, b_ref,
                  xo_ref, klast_ref, vlast_ref, ulast_ref,
                  kd_scr, vd_scr, uscr, ascr, r_scr, *, rows, steps, n_tiles, alpha, zero_rows,
                  layer):
    step = pl.program_id(0)
    g, b = g_ref[...], b_ref[...]

    @pl.when(step == 0)
    def _():
        r_scr[...] = jnp.zeros(r_scr.shape, F32)

    @pl.when(step == n_tiles)
    def _():
        xo_ref[...] = _layer_norm(r_scr[...], g, b)

    @pl.when(step < n_tiles)
    def _():
        _mixer_tile(x_ref, win_ref, wout_ref, convw_ref, sink_ref, bias_ref, fmask_ref,
                    kinit_ref, vinit_ref, uinit_ref, g, b, xo_ref, klast_ref, vlast_ref,
                    ulast_ref, kd_scr, vd_scr, uscr, ascr, r_scr, t=step % steps, rows=rows,
                    alpha=alpha, zero_rows=zero_rows, layer=layer)


def _mixer_tile(x_ref, win_ref, wout_ref, convw_ref, sink_ref, bias_ref, fmask_ref,
                kinit_ref, vinit_ref, uinit_ref, g, b, xo_ref, klast_ref, vlast_ref, ulast_ref,
                kd_scr, vd_scr, uscr, ascr, r_scr, *, t, rows, alpha, zero_rows, layer):
    lane = lax.broadcasted_iota(jnp.int32, (1, LANES), 1)
    low = lane < HEAD_DIM

    def dup_heads(a):
        sw = pltpu.roll(a, HEAD_DIM, 1)
        return (jnp.where(low, a, sw).astype(BF16), jnp.where(low, sw, a).astype(BF16))

    @pl.when(t == 0)
    def _():
        for kv, (kd, vd) in enumerate(zip(dup_heads(kinit_ref[0]), dup_heads(vinit_ref[0]))):
            kd_scr[kv, 0:BLOCK, :] = kd
            vd_scr[kv, 0:BLOCK, :] = vd
        uscr[0:SUBLANES, :] = uinit_ref[0]

    x = x_ref[...]
    if zero_rows:
        ridx = lax.broadcasted_iota(jnp.int32, x.shape, 0)
        x = jnp.where(ridx >= zero_rows, x, 0.0)
    xb = x.astype(BF16)
    c0 = D_ATTN + 2 * D_KV
    zq = jnp.dot(xb, win_ref[:, 0:c0], preferred_element_type=F32)
    zc = jnp.dot(xb, win_ref[:, c0:], preferred_element_type=F32)
    part = rows // MIX_NORM_PARTS
    edges = []
    for ci in range(MIX_NORM_PARTS):
        rs = slice(ci * part, (ci + 1) * part)
        out = _layer_norm(r_scr[rs, :], g, b)
        xo_ref[rs, :] = out
        edges.append(_zero_after(_fold_rows(out)))
    k = zq[:, D_ATTN:D_ATTN + D_KV]
    v = zq[:, D_ATTN + D_KV:c0]
    k = jnp.concatenate([k[0:16] + sum(edges[0::2]), k[16:]], axis=0)
    v = jnp.concatenate([v[0:16] + sum(edges[1::2]), v[16:]], axis=0)
    u = zc[:, D_CONV:2 * D_CONV] * zc[:, 2 * D_CONV:3 * D_CONV]
    klast_ref[0] = k[rows - BLOCK:, :]
    vlast_ref[0] = v[rows - BLOCK:, :]
    ulast_ref[0] = u[rows - SUBLANES:, :]
    for kv, (kd, vd) in enumerate(zip(dup_heads(k), dup_heads(v))):
        kd_scr[kv, BLOCK:BLOCK + rows, :] = kd
        vd_scr[kv, BLOCK:BLOCK + rows, :] = vd
    uscr[SUBLANES:SUBLANES + rows, :] = u

    qs = zq[:, :D_ATTN] * Q_SCALE
    lane_q = lax.broadcasted_iota(jnp.int32, (1, D_ATTN), 1) % LANES
    q_even = jnp.where(lane_q < HEAD_DIM, qs, 0.0).astype(BF16)
    q_odd = jnp.where(lane_q < HEAD_DIM, 0.0, qs).astype(BF16)

    first = jnp.where(t == 0, fmask_ref[...], 0.0)
    ones_cols = jnp.ones((2 * BLOCK, LANES), BF16)
    for j in range(rows // BLOCK):
        r0 = j * BLOCK
        for kv in range(N_KV_HEADS):
            kd = kd_scr[kv, r0:r0 + 2 * BLOCK, :]
            vdx = jnp.concatenate([vd_scr[kv, r0:r0 + 2 * BLOCK, :], ones_cols], axis=1)
            heads = range(kv * GQA_GROUP, (kv + 1) * GQA_GROUP)
            q4 = jnp.concatenate(
                [(q_odd if h % 2 else q_even)[r0:r0 + BLOCK, (h // 2) * LANES:(h // 2 + 1) * LANES]
                 for h in heads], axis=0)
            s4 = lax.dot_general(q4, kd, (((1,), (1,)), ((), ())), preferred_element_type=F32)
            ps, es = [], []
            for g, h in enumerate(heads):
                s = s4[g * BLOCK:(g + 1) * BLOCK] + bias_ref[h]
                if j == 0:
                    s = s + first
                sl, sr = s[:, :LANES], s[:, LANES:]
                sk = sink_ref[layer, h]
                m1 = jnp.max(jnp.maximum(sl, sr), axis=-1, keepdims=True)
                mb = jnp.broadcast_to(jnp.maximum(m1, sk), (BLOCK, LANES))
                ps.append(jnp.concatenate([jnp.exp(sl - mb), jnp.exp(sr - mb)],
                                          axis=1).astype(BF16))
                es.append(jnp.exp(sk - mb))
            ox = jnp.dot(jnp.concatenate(ps, axis=0), vdx, preferred_element_type=F32)
            outs = []
            for g in range(GQA_GROUP):
                og = ox[g * BLOCK:(g + 1) * BLOCK]
                outs.append(og[:, :LANES] / (og[:, LANES:] + es[g]))
            for i in range(GQA_GROUP // 2):
                slab = jnp.where(low, outs[2 * i], outs[2 * i + 1])
                col = (kv * GQA_GROUP // 2 + i) * LANES
                ascr[r0:r0 + BLOCK, col:col + LANES] = slab.astype(BF16)

    um2 = uscr[SUBLANES - 2:SUBLANES - 2 + rows, :]
    um1 = uscr[SUBLANES - 1:SUBLANES - 1 + rows, :]
    conv = convw_ref[0:1, :] * um2 + convw_ref[1:2, :] * um1 + convw_ref[2:3, :] * u
    mix = zc[:, 0:D_CONV] * conv
    y = (jnp.dot(ascr[...], wout_ref[0:D_ATTN, :], preferred_element_type=F32)
         + jnp.dot(mix.astype(BF16), wout_ref[D_ATTN:, :], preferred_element_type=F32))
    r_scr[...] = alpha * x + y

    for kv in range(N_KV_HEADS):
        kd_scr[kv, 0:BLOCK, :] = kd_scr[kv, rows:rows + BLOCK, :]
        vd_scr[kv, 0:BLOCK, :] = vd_scr[kv, rows:rows + BLOCK, :]
    uscr[0:SUBLANES, :] = uscr[rows:rows + SUBLANES, :]


def _mixer(x, row0, n_seq, seq_len, rows, zero_rows, in_place, win, wout, convw, sink, bias_p,
           fmask, kinit, vinit, uinit, ln_g, ln_b, layer, alpha, name):
    assert seq_len % rows == 0 and rows % BLOCK == 0 and row0 % rows == 0
    assert rows % (16 * MIX_NORM_PARTS) == 0
    steps = seq_len // rows
    n_tiles = n_seq * steps
    blk0 = row0 // rows
    x_spec = pl.BlockSpec((rows, D_MODEL), lambda s: (blk0 + jnp.minimum(s, n_tiles - 1), 0))
    xo_spec = pl.BlockSpec((rows, D_MODEL), lambda s: (blk0 + jnp.maximum(s - 1, 0), 0))
    seq_spec = lambda shape: pl.BlockSpec(
        (1,) + shape, lambda s: (jnp.minimum(s // steps, n_seq - 1), 0, 0))
    kern = functools.partial(_mixer_kernel, rows=rows, steps=steps, n_tiles=n_tiles, alpha=alpha,
                             zero_rows=zero_rows, layer=layer)
    return pl.pallas_call(
        kern,
        out_shape=(jax.ShapeDtypeStruct(x.shape, F32),
                   jax.ShapeDtypeStruct((n_seq, BLOCK, D_KV), F32),
                   jax.ShapeDtypeStruct((n_seq, BLOCK, D_KV), F32),
                   jax.ShapeDtypeStruct((n_seq, SUBLANES, D_CONV), F32)),
        grid=(n_tiles + 1,),
        in_specs=[x_spec,
                  _const_spec((D_MODEL, D_IN)),
                  _const_spec((D_MODEL, D_MODEL)),
                  _layer_spec((CONV_WIDTH, D_CONV), layer),
                  pl.BlockSpec(memory_space=pltpu.SMEM),
                  _const_spec(bias_p.shape), _const_spec(fmask.shape),
                  _const_spec(kinit.shape), _const_spec(vinit.shape), _const_spec(uinit.shape),
                  _layer_spec((1, D_MODEL), layer, 1),
                  _layer_spec((1, D_MODEL), layer, 1)],
        out_specs=(xo_spec, seq_spec((BLOCK, D_KV)), seq_spec((BLOCK, D_KV)),
                   seq_spec((SUBLANES, D_CONV))),
        scratch_shapes=[pltpu.VMEM((N_KV_HEADS, BLOCK + rows, D_KV), BF16),
                        pltpu.VMEM((N_KV_HEADS, BLOCK + rows, D_KV), BF16),
                        pltpu.VMEM((SUBLANES + rows, D_CONV), F32),
                        pltpu.VMEM((rows, D_ATTN), BF16),
                        pltpu.VMEM((rows, D_MODEL), F32)],
        input_output_aliases={0: 0} if in_place else {},
        compiler_params=_params("arbitrary"),
        name=name,
    )(x, win, wout, convw, sink, bias_p, fmask, kinit, vinit, uinit, ln_g, ln_b)


def _heads_to_rows(q, n_seq):
    lane = lax.broadcasted_iota(jnp.int32, (1, LANES), 1)
    slabs = []
    for h in range(N_HEADS):
        pair = q[:, (h // 2) * LANES:(h // 2 + 1) * LANES]
        kv = h // GQA_GROUP
        data = pair if h % 2 == kv else pltpu.roll(pair, HEAD_DIM, 1)
        on_kv = (lane >= HEAD_DIM) if kv else (lane < HEAD_DIM)
        slabs.append(jnp.where(on_kv, data, 0.0))
    rows = jnp.concatenate(slabs, axis=1).reshape(q.shape[0], N_HEADS, LANES)
    return rows.reshape(n_seq, q.shape[0] // n_seq * N_HEADS, LANES)


def _rows_to_heads(o):
    n = o.shape[0] * o.shape[1] // N_HEADS
    wide = o.reshape(n, N_HEADS, LANES).reshape(n, N_HEADS * LANES)
    lane = lax.broadcasted_iota(jnp.int32, (1, LANES), 1)
    pairs = []
    for p in range(N_HEADS // 2):
        halves = []
        for e in range(2):
            h = 2 * p + e
            slab = wide[:, h * LANES:(h + 1) * LANES]
            halves.append(slab if h // GQA_GROUP == e else pltpu.roll(slab, HEAD_DIM, 1))
        pairs.append(jnp.where(lane < HEAD_DIM, halves[0], halves[1]))
    return jnp.concatenate(pairs, axis=1)


def _sample_mixer_kernel(x_ref, win_ref, wout_ref, convw_ref, state_ref, bias_ref, sink_ref,
                         g_ref, b_ref, ck_ref, cv_ref,
                         xo_ref, u_ref, ok_ref, ov_ref,
                         q_scr, kn_scr, vn_scr, o_scr, mix_scr, *, n_new, n_seq, chunk, alpha):
    i = pl.program_id(0)
    n_chunks = n_seq // chunk
    c0 = D_ATTN + 2 * D_KV

    @pl.when(i == 0)
    def _():
        z = jnp.dot(x_ref[...].astype(BF16), win_ref[...], preferred_element_type=F32)
        q_scr[...] = _heads_to_rows(z[:, :D_ATTN] * Q_SCALE, n_seq).astype(BF16)
        kn_scr[...] = z[:, D_ATTN:D_ATTN + D_KV].reshape(n_seq, n_new, D_KV)
        vn_scr[...] = z[:, D_ATTN + D_KV:c0].reshape(n_seq, n_new, D_KV)
        u = z[:, c0 + D_CONV:c0 + 2 * D_CONV] * z[:, c0 + 2 * D_CONV:c0 + 3 * D_CONV]
        u_ref[...] = u
        n = u.shape[0]
        tok = lax.broadcasted_iota(jnp.int32, u.shape, 0) % n_new
        st = state_ref[...]
        um2 = jnp.where(tok < 2, st, pltpu.roll(u, 2, 0))
        um1 = jnp.where(tok < 1, pltpu.roll(st, n - 1, 0), pltpu.roll(u, 1, 0))
        conv = convw_ref[0:1, :] * um2 + convw_ref[1:2, :] * um1 + convw_ref[2:3, :] * u
        mix_scr[...] = (z[:, c0:c0 + D_CONV] * conv).astype(BF16)

    @pl.when((i >= 1) & (i <= n_chunks))
    def _():
        s0 = pl.multiple_of((i - 1) * chunk, chunk)
        qb = q_scr[pl.ds(s0, chunk)]
        kn = kn_scr[pl.ds(s0, chunk)]
        vn = vn_scr[pl.ds(s0, chunk)]
        ck = ck_ref[...]
        cv = cv_ref[...]
        bias = bias_ref[...]
        sink = sink_ref[...]
        s_c = jnp.einsum("bqd,bkd->bqk", qb, ck.astype(BF16),
                         preferred_element_type=F32) + bias[None, :, :WINDOW]
        qf = qb.astype(F32)
        knf = kn.astype(BF16).astype(F32)
        vnf = vn.astype(BF16).astype(F32)
        s_n = [jnp.sum(qf * knf[:, t:t + 1, :], axis=-1, keepdims=True)
               + bias[None, :, WINDOW + t:WINDOW + t + 1] for t in range(n_new)]
        m = jnp.maximum(jnp.max(s_c, axis=-1, keepdims=True), sink[None])
        for s in s_n:
            m = jnp.maximum(m, s)
        p_c = jnp.exp(s_c - m)
        den = jnp.sum(p_c, axis=-1, keepdims=True) + jnp.exp(sink[None] - m)
        o = jnp.einsum("bqk,bkd->bqd", p_c.astype(BF16), cv.astype(BF16),
                       preferred_element_type=F32)
        for t, s in enumerate(s_n):
            p = jnp.exp(s - m)
            den = den + p
            o = o + p.astype(BF16).astype(F32) * vnf[:, t:t + 1, :]
        o_scr[pl.ds(s0, chunk)] = o / den
        ok_ref[:, 0:WINDOW - n_new, :] = ck_ref[:, n_new:WINDOW, :]
        ok_ref[:, WINDOW - n_new:WINDOW, :] = kn
        ov_ref[:, 0:WINDOW - n_new, :] = cv_ref[:, n_new:WINDOW, :]
        ov_ref[:, WINDOW - n_new:WINDOW, :] = vn

    @pl.when(i == n_chunks + 1)
    def _():
        a = _rows_to_heads(o_scr[...]).astype(BF16)
        y = (jnp.dot(a, wout_ref[0:D_ATTN, :], preferred_element_type=F32)
             + jnp.dot(mix_scr[...], wout_ref[D_ATTN:, :], preferred_element_type=F32))
        xo_ref[...] = _layer_norm(alpha * x_ref[...] + y, g_ref[...], b_ref[...])


def _sample_mixer(xs, n_seq, n_new, win, wout, convw, state_rows, bias_s, sink_rows, ln_g, ln_b,
                  k_buf, v_buf, layer, alpha):
    assert n_seq % SAMPLE_SEQS == 0
    n = n_seq * n_new
    qr = n_new * N_HEADS
    n_chunks = n_seq // SAMPLE_SEQS
    x_spec = pl.BlockSpec((n, D_MODEL), lambda i: (0, 0))
    lay_spec = pl.BlockSpec((None, SAMPLE_SEQS, WINDOW, D_KV),
                            lambda i: (layer, jnp.clip(i - 1, 0, n_chunks - 1), 0, 0))
    kern = functools.partial(_sample_mixer_kernel, n_new=n_new, n_seq=n_seq, chunk=SAMPLE_SEQS,
                             alpha=alpha)
    return pl.pallas_call(
        kern,
        out_shape=(jax.ShapeDtypeStruct(xs.shape, F32),
                   jax.ShapeDtypeStruct((n, D_CONV), F32),
                   jax.ShapeDtypeStruct(k_buf.shape, F32), jax.ShapeDtypeStruct(v_buf.shape, F32)),
        grid=(n_chunks + 2,),
        in_specs=[x_spec,
                  _const_spec((D_MODEL, D_IN)),
                  _const_spec((D_MODEL, D_MODEL)),
                  _layer_spec((CONV_WIDTH, D_CONV), layer),
                  _layer_spec((n, D_CONV), layer),
                  _layer_spec(bias_s.shape), _layer_spec(sink_rows.shape[1:], layer),
                  _layer_spec((1, D_MODEL), layer, 1), _layer_spec((1, D_MODEL), layer, 1),
                  lay_spec, lay_spec],
        out_specs=(x_spec, pl.BlockSpec((n, D_CONV), lambda i: (0, 0)), lay_spec, lay_spec),
        scratch_shapes=[pltpu.VMEM((n_seq, qr, LANES), BF16),
                        pltpu.VMEM((n_seq, n_new, D_KV), F32),
                        pltpu.VMEM((n_seq, n_new, D_KV), F32),
                        pltpu.VMEM((n_seq, qr, LANES), F32),
                        pltpu.VMEM((n, D_CONV), BF16)],
        input_output_aliases={0: 0, 9: 2, 10: 3},
        compiler_params=_params("arbitrary"),
        name="sample_mixer",
    )(xs, win, wout, convw, state_rows, bias_s, sink_rows, ln_g, ln_b, k_buf, v_buf)


def kernel(x_prompt, x_sample, cache_k, cache_v, state_conv, meta_tokens, rel_bias, w_in, conv_w,
           attn_sink, w_out, ffn_w_gate, ffn_w_up, ffn_w_down, ln_g, ln_b):
    depth = w_in.shape[0]
    alpha = float((2 * depth) ** 0.25)
    n_prompt, seq, d_model = x_prompt.shape
    n_sample, n_new, _ = x_sample.shape
    assert d_model == D_MODEL and seq % MIX_ROWS == 0
    assert n_new >= CONV_WIDTH - 1 and cache_k.shape[2] == WINDOW
    rows_p = n_prompt * seq
    rows_s = n_sample * n_new
    assert rows_s % BLOCK == 0

    xp = x_prompt.reshape(rows_p, D_MODEL)
    xs = jnp.concatenate([x_sample.reshape(rows_s, D_MODEL),
                          jnp.zeros((META_PAD, D_MODEL), F32), meta_tokens.astype(F32)], axis=0)

    bias_p, bias_s = _bias_tables(rel_bias, n_new)
    col = jnp.arange(2 * BLOCK)[None, :]
    fmask_prompt = jnp.where(col < META_PAD, NEG_INF, 0.0).astype(F32)
    fmask_meta = jnp.where(col < BLOCK + META_PAD, NEG_INF, 0.0).astype(F32)
    zeros_kv = jnp.zeros((1, BLOCK, D_KV), F32)
    zeros_u = jnp.zeros((1, SUBLANES, D_CONV), F32)

    ffn_f32 = (ffn_w_gate, ffn_w_up, ffn_w_down)
    w_ffn = tuple(a[0, 0].astype(BF16) for a in ffn_f32)

    def ffn(xp, xs, w_cur, layer, which):
        nxt = (layer, 1) if which == 0 else (layer + 1, 0)
        to_cast = [(a, nxt) for a in ffn_f32] if nxt[0] < depth else []
        if which == 0:
            to_cast += [(w_in, (layer,)), (w_out, (layer,))]
        xp, xs, *cast = _ffn(xp, xs, w_cur, to_cast, ln_g4, ln_b4, layer, 2 * which, alpha)
        return xp, xs, tuple(cast[:3]), tuple(cast[3:])

    ln_g4 = ln_g.reshape(depth, 3, 1, D_MODEL)
    ln_b4 = ln_b.reshape(depth, 3, 1, D_MODEL)
    k_buf = cache_k.reshape(depth, n_sample, WINDOW, D_KV)
    v_buf = cache_v.reshape(depth, n_sample, WINDOW, D_KV)
    state_rows = jnp.pad(state_conv, ((0, 0), (0, 0), (0, n_new - (CONV_WIDTH - 1)), (0, 0))
                         ).reshape(depth, rows_s, D_CONV)
    sink_rows = jnp.tile(attn_sink, (1, n_new)).reshape(depth, n_new * N_HEADS, 1)

    kp, vp, cp, cs = [], [], [], []
    for l in range(depth):
        xp, xs, w_ffn, (win, wout) = ffn(xp, xs, w_ffn, l, 0)

        xs, k_m, v_m, u_m = _mixer(
            xs, rows_s, 1, BLOCK, BLOCK, META_PAD, True, win, wout, conv_w, attn_sink, bias_p,
            fmask_meta, zeros_kv, zeros_kv, zeros_u, ln_g4, ln_b4, l, alpha, "mixer_meta")
        xp, k_p, v_p, u_p = _mixer(
            xp, 0, n_prompt, seq, MIX_ROWS, 0, False, win, wout, conv_w, attn_sink, bias_p,
            fmask_prompt, k_m, v_m, u_m, ln_g4, ln_b4, l, alpha, "mixer_prompt")

        xs, u_s, k_buf, v_buf = _sample_mixer(
            xs, n_sample, n_new, win, wout, conv_w, state_rows, bias_s, sink_rows, ln_g4, ln_b4,
            k_buf, v_buf, l, alpha)

        xp, xs, w_ffn, _ = ffn(xp, xs, w_ffn, l, 1)

        kp.append(k_p)
        vp.append(v_p)
        cp.append(u_p[:, SUBLANES - (CONV_WIDTH - 1):, :])
        cs.append(u_s.reshape(n_sample, n_new, D_CONV)[:, n_new - (CONV_WIDTH - 1):, :])

    kv_shape = (depth, -1, WINDOW, N_KV_HEADS, HEAD_DIM)
    return (xp.reshape(n_prompt, seq, D_MODEL),
            xs[:rows_s].reshape(n_sample, n_new, D_MODEL),
            jnp.stack(kp).reshape(kv_shape), jnp.stack(vp).reshape(kv_shape), jnp.stack(cp),
            k_buf.reshape(kv_shape), v_buf.reshape(kv_shape), jnp.stack(cs))
```

```python
import functools
import math

import jax
import jax.numpy as jnp
from jax import lax
from jax.experimental import pallas as pl
from jax.experimental.pallas import tpu as pltpu

F32 = jnp.float32
BF16 = jnp.bfloat16

D_MODEL = 1024
N_HEADS = 8
N_KV_HEADS = 2
HEAD_DIM = 64
GQA_GROUP = N_HEADS // N_KV_HEADS
D_ATTN = N_HEADS * HEAD_DIM
D_CONV = D_MODEL - D_ATTN
D_KV = N_KV_HEADS * HEAD_DIM
D_IN = D_ATTN + 2 * D_KV + 3 * D_CONV
D_FF = 2816
CONV_WIDTH = 3
WINDOW = 128
BLOCK = 128
N_META = 16
N_BUCKETS = 32
MAX_DISTANCE = 128
LN_EPS = 1e-5
Q_SCALE = HEAD_DIM ** -0.5

V7X_VMEM_LIMIT_BYTES = 60 * 1024 * 1024
V7X_MXU_COLUMNS = 256
SUBLANES = 8
LANES = 128

FFN_ROWS = 1024
FFN_NORM_PARTS = 8
FFN_CHUNK = V7X_MXU_COLUMNS
MIX_ROWS = 512
MIX_NORM_PARTS = 4
SAMPLE_SEQS = 16
META_PAD = BLOCK - N_META

NEG_INF = float("-inf")


def _const_spec(shape):
    nd = len(shape)
    return pl.BlockSpec(shape, lambda *_: (0,) * nd, pipeline_mode=pl.Buffered(1))


def _layer_spec(shape, *lead):
    block = (None,) * len(lead) + tuple(shape)
    idx = tuple(lead) + (0,) * len(shape)
    return pl.BlockSpec(block, lambda *_: idx, pipeline_mode=pl.Buffered(1))


def _params(*sem):
    return pltpu.CompilerParams(dimension_semantics=sem,
                                vmem_limit_bytes=V7X_VMEM_LIMIT_BYTES)


def _layer_norm(r, g, b):
    mu = jnp.mean(r, axis=-1, keepdims=True)
    rc = r - mu
    var = jnp.mean(rc * rc, axis=-1, keepdims=True)
    return rc * lax.rsqrt(var + LN_EPS) * g + b


def _t5_bucket(d):
    d = jnp.maximum(d, 0)
    max_exact = N_BUCKETS // 2
    df = jnp.maximum(d, 1).astype(F32)
    large = max_exact + (jnp.log(df / max_exact) / math.log(MAX_DISTANCE / max_exact)
                         * (N_BUCKETS - max_exact)).astype(jnp.int32)
    large = jnp.minimum(large, N_BUCKETS - 1)
    return jnp.where(d < max_exact, d, large)


def _bias_kernel(tab_ref, bp_ref, bs_ref, *, n_new):
    qi = lax.broadcasted_iota(jnp.int32, (BLOCK, 2 * BLOCK), 0)
    sj = lax.broadcasted_iota(jnp.int32, (BLOCK, 2 * BLOCK), 1)
    d = qi + BLOCK - sj
    valid = (d >= 0) & (d <= WINDOW)
    bk = _t5_bucket(d)
    for h in range(N_HEADS):
        acc = jnp.zeros(d.shape, F32)
        for b in range(N_BUCKETS):
            acc = jnp.where(bk == b, tab_ref[b, h], acc)
        bp_ref[h] = jnp.where(valid, acc, NEG_INF)
    rows, cols = bs_ref.shape
    r = lax.broadcasted_iota(jnp.int32, (rows, cols), 0)
    s = lax.broadcasted_iota(jnp.int32, (rows, cols), 1)
    ds = r // N_HEADS + WINDOW - s
    hs = r % N_HEADS
    valid_s = (ds >= 0) & (ds <= WINDOW) & (s < WINDOW + n_new)
    bks = _t5_bucket(ds)
    acc = jnp.zeros((rows, cols), F32)
    for h in range(N_HEADS):
        for b in range(N_BUCKETS):
            acc = jnp.where((bks == b) & (hs == h), tab_ref[b, h], acc)
    bs_ref[...] = jnp.where(valid_s, acc, NEG_INF)


def _bias_tables(rel_bias, n_new):
    rows = n_new * N_HEADS
    return pl.pallas_call(
        functools.partial(_bias_kernel, n_new=n_new),
        out_shape=(jax.ShapeDtypeStruct((N_HEADS, BLOCK, 2 * BLOCK), F32),
                   jax.ShapeDtypeStruct((rows, WINDOW + SUBLANES), F32)),
        in_specs=[pl.BlockSpec(memory_space=pltpu.SMEM)],
        name="bias_tables",
    )(rel_bias)


def _zero_after(v):
    u = lax.bitcast_convert_type(v, jnp.uint32)
    z = lax.shift_right_logical(lax.shift_right_logical(u, jnp.uint32(16)), jnp.uint32(16))
    return lax.bitcast_convert_type(z, F32)


def _fold_rows(a):
    a = a.reshape(a.shape[0] // 16, 16, a.shape[1]).sum(axis=0)
    return a.reshape(16, a.shape[1] // LANES, LANES).sum(axis=1)


def _ffn_pre_norm(x, wg_ref, wu_ref, wd_ref, act_ref, alpha, side_work):
    rows = x.shape[0]
    xb = x.astype(BF16)
    for ci, c in enumerate(range(0, D_FF, FFN_CHUNK)):
        gate = jnp.dot(xb, wg_ref[:, c:c + FFN_CHUNK], preferred_element_type=F32)
        up = jnp.dot(xb, wu_ref[:, c:c + FFN_CHUNK], preferred_element_type=F32)
        act = jax.nn.silu(gate) * up
        act_ref[0:rows, c:c + FFN_CHUNK] = act.astype(BF16)
        z = side_work(ci)
        if z is not None:
            act_ref[0:16, c:c + LANES] = (act[0:16, 0:LANES] + z).astype(BF16)
    y = jnp.dot(act_ref[0:rows, :], wd_ref[...], preferred_element_type=F32)
    return alpha * x + 0.5 * y


def _ffn_kernel(xp_ref, xs_ref, wg_ref, wu_ref, wd_ref, g_ref, b_ref, *rest, alpha, n_big,
                n_cast):
    f32_refs, rest = rest[:n_cast], rest[n_cast:]
    op_ref, os_ref = rest[:2]
    bf16_refs, (act_ref, r_ref) = rest[2:2 + n_cast], rest[2 + n_cast:]
    i = pl.program_id(0)
    w = (wg_ref, wu_ref, wd_ref)
    g, b = g_ref[...], b_ref[...]
    part = FFN_ROWS // FFN_NORM_PARTS

    def norm_previous_tile(ci):
        if ci >= FFN_NORM_PARTS:
            return None
        rs = slice(ci * part, (ci + 1) * part)
        out = _layer_norm(r_ref[rs, :], g, b)
        op_ref[rs, :] = out
        return _zero_after(_fold_rows(out))

    def norm_and_convert(ci):
        if ci == FFN_NORM_PARTS:
            for src, dst in zip(f32_refs, bf16_refs):
                dst[...] = src[...].astype(BF16)
        return norm_previous_tile(ci)

    @pl.when(i == 0)
    def _():
        r_ref[...] = jnp.zeros(r_ref.shape, F32)

    @pl.when(i < n_big)
    def _():
        r_ref[...] = _ffn_pre_norm(xp_ref[...], *w, act_ref, alpha, norm_and_convert)

    @pl.when(i == n_big)
    def _():
        os_ref[...] = _ffn_pre_norm(xs_ref[...], *w, act_ref, alpha, norm_previous_tile)

    @pl.when(i == n_big + 1)
    def _():
        os_ref[...] = _layer_norm(os_ref[...], g, b)


def _ffn(xp, xs, w_cur, to_cast, ln_g, ln_b, layer, ln_idx, alpha):
    assert xp.shape[0] % FFN_ROWS == 0 and D_FF % FFN_CHUNK == 0
    assert FFN_NORM_PARTS < D_FF // FFN_CHUNK and FFN_ROWS % (16 * FFN_NORM_PARTS) == 0
    n_big = xp.shape[0] // FFN_ROWS
    rows_s = xs.shape[0]
    assert rows_s <= FFN_ROWS
    last = lambda i: jnp.minimum(i, n_big - 1)
    in_spec = pl.BlockSpec((FFN_ROWS, D_MODEL), lambda i: (last(i), 0))
    out_spec = pl.BlockSpec((FFN_ROWS, D_MODEL), lambda i: (jnp.clip(i - 1, 0, n_big - 1), 0))
    small_in = pl.BlockSpec((rows_s, D_MODEL), lambda i: (0, 0), pipeline_mode=pl.Buffered(1))
    small_out = pl.BlockSpec((rows_s, D_MODEL), lambda i: (0, 0), pipeline_mode=pl.Buffered(1))
    in_specs = [in_spec, small_in] + [_const_spec(a.shape) for a in w_cur] + [
        _layer_spec((1, D_MODEL), layer, ln_idx), _layer_spec((1, D_MODEL), layer, ln_idx)]
    out_specs = [out_spec, small_out]
    out_shape = [jax.ShapeDtypeStruct(xp.shape, F32), jax.ShapeDtypeStruct(xs.shape, F32)]
    args = [xp, xs, *w_cur, ln_g, ln_b]
    for a, lead in to_cast:
        r, c = a.shape[-2] // n_big, a.shape[-1]
        assert a.shape[-2] % (16 * n_big) == 0 and len(lead) == a.ndim - 2
        in_specs.append(pl.BlockSpec((None,) * len(lead) + (r, c),
                                     lambda i, lead=lead: lead + (last(i), 0)))
        out_specs.append(pl.BlockSpec((r, c), lambda i: (last(i), 0)))
        out_shape.append(jax.ShapeDtypeStruct(a.shape[-2:], BF16))
        args.append(a)
    return pl.pallas_call(
        functools.partial(_ffn_kernel, alpha=alpha, n_big=n_big, n_cast=len(to_cast)),
        out_shape=out_shape,
        grid=(n_big + 2,),
        in_specs=in_specs,
        out_specs=out_specs,
        scratch_shapes=[pltpu.VMEM((FFN_ROWS, D_FF), BF16), pltpu.VMEM((FFN_ROWS, D_MODEL), F32)],
        compiler_params=_params("arbitrary"),
        name="ffn_ln",
    )(*args)


def _mixer_kernel(x_ref, win_ref, wout_ref, convw_ref, sink_ref, bias_ref, fmask_ref,
                  kinit_ref, vinit_ref, uinit_ref, g_ref, b_ref,
                  xo_ref, klast_ref, vlast_ref, ulast_ref,
                  kd_scr, vd_scr, uscr, ascr, r_scr, *, rows, steps, n_tiles, alpha, layer):
    step = pl.program_id(0)
    g, b = g_ref[...], b_ref[...]
    io_rows = x_ref.shape[0]

    @pl.when(step == 0)
    def _():
        r_scr[...] = jnp.zeros(r_scr.shape, F32)

    @pl.when(step == n_tiles)
    def _():
        xo_ref[...] = _layer_norm(r_scr[rows - io_rows:, :], g, b)

    @pl.when(step < n_tiles)
    def _():
        _mixer_tile(x_ref, win_ref, wout_ref, convw_ref, sink_ref, bias_ref, fmask_ref,
                    kinit_ref, vinit_ref, uinit_ref, g, b, xo_ref, klast_ref, vlast_ref,
                    ulast_ref, kd_scr, vd_scr, uscr, ascr, r_scr, t=step % steps, rows=rows,
                    alpha=alpha, layer=layer)


def _mixer_tile(x_ref, win_ref, wout_ref, convw_ref, sink_ref, bias_ref, fmask_ref,
                kinit_ref, vinit_ref, uinit_ref, g, b, xo_ref, klast_ref, vlast_ref, ulast_ref,
                kd_scr, vd_scr, uscr, ascr, r_scr, *, t, rows, alpha, layer):
    lane = lax.broadcasted_iota(jnp.int32, (1, LANES), 1)
    low = lane < HEAD_DIM

    def dup_heads(a):
        sw = pltpu.roll(a, HEAD_DIM, 1)
        return (jnp.where(low, a, sw).astype(BF16), jnp.where(low, sw, a).astype(BF16))

    @pl.when(t == 0)
    def _():
        for kv, (kd, vd) in enumerate(zip(dup_heads(kinit_ref[0]), dup_heads(vinit_ref[0]))):
            kd_scr[kv, 0:BLOCK, :] = kd
            vd_scr[kv, 0:BLOCK, :] = vd
        uscr[0:SUBLANES, :] = uinit_ref[0]

    x = x_ref[...]
    whole_tile = x.shape[0] == rows
    if not whole_tile:
        x = jnp.concatenate([jnp.zeros((rows - x.shape[0], D_MODEL), F32), x], axis=0)
    xb = x.astype(BF16)
    c0 = D_ATTN + 2 * D_KV
    zq = jnp.dot(xb, win_ref[:, 0:c0], preferred_element_type=F32)
    zc = jnp.dot(xb, win_ref[:, c0:], preferred_element_type=F32)
    k = zq[:, D_ATTN:D_ATTN + D_KV]
    v = zq[:, D_ATTN + D_KV:c0]
    if whole_tile:
        part = rows // MIX_NORM_PARTS
        edges = []
        for ci in range(MIX_NORM_PARTS):
            rs = slice(ci * part, (ci + 1) * part)
            out = _layer_norm(r_scr[rs, :], g, b)
            xo_ref[rs, :] = out
            edges.append(_zero_after(_fold_rows(out)))
        k = jnp.concatenate([k[0:16] + sum(edges[0::2]), k[16:]], axis=0)
        v = jnp.concatenate([v[0:16] + sum(edges[1::2]), v[16:]], axis=0)
    u = zc[:, D_CONV:2 * D_CONV] * zc[:, 2 * D_CONV:3 * D_CONV]
    klast_ref[0] = k[rows - BLOCK:, :]
    vlast_ref[0] = v[rows - BLOCK:, :]
    ulast_ref[0] = u[rows - SUBLANES:, :]
    for kv, (kd, vd) in enumerate(zip(dup_heads(k), dup_heads(v))):
        kd_scr[kv, BLOCK:BLOCK + rows, :] = kd
        vd_scr[kv, BLOCK:BLOCK + rows, :] = vd
    uscr[SUBLANES:SUBLANES + rows, :] = u

    qs = zq[:, :D_ATTN] * Q_SCALE
    lane_q = lax.broadcasted_iota(jnp.int32, (1, D_ATTN), 1) % LANES
    q_even = jnp.where(lane_q < HEAD_DIM, qs, 0.0).astype(BF16)
    q_odd = jnp.where(lane_q < HEAD_DIM, 0.0, qs).astype(BF16)

    first = jnp.where(t == 0, fmask_ref[...], 0.0)
    ones_cols = jnp.ones((2 * BLOCK, LANES), BF16)
    for j in range(rows // BLOCK):
        r0 = j * BLOCK
        for kv in range(N_KV_HEADS):
            kd = kd_scr[kv, r0:r0 + 2 * BLOCK, :]
            vdx = jnp.concatenate([vd_scr[kv, r0:r0 + 2 * BLOCK, :], ones_cols], axis=1)
            heads = range(kv * GQA_GROUP, (kv + 1) * GQA_GROUP)
            q4 = jnp.concatenate(
                [(q_odd if h % 2 else q_even)[r0:r0 + BLOCK, (h // 2) * LANES:(h // 2 + 1) * LANES]
                 for h in heads], axis=0)
            s4 = lax.dot_general(q4, kd, (((1,), (1,)), ((), ())), preferred_element_type=F32)
            ps, es = [], []
            for g, h in enumerate(heads):
                s = s4[g * BLOCK:(g + 1) * BLOCK] + bias_ref[h]
                if j == 0:
                    s = s + first
                sl, sr = s[:, :LANES], s[:, LANES:]
                sk = sink_ref[layer, h]
                m1 = jnp.max(jnp.maximum(sl, sr), axis=-1, keepdims=True)
                mb = jnp.broadcast_to(jnp.maximum(m1, sk), (BLOCK, LANES))
                ps.append(jnp.concatenate([jnp.exp(sl - mb), jnp.exp(sr - mb)],
                                          axis=1).astype(BF16))
                es.append(jnp.exp(sk - mb))
            ox = jnp.dot(jnp.concatenate(ps, axis=0), vdx, preferred_element_type=F32)
            outs = []
            for g in range(GQA_GROUP):
                og = ox[g * BLOCK:(g + 1) * BLOCK]
                outs.append(og[:, :LANES] / (og[:, LANES:] + es[g]))
            for i in range(GQA_GROUP // 2):
                slab = jnp.where(low, outs[2 * i], outs[2 * i + 1])
                col = (kv * GQA_GROUP // 2 + i) * LANES
                ascr[r0:r0 + BLOCK, col:col + LANES] = slab.astype(BF16)

    um2 = uscr[SUBLANES - 2:SUBLANES - 2 + rows, :]
    um1 = uscr[SUBLANES - 1:SUBLANES - 1 + rows, :]
    conv = convw_ref[0:1, :] * um2 + convw_ref[1:2, :] * um1 + convw_ref[2:3, :] * u
    mix = zc[:, 0:D_CONV] * conv
    y = (jnp.dot(ascr[...], wout_ref[0:D_ATTN, :], preferred_element_type=F32)
         + jnp.dot(mix.astype(BF16), wout_ref[D_ATTN:, :], preferred_element_type=F32))
    r_scr[...] = alpha * x + y

    for kv in range(N_KV_HEADS):
        kd_scr[kv, 0:BLOCK, :] = kd_scr[kv, rows:rows + BLOCK, :]
        vd_scr[kv, 0:BLOCK, :] = vd_scr[kv, rows:rows + BLOCK, :]
    uscr[0:SUBLANES, :] = uscr[rows:rows + SUBLANES, :]


def _mixer(x, row0, n_seq, seq_len, rows, in_place, win, wout, convw, sink, bias_p,
           fmask, kinit, vinit, uinit, ln_g, ln_b, layer, alpha, name):
    io_rows = min(rows, seq_len)
    assert seq_len % io_rows == 0 and rows % BLOCK == 0 and row0 % io_rows == 0
    assert rows % (16 * MIX_NORM_PARTS) == 0 and io_rows % SUBLANES == 0
    assert io_rows == rows or n_seq == 1
    steps = seq_len // io_rows
    n_tiles = n_seq * steps
    blk0 = row0 // io_rows
    x_spec = pl.BlockSpec((io_rows, D_MODEL), lambda s: (blk0 + jnp.minimum(s, n_tiles - 1), 0))
    xo_spec = pl.BlockSpec((io_rows, D_MODEL), lambda s: (blk0 + jnp.maximum(s - 1, 0), 0))
    seq_spec = lambda shape: pl.BlockSpec(
        (1,) + shape, lambda s: (jnp.minimum(s // steps, n_seq - 1), 0, 0))
    kern = functools.partial(_mixer_kernel, rows=rows, steps=steps, n_tiles=n_tiles, alpha=alpha,
                             layer=layer)
    return pl.pallas_call(
        kern,
        out_shape=(jax.ShapeDtypeStruct(x.shape, F32),
                   jax.ShapeDtypeStruct((n_seq, BLOCK, D_KV), F32),
                   jax.ShapeDtypeStruct((n_seq, BLOCK, D_KV), F32),
                   jax.ShapeDtypeStruct((n_seq, SUBLANES, D_CONV), F32)),
        grid=(n_tiles + 1,),
        in_specs=[x_spec,
                  _const_spec((D_MODEL, D_IN)),
                  _const_spec((D_MODEL, D_MODEL)),
                  _layer_spec((CONV_WIDTH, D_CONV), layer),
                  pl.BlockSpec(memory_space=pltpu.SMEM),
                  _const_spec(bias_p.shape), _const_spec(fmask.shape),
                  _const_spec(kinit.shape), _const_spec(vinit.shape), _const_spec(uinit.shape),
                  _layer_spec((1, D_MODEL), layer, 1),
                  _layer_spec((1, D_MODEL), layer, 1)],
        out_specs=(xo_spec, seq_spec((BLOCK, D_KV)), seq_spec((BLOCK, D_KV)),
                   seq_spec((SUBLANES, D_CONV))),
        scratch_shapes=[pltpu.VMEM((N_KV_HEADS, BLOCK + rows, D_KV), BF16),
                        pltpu.VMEM((N_KV_HEADS, BLOCK + rows, D_KV), BF16),
                        pltpu.VMEM((SUBLANES + rows, D_CONV), F32),
                        pltpu.VMEM((rows, D_ATTN), BF16),
                        pltpu.VMEM((rows, D_MODEL), F32)],
        input_output_aliases={0: 0} if in_place else {},
        compiler_params=_params("arbitrary"),
        name=name,
    )(x, win, wout, convw, sink, bias_p, fmask, kinit, vinit, uinit, ln_g, ln_b)


def _heads_to_rows(q, n_seq):
    lane = lax.broadcasted_iota(jnp.int32, (1, LANES), 1)
    slabs = []
    for h in range(N_HEADS):
        pair = q[:, (h // 2) * LANES:(h // 2 + 1) * LANES]
        kv = h // GQA_GROUP
        data = pair if h % 2 == kv else pltpu.roll(pair, HEAD_DIM, 1)
        on_kv = (lane >= HEAD_DIM) if kv else (lane < HEAD_DIM)
        slabs.append(jnp.where(on_kv, data, 0.0))
    rows = jnp.concatenate(slabs, axis=1).reshape(q.shape[0], N_HEADS, LANES)
    return rows.reshape(n_seq, q.shape[0] // n_seq * N_HEADS, LANES)


def _rows_to_heads(o):
    n = o.shape[0] * o.shape[1] // N_HEADS
    wide = o.reshape(n, N_HEADS, LANES).reshape(n, N_HEADS * LANES)
    lane = lax.broadcasted_iota(jnp.int32, (1, LANES), 1)
    pairs = []
    for p in range(N_HEADS // 2):
        halves = []
        for e in range(2):
            h = 2 * p + e
            slab = wide[:, h * LANES:(h + 1) * LANES]
            halves.append(slab if h // GQA_GROUP == e else pltpu.roll(slab, HEAD_DIM, 1))
        pairs.append(jnp.where(lane < HEAD_DIM, halves[0], halves[1]))
    return jnp.concatenate(pairs, axis=1)


def _sample_mixer_kernel(x_ref, win_ref, wout_ref, convw_ref, state_ref, bias_ref, sink_ref,
                         g_ref, b_ref, ck_ref, cv_ref,
                         xo_ref, u_ref, ok_ref, ov_ref,
                         q_scr, kn_scr, vn_scr, o_scr, mix_scr, *, n_new, n_seq, chunk, alpha):
    i = pl.program_id(0)
    n_chunks = n_seq // chunk
    c0 = D_ATTN + 2 * D_KV

    @pl.when(i == 0)
    def _():
        z = jnp.dot(x_ref[...].astype(BF16), win_ref[...], preferred_element_type=F32)
        q_scr[...] = _heads_to_rows(z[:, :D_ATTN] * Q_SCALE, n_seq).astype(BF16)
        kn_scr[...] = z[:, D_ATTN:D_ATTN + D_KV].reshape(n_seq, n_new, D_KV)
        vn_scr[...] = z[:, D_ATTN + D_KV:c0].reshape(n_seq, n_new, D_KV)
        u = z[:, c0 + D_CONV:c0 + 2 * D_CONV] * z[:, c0 + 2 * D_CONV:c0 + 3 * D_CONV]
        u_ref[...] = u
        n = u.shape[0]
        tok = lax.broadcasted_iota(jnp.int32, u.shape, 0) % n_new
        st = state_ref[...]
        um2 = jnp.where(tok < 2, st, pltpu.roll(u, 2, 0))
        um1 = jnp.where(tok < 1, pltpu.roll(st, n - 1, 0), pltpu.roll(u, 1, 0))
        conv = convw_ref[0:1, :] * um2 + convw_ref[1:2, :] * um1 + convw_ref[2:3, :] * u
        mix_scr[...] = (z[:, c0:c0 + D_CONV] * conv).astype(BF16)

    @pl.when((i >= 1) & (i <= n_chunks))
    def _():
        s0 = pl.multiple_of((i - 1) * chunk, chunk)
        qb = q_scr[pl.ds(s0, chunk)]
        kn = kn_scr[pl.ds(s0, chunk)]
        vn = vn_scr[pl.ds(s0, chunk)]
        ck = ck_ref[...]
        cv = cv_ref[...]
        bias = bias_ref[...]
        sink = sink_ref[...]
        s_c = jnp.einsum("bqd,bkd->bqk", qb, ck.astype(BF16),
                         preferred_element_type=F32) + bias[None, :, :WINDOW]
        qf = qb.astype(F32)
        knf = kn.astype(BF16).astype(F32)
        vnf = vn.astype(BF16).astype(F32)
        s_n = [jnp.sum(qf * knf[:, t:t + 1, :], axis=-1, keepdims=True)
               + bias[None, :, WINDOW + t:WINDOW + t + 1] for t in range(n_new)]
        m = jnp.maximum(jnp.max(s_c, axis=-1, keepdims=True), sink[None])
        for s in s_n:
            m = jnp.maximum(m, s)
        p_c = jnp.exp(s_c - m)
        den = jnp.sum(p_c, axis=-1, keepdims=True) + jnp.exp(sink[None] - m)
        o = jnp.einsum("bqk,bkd->bqd", p_c.astype(BF16), cv.astype(BF16),
                       preferred_element_type=F32)
        for t, s in enumerate(s_n):
            p = jnp.exp(s - m)
            den = den + p
            o = o + p.astype(BF16).astype(F32) * vnf[:, t:t + 1, :]
        o_scr[pl.ds(s0, chunk)] = o / den
        ok_ref[:, 0:WINDOW - n_new, :] = ck_ref[:, n_new:WINDOW, :]
        ok_ref[:, WINDOW - n_new:WINDOW, :] = kn
        ov_ref[:, 0:WINDOW - n_new, :] = cv_ref[:, n_new:WINDOW, :]
        ov_ref[:, WINDOW - n_new:WINDOW, :] = vn

    @pl.when(i == n_chunks + 1)
    def _():
        a = _rows_to_heads(o_scr[...]).astype(BF16)
        y = (jnp.dot(a, wout_ref[0:D_ATTN, :], preferred_element_type=F32)
             + jnp.dot(mix_scr[...], wout_ref[D_ATTN:, :], preferred_element_type=F32))
        xo_ref[...] = _layer_norm(alpha * x_ref[...] + y, g_ref[...], b_ref[...])


def _sample_mixer(xs, n_seq, n_new, win, wout, convw, state_rows, bias_s, sink_rows, ln_g, ln_b,
                  k_buf, v_buf, layer, alpha):
    assert n_seq % SAMPLE_SEQS == 0
    n = n_seq * n_new
    qr = n_new * N_HEADS
    n_chunks = n_seq // SAMPLE_SEQS
    x_spec = pl.BlockSpec((n, D_MODEL), lambda i: (0, 0))
    lay_spec = pl.BlockSpec((None, SAMPLE_SEQS, WINDOW, D_KV),
                            lambda i: (layer, jnp.clip(i - 1, 0, n_chunks - 1), 0, 0))
    kern = functools.partial(_sample_mixer_kernel, n_new=n_new, n_seq=n_seq, chunk=SAMPLE_SEQS,
                             alpha=alpha)
    return pl.pallas_call(
        kern,
        out_shape=(jax.ShapeDtypeStruct(xs.shape, F32),
                   jax.ShapeDtypeStruct((n, D_CONV), F32),
                   jax.ShapeDtypeStruct(k_buf.shape, F32), jax.ShapeDtypeStruct(v_buf.shape, F32)),
        grid=(n_chunks + 2,),
        in_specs=[x_spec,
                  _const_spec((D_MODEL, D_IN)),
                  _const_spec((D_MODEL, D_MODEL)),
                  _layer_spec((CONV_WIDTH, D_CONV), layer),
                  _layer_spec((n, D_CONV), layer),
                  _layer_spec(bias_s.shape), _layer_spec(sink_rows.shape[1:], layer),
                  _layer_spec((1, D_MODEL), layer, 1), _layer_spec((1, D_MODEL), layer, 1),
                  lay_spec, lay_spec],
        out_specs=(x_spec, pl.BlockSpec((n, D_CONV), lambda i: (0, 0)), lay_spec, lay_spec),
        scratch_shapes=[pltpu.VMEM((n_seq, qr, LANES), BF16),
                        pltpu.VMEM((n_seq, n_new, D_KV), F32),
                        pltpu.VMEM((n_seq, n_new, D_KV), F32),
                        pltpu.VMEM((n_seq, qr, LANES), F32),
                        pltpu.VMEM((n, D_CONV), BF16)],
        input_output_aliases={0: 0, 9: 2, 10: 3},
        compiler_params=_params("arbitrary"),
        name="sample_mixer",
    )(xs, win, wout, convw, state_rows, bias_s, sink_rows, ln_g, ln_b, k_buf, v_buf)


def kernel(x_prompt, x_sample, cache_k, cache_v, state_conv, meta_tokens, rel_bias, w_in, conv_w,
           attn_sink, w_out, ffn_w_gate, ffn_w_up, ffn_w_down, ln_g, ln_b):
    depth = w_in.shape[0]
    alpha = float((2 * depth) ** 0.25)
    n_prompt, seq, d_model = x_prompt.shape
    n_sample, n_new, _ = x_sample.shape
    assert d_model == D_MODEL and seq % MIX_ROWS == 0
    assert n_new >= CONV_WIDTH - 1 and cache_k.shape[2] == WINDOW
    rows_p = n_prompt * seq
    rows_s = n_sample * n_new
    assert rows_s % BLOCK == 0

    xp = x_prompt.reshape(rows_p, D_MODEL)
    xs = jnp.concatenate([x_sample.reshape(rows_s, D_MODEL), meta_tokens.astype(F32)], axis=0)

    bias_p, bias_s = _bias_tables(rel_bias, n_new)
    col = jnp.arange(2 * BLOCK)[None, :]
    fmask_prompt = jnp.where(col < META_PAD, NEG_INF, 0.0).astype(F32)
    fmask_meta = jnp.where(col < BLOCK + META_PAD, NEG_INF, 0.0).astype(F32)
    zeros_kv = jnp.zeros((1, BLOCK, D_KV), F32)
    zeros_u = jnp.zeros((1, SUBLANES, D_CONV), F32)

    ffn_f32 = (ffn_w_gate, ffn_w_up, ffn_w_down)
    w_ffn = tuple(a[0, 0].astype(BF16) for a in ffn_f32)

    def ffn(xp, xs, w_cur, layer, which):
        nxt = (layer, 1) if which == 0 else (layer + 1, 0)
        to_cast = [(a, nxt) for a in ffn_f32] if nxt[0] < depth else []
        if which == 0:
            to_cast += [(w_in, (layer,)), (w_out, (layer,))]
        xp, xs, *cast = _ffn(xp, xs, w_cur, to_cast, ln_g4, ln_b4, layer, 2 * which, alpha)
        return xp, xs, tuple(cast[:3]), tuple(cast[3:])

    ln_g4 = ln_g.reshape(depth, 3, 1, D_MODEL)
    ln_b4 = ln_b.reshape(depth, 3, 1, D_MODEL)
    k_buf = cache_k.reshape(depth, n_sample, WINDOW, D_KV)
    v_buf = cache_v.reshape(depth, n_sample, WINDOW, D_KV)
    state_rows = jnp.pad(state_conv, ((0, 0), (0, 0), (0, n_new - (CONV_WIDTH - 1)), (0, 0))
                         ).reshape(depth, rows_s, D_CONV)
    sink_rows = jnp.tile(attn_sink, (1, n_new)).reshape(depth, n_new * N_HEADS, 1)

    kp, vp, cp, cs = [], [], [], []
    for l in range(depth):
        xp, xs, w_ffn, (win, wout) = ffn(xp, xs, w_ffn, l, 0)

        xs, k_m, v_m, u_m = _mixer(
            xs, rows_s, 1, N_META, BLOCK, True, win, wout, conv_w, attn_sink, bias_p,
            fmask_meta, zeros_kv, zeros_kv, zeros_u, ln_g4, ln_b4, l, alpha, "mixer_meta")
        xp, k_p, v_p, u_p = _mixer(
            xp, 0, n_prompt, seq, MIX_ROWS, False, win, wout, conv_w, attn_sink, bias_p,
            fmask_prompt, k_m, v_m, u_m, ln_g4, ln_b4, l, alpha, "mixer_prompt")

        xs, u_s, k_buf, v_buf = _sample_mixer(
            xs, n_sample, n_new, win, wout, conv_w, state_rows, bias_s, sink_rows, ln_g4, ln_b4,
            k_buf, v_buf, l, alpha)

        xp, xs, w_ffn, _ = ffn(xp, xs, w_ffn, l, 1)

        kp.append(k_p)
        vp.append(v_p)
        cp.append(u_p[:, SUBLANES - (CONV_WIDTH - 1):, :])
        cs.append(u_s.reshape(n_sample, n_new, D_CONV)[:, n_new - (CONV_WIDTH - 1):, :])

    kv_shape = (depth, -1, WINDOW, N_KV_HEADS, HEAD_DIM)
    return (xp.reshape(n_prompt, seq, D_MODEL),
            xs[:rows_s].reshape(n_sample, n_new, D_MODEL),
            jnp.stack(kp).reshape(kv_shape), jnp.stack(vp).reshape(kv_shape), jnp.stack(cp),
            k_buf.reshape(kv_shape), v_buf.reshape(kv_shape), jnp.stack(cs))
```

```python
import functools
import math

import jax
import jax.numpy as jnp
from jax import lax
from jax.experimental import pallas as pl
from jax.experimental.pallas import tpu as pltpu

F32 = jnp.float32
BF16 = jnp.bfloat16

D_MODEL = 1024
N_HEADS = 8
N_KV_HEADS = 2
HEAD_DIM = 64
GQA_GROUP = N_HEADS // N_KV_HEADS
D_ATTN = N_HEADS * HEAD_DIM
D_CONV = D_MODEL - D_ATTN
D_KV = N_KV_HEADS * HEAD_DIM
D_IN = D_ATTN + 2 * D_KV + 3 * D_CONV
D_FF = 2816
CONV_WIDTH = 3
WINDOW = 128
BLOCK = 128
N_META = 16
N_BUCKETS = 32
MAX_DISTANCE = 128
LN_EPS = 1e-5
Q_SCALE = HEAD_DIM ** -0.5

V7X_VMEM_LIMIT_BYTES = 60 * 1024 * 1024
V7X_MXU_COLUMNS = 256
SUBLANES = 8
LANES = 128

FFN_ROWS = 1024
FFN_NORM_PARTS = 8
FFN_CHUNK = V7X_MXU_COLUMNS
MIX_ROWS = 512
MIX_NORM_PARTS = 4
SAMPLE_SEQS = 32
META_PAD = BLOCK - N_META

NEG_INF = float("-inf")


def _const_spec(shape):
    nd = len(shape)
    return pl.BlockSpec(shape, lambda *_: (0,) * nd, pipeline_mode=pl.Buffered(1))


def _layer_spec(shape, *lead):
    block = (None,) * len(lead) + tuple(shape)
    idx = tuple(lead) + (0,) * len(shape)
    return pl.BlockSpec(block, lambda *_: idx, pipeline_mode=pl.Buffered(1))


def _params(*sem):
    return pltpu.CompilerParams(dimension_semantics=sem,
                                vmem_limit_bytes=V7X_VMEM_LIMIT_BYTES)


def _layer_norm(r, g, b):
    mu = jnp.mean(r, axis=-1, keepdims=True)
    rc = r - mu
    var = jnp.mean(rc * rc, axis=-1, keepdims=True)
    return rc * lax.rsqrt(var + LN_EPS) * g + b


def _t5_bucket(d):
    d = jnp.maximum(d, 0)
    max_exact = N_BUCKETS // 2
    df = jnp.maximum(d, 1).astype(F32)
    large = max_exact + (jnp.log(df / max_exact) / math.log(MAX_DISTANCE / max_exact)
                         * (N_BUCKETS - max_exact)).astype(jnp.int32)
    large = jnp.minimum(large, N_BUCKETS - 1)
    return jnp.where(d < max_exact, d, large)


def _bias_kernel(tab_ref, bp_ref, bs_ref, *, n_new):
    qi = lax.broadcasted_iota(jnp.int32, (BLOCK, 2 * BLOCK), 0)
    sj = lax.broadcasted_iota(jnp.int32, (BLOCK, 2 * BLOCK), 1)
    d = qi + BLOCK - sj
    valid = (d >= 0) & (d <= WINDOW)
    bk = _t5_bucket(d)
    for h in range(N_HEADS):
        acc = jnp.zeros(d.shape, F32)
        for b in range(N_BUCKETS):
            acc = jnp.where(bk == b, tab_ref[b, h], acc)
        bp_ref[h] = jnp.where(valid, acc, NEG_INF)
    rows, cols = bs_ref.shape
    r = lax.broadcasted_iota(jnp.int32, (rows, cols), 0)
    s = lax.broadcasted_iota(jnp.int32, (rows, cols), 1)
    ds = r // N_HEADS + WINDOW - s
    hs = r % N_HEADS
    valid_s = (ds >= 0) & (ds <= WINDOW) & (s < WINDOW + n_new)
    bks = _t5_bucket(ds)
    acc = jnp.zeros((rows, cols), F32)
    for h in range(N_HEADS):
        for b in range(N_BUCKETS):
            acc = jnp.where((bks == b) & (hs == h), tab_ref[b, h], acc)
    bs_ref[...] = jnp.where(valid_s, acc, NEG_INF)


def _bias_tables(rel_bias, n_new):
    rows = n_new * N_HEADS
    return pl.pallas_call(
        functools.partial(_bias_kernel, n_new=n_new),
        out_shape=(jax.ShapeDtypeStruct((N_HEADS, BLOCK, 2 * BLOCK), F32),
                   jax.ShapeDtypeStruct((rows, WINDOW + SUBLANES), F32)),
        in_specs=[pl.BlockSpec(memory_space=pltpu.SMEM)],
        name="bias_tables",
    )(rel_bias)


def _zero_after(v):
    u = lax.bitcast_convert_type(v, jnp.uint32)
    z = lax.shift_right_logical(lax.shift_right_logical(u, jnp.uint32(16)), jnp.uint32(16))
    return lax.bitcast_convert_type(z, F32)


def _fold_rows(a):
    a = a.reshape(a.shape[0] // 16, 16, a.shape[1]).sum(axis=0)
    return a.reshape(16, a.shape[1] // LANES, LANES).sum(axis=1)


def _ffn_pre_norm(x, wg_ref, wu_ref, wd_ref, act_ref, alpha, side_work):
    rows = x.shape[0]
    xb = x.astype(BF16)
    for ci, c in enumerate(range(0, D_FF, FFN_CHUNK)):
        gate = jnp.dot(xb, wg_ref[:, c:c + FFN_CHUNK], preferred_element_type=F32)
        up = jnp.dot(xb, wu_ref[:, c:c + FFN_CHUNK], preferred_element_type=F32)
        act = jax.nn.silu(gate) * up
        act_ref[0:rows, c:c + FFN_CHUNK] = act.astype(BF16)
        z = side_work(ci)
        if z is not None:
            act_ref[0:16, c:c + LANES] = (act[0:16, 0:LANES] + z).astype(BF16)
    y = jnp.dot(act_ref[0:rows, :], wd_ref[...], preferred_element_type=F32)
    return alpha * x + 0.5 * y


def _ffn_kernel(xp_ref, xs_ref, wg_ref, wu_ref, wd_ref, g_ref, b_ref, *rest, alpha, n_big,
                n_cast):
    f32_refs, rest = rest[:n_cast], rest[n_cast:]
    op_ref, os_ref = rest[:2]
    bf16_refs, (act_ref, r_ref) = rest[2:2 + n_cast], rest[2 + n_cast:]
    i = pl.program_id(0)
    w = (wg_ref, wu_ref, wd_ref)
    g, b = g_ref[...], b_ref[...]
    part = FFN_ROWS // FFN_NORM_PARTS

    def norm_previous_tile(ci):
        if ci >= FFN_NORM_PARTS:
            return None
        rs = slice(ci * part, (ci + 1) * part)
        out = _layer_norm(r_ref[rs, :], g, b)
        op_ref[rs, :] = out
        return _zero_after(_fold_rows(out))

    def norm_and_convert(ci):
        if ci == FFN_NORM_PARTS:
            for src, dst in zip(f32_refs, bf16_refs):
                dst[...] = src[...].astype(BF16)
        return norm_previous_tile(ci)

    @pl.when(i == 0)
    def _():
        r_ref[...] = jnp.zeros(r_ref.shape, F32)

    @pl.when(i < n_big)
    def _():
        r_ref[...] = _ffn_pre_norm(xp_ref[...], *w, act_ref, alpha, norm_and_convert)

    @pl.when(i == n_big)
    def _():
        os_ref[...] = _ffn_pre_norm(xs_ref[...], *w, act_ref, alpha, norm_previous_tile)

    @pl.when(i == n_big + 1)
    def _():
        os_ref[...] = _layer_norm(os_ref[...], g, b)


def _ffn(xp, xs, w_cur, to_cast, ln_g, ln_b, layer, ln_idx, alpha):
    assert xp.shape[0] % FFN_ROWS == 0 and D_FF % FFN_CHUNK == 0
    assert FFN_NORM_PARTS < D_FF // FFN_CHUNK and FFN_ROWS % (16 * FFN_NORM_PARTS) == 0
    n_big = xp.shape[0] // FFN_ROWS
    rows_s = xs.shape[0]
    assert rows_s <= FFN_ROWS
    last = lambda i: jnp.minimum(i, n_big - 1)
    in_spec = pl.BlockSpec((FFN_ROWS, D_MODEL), lambda i: (last(i), 0))
    out_spec = pl.BlockSpec((FFN_ROWS, D_MODEL), lambda i: (jnp.clip(i - 1, 0, n_big - 1), 0))
    small_in = pl.BlockSpec((rows_s, D_MODEL), lambda i: (0, 0), pipeline_mode=pl.Buffered(1))
    small_out = pl.BlockSpec((rows_s, D_MODEL), lambda i: (0, 0), pipeline_mode=pl.Buffered(1))
    in_specs = [in_spec, small_in] + [_const_spec(a.shape) for a in w_cur] + [
        _layer_spec((1, D_MODEL), layer, ln_idx), _layer_spec((1, D_MODEL), layer, ln_idx)]
    out_specs = [out_spec, small_out]
    out_shape = [jax.ShapeDtypeStruct(xp.shape, F32), jax.ShapeDtypeStruct(xs.shape, F32)]
    args = [xp, xs, *w_cur, ln_g, ln_b]
    for a, lead in to_cast:
        r, c = a.shape[-2] // n_big, a.shape[-1]
        assert a.shape[-2] % (16 * n_big) == 0 and len(lead) == a.ndim - 2
        in_specs.append(pl.BlockSpec((None,) * len(lead) + (r, c),
                                     lambda i, lead=lead: lead + (last(i), 0)))
        out_specs.append(pl.BlockSpec((r, c), lambda i: (last(i), 0)))
        out_shape.append(jax.ShapeDtypeStruct(a.shape[-2:], BF16))
        args.append(a)
    return pl.pallas_call(
        functools.partial(_ffn_kernel, alpha=alpha, n_big=n_big, n_cast=len(to_cast)),
        out_shape=out_shape,
        grid=(n_big + 2,),
        in_specs=in_specs,
        out_specs=out_specs,
        scratch_shapes=[pltpu.VMEM((FFN_ROWS, D_FF), BF16), pltpu.VMEM((FFN_ROWS, D_MODEL), F32)],
        compiler_params=_params("arbitrary"),
        name="ffn_ln",
    )(*args)


def _mixer_kernel(x_ref, win_ref, wout_ref, convw_ref, sink_ref, bias_ref, fmask_ref,
                  kinit_ref, vinit_ref, uinit_ref, g_ref, b_ref, *rest,
                  rows, steps, n_tiles, alpha, layer, n_relaid):
    src_refs, rest = rest[:n_relaid], rest[n_relaid:]
    (xo_ref, klast_ref, vlast_ref, ulast_ref), rest = rest[:4], rest[4:]
    dst_refs, (kd_scr, vd_scr, uscr, ascr, r_scr) = rest[:n_relaid], rest[n_relaid:]
    step = pl.program_id(0)
    g, b = g_ref[...], b_ref[...]
    io_rows = x_ref.shape[0]

    @pl.when(step == 0)
    def _():
        r_scr[...] = jnp.zeros(r_scr.shape, F32)

    @pl.when(step == n_tiles)
    def _():
        xo_ref[...] = _layer_norm(r_scr[rows - io_rows:, :], g, b)

    @pl.when(step < n_tiles)
    def _():
        _mixer_tile(x_ref, win_ref, wout_ref, convw_ref, sink_ref, bias_ref, fmask_ref,
                    kinit_ref, vinit_ref, uinit_ref, g, b, xo_ref, klast_ref, vlast_ref,
                    ulast_ref, kd_scr, vd_scr, uscr, ascr, r_scr, t=step % steps, rows=rows,
                    alpha=alpha, layer=layer)
        for src, dst in zip(src_refs, dst_refs):
            for i in range(src.shape[0]):
                dst[i] = src[i].reshape(D_KV, WINDOW).T


def _mixer_tile(x_ref, win_ref, wout_ref, convw_ref, sink_ref, bias_ref, fmask_ref,
                kinit_ref, vinit_ref, uinit_ref, g, b, xo_ref, klast_ref, vlast_ref, ulast_ref,
                kd_scr, vd_scr, uscr, ascr, r_scr, *, t, rows, alpha, layer):
    lane = lax.broadcasted_iota(jnp.int32, (1, LANES), 1)
    low = lane < HEAD_DIM

    def dup_heads(a):
        sw = pltpu.roll(a, HEAD_DIM, 1)
        return (jnp.where(low, a, sw).astype(BF16), jnp.where(low, sw, a).astype(BF16))

    @pl.when(t == 0)
    def _():
        for kv, (kd, vd) in enumerate(zip(dup_heads(kinit_ref[0]), dup_heads(vinit_ref[0]))):
            kd_scr[kv, 0:BLOCK, :] = kd
            vd_scr[kv, 0:BLOCK, :] = vd
        uscr[0:SUBLANES, :] = uinit_ref[0]

    x = x_ref[...]
    whole_tile = x.shape[0] == rows
    if not whole_tile:
        x = jnp.concatenate([jnp.zeros((rows - x.shape[0], D_MODEL), F32), x], axis=0)
    xb = x.astype(BF16)
    c0 = D_ATTN + 2 * D_KV
    zq = jnp.dot(xb, win_ref[:, 0:c0], preferred_element_type=F32)
    zc = jnp.dot(xb, win_ref[:, c0:], preferred_element_type=F32)
    k = zq[:, D_ATTN:D_ATTN + D_KV]
    v = zq[:, D_ATTN + D_KV:c0]
    if whole_tile:
        part = rows // MIX_NORM_PARTS
        edges = []
        for ci in range(MIX_NORM_PARTS):
            rs = slice(ci * part, (ci + 1) * part)
            out = _layer_norm(r_scr[rs, :], g, b)
            xo_ref[rs, :] = out
            edges.append(_zero_after(_fold_rows(out)))
        k = jnp.concatenate([k[0:16] + sum(edges[0::2]), k[16:]], axis=0)
        v = jnp.concatenate([v[0:16] + sum(edges[1::2]), v[16:]], axis=0)
    u = zc[:, D_CONV:2 * D_CONV] * zc[:, 2 * D_CONV:3 * D_CONV]
    klast_ref[0] = k[rows - BLOCK:, :]
    vlast_ref[0] = v[rows - BLOCK:, :]
    ulast_ref[0] = u[rows - SUBLANES:, :]
    for kv, (kd, vd) in enumerate(zip(dup_heads(k), dup_heads(v))):
        kd_scr[kv, BLOCK:BLOCK + rows, :] = kd
        vd_scr[kv, BLOCK:BLOCK + rows, :] = vd
    uscr[SUBLANES:SUBLANES + rows, :] = u

    qs = zq[:, :D_ATTN] * Q_SCALE
    lane_q = lax.broadcasted_iota(jnp.int32, (1, D_ATTN), 1) % LANES
    q_even = jnp.where(lane_q < HEAD_DIM, qs, 0.0).astype(BF16)
    q_odd = jnp.where(lane_q < HEAD_DIM, 0.0, qs).astype(BF16)

    first = jnp.where(t == 0, fmask_ref[...], 0.0)
    ones_cols = jnp.ones((2 * BLOCK, LANES), BF16)
    for j in range(rows // BLOCK):
        r0 = j * BLOCK
        for kv in range(N_KV_HEADS):
            kd = kd_scr[kv, r0:r0 + 2 * BLOCK, :]
            vdx = jnp.concatenate([vd_scr[kv, r0:r0 + 2 * BLOCK, :], ones_cols], axis=1)
            heads = range(kv * GQA_GROUP, (kv + 1) * GQA_GROUP)
            q4 = jnp.concatenate(
                [(q_odd if h % 2 else q_even)[r0:r0 + BLOCK, (h // 2) * LANES:(h // 2 + 1) * LANES]
                 for h in heads], axis=0)
            s4 = lax.dot_general(q4, kd, (((1,), (1,)), ((), ())), preferred_element_type=F32)
            ps, es = [], []
            for g, h in enumerate(heads):
                s = s4[g * BLOCK:(g + 1) * BLOCK] + bias_ref[h]
                if j == 0:
                    s = s + first
                sl, sr = s[:, :LANES], s[:, LANES:]
                sk = sink_ref[layer, h]
                m1 = jnp.max(jnp.maximum(sl, sr), axis=-1, keepdims=True)
                mb = jnp.broadcast_to(jnp.maximum(m1, sk), (BLOCK, LANES))
                ps.append(jnp.concatenate([jnp.exp(sl - mb), jnp.exp(sr - mb)],
                                          axis=1).astype(BF16))
                es.append(jnp.exp(sk - mb))
            ox = jnp.dot(jnp.concatenate(ps, axis=0), vdx, preferred_element_type=F32)
            outs = []
            for g in range(GQA_GROUP):
                og = ox[g * BLOCK:(g + 1) * BLOCK]
                outs.append(og[:, :LANES] / (og[:, LANES:] + es[g]))
            for i in range(GQA_GROUP // 2):
                slab = jnp.where(low, outs[2 * i], outs[2 * i + 1])
                col = (kv * GQA_GROUP // 2 + i) * LANES
                ascr[r0:r0 + BLOCK, col:col + LANES] = slab.astype(BF16)

    um2 = uscr[SUBLANES - 2:SUBLANES - 2 + rows, :]
    um1 = uscr[SUBLANES - 1:SUBLANES - 1 + rows, :]
    conv = convw_ref[0:1, :] * um2 + convw_ref[1:2, :] * um1 + convw_ref[2:3, :] * u
    mix = zc[:, 0:D_CONV] * conv
    y = (jnp.dot(ascr[...], wout_ref[0:D_ATTN, :], preferred_element_type=F32)
         + jnp.dot(mix.astype(BF16), wout_ref[D_ATTN:, :], preferred_element_type=F32))
    r_scr[...] = alpha * x + y

    for kv in range(N_KV_HEADS):
        kd_scr[kv, 0:BLOCK, :] = kd_scr[kv, rows:rows + BLOCK, :]
        vd_scr[kv, 0:BLOCK, :] = vd_scr[kv, rows:rows + BLOCK, :]
    uscr[0:SUBLANES, :] = uscr[rows:rows + SUBLANES, :]


def _mixer(x, row0, n_seq, seq_len, rows, in_place, win, wout, convw, sink, bias_p,
           fmask, kinit, vinit, uinit, ln_g, ln_b, layer, alpha, name, relay=()):
    io_rows = min(rows, seq_len)
    assert seq_len % io_rows == 0 and rows % BLOCK == 0 and row0 % io_rows == 0
    assert rows % (16 * MIX_NORM_PARTS) == 0 and io_rows % SUBLANES == 0
    assert io_rows == rows or n_seq == 1
    steps = seq_len // io_rows
    n_tiles = n_seq * steps
    blk0 = row0 // io_rows
    x_spec = pl.BlockSpec((io_rows, D_MODEL), lambda s: (blk0 + jnp.minimum(s, n_tiles - 1), 0))
    xo_spec = pl.BlockSpec((io_rows, D_MODEL), lambda s: (blk0 + jnp.maximum(s - 1, 0), 0))
    seq_spec = lambda shape: pl.BlockSpec(
        (1,) + shape, lambda s: (jnp.minimum(s // steps, n_seq - 1), 0, 0))
    kern = functools.partial(_mixer_kernel, rows=rows, steps=steps, n_tiles=n_tiles, alpha=alpha,
                             layer=layer, n_relaid=len(relay))
    relay_in, relay_out, relay_shape = [], [], []
    for a in relay:
        per_step = a.shape[0] // n_tiles
        assert a.shape[0] % n_tiles == 0 and a.shape[1] * a.shape[2] == D_KV
        at_step = lambda s: (jnp.minimum(s, n_tiles - 1), 0, 0)
        relay_in.append(pl.BlockSpec((per_step,) + a.shape[1:], lambda s: at_step(s) + (0,)))
        relay_out.append(pl.BlockSpec((per_step, WINDOW, D_KV), at_step))
        relay_shape.append(jax.ShapeDtypeStruct((a.shape[0], WINDOW, D_KV), F32))
    return pl.pallas_call(
        kern,
        out_shape=(jax.ShapeDtypeStruct(x.shape, F32),
                   jax.ShapeDtypeStruct((n_seq, BLOCK, D_KV), F32),
                   jax.ShapeDtypeStruct((n_seq, BLOCK, D_KV), F32),
                   jax.ShapeDtypeStruct((n_seq, SUBLANES, D_CONV), F32), *relay_shape),
        grid=(n_tiles + 1,),
        in_specs=[x_spec,
                  _const_spec((D_MODEL, D_IN)),
                  _const_spec((D_MODEL, D_MODEL)),
                  _layer_spec((CONV_WIDTH, D_CONV), layer),
                  pl.BlockSpec(memory_space=pltpu.SMEM),
                  _const_spec(bias_p.shape), _const_spec(fmask.shape),
                  _const_spec(kinit.shape), _const_spec(vinit.shape), _const_spec(uinit.shape),
                  _layer_spec((1, D_MODEL), layer, 1),
                  _layer_spec((1, D_MODEL), layer, 1), *relay_in],
        out_specs=(xo_spec, seq_spec((BLOCK, D_KV)), seq_spec((BLOCK, D_KV)),
                   seq_spec((SUBLANES, D_CONV)), *relay_out),
        scratch_shapes=[pltpu.VMEM((N_KV_HEADS, BLOCK + rows, D_KV), BF16),
                        pltpu.VMEM((N_KV_HEADS, BLOCK + rows, D_KV), BF16),
                        pltpu.VMEM((SUBLANES + rows, D_CONV), F32),
                        pltpu.VMEM((rows, D_ATTN), BF16),
                        pltpu.VMEM((rows, D_MODEL), F32)],
        input_output_aliases={0: 0} if in_place else {},
        compiler_params=_params("arbitrary"),
        name=name,
    )(x, win, wout, convw, sink, bias_p, fmask, kinit, vinit, uinit, ln_g, ln_b, *relay)


def _heads_to_rows(q, n_seq):
    lane = lax.broadcasted_iota(jnp.int32, (1, LANES), 1)
    slabs = []
    for h in range(N_HEADS):
        pair = q[:, (h // 2) * LANES:(h // 2 + 1) * LANES]
        kv = h // GQA_GROUP
        data = pair if h % 2 == kv else pltpu.roll(pair, HEAD_DIM, 1)
        on_kv = (lane >= HEAD_DIM) if kv else (lane < HEAD_DIM)
        slabs.append(jnp.where(on_kv, data, 0.0))
    rows = jnp.concatenate(slabs, axis=1).reshape(q.shape[0], N_HEADS, LANES)
    return rows.reshape(n_seq, q.shape[0] // n_seq * N_HEADS, LANES)


def _rows_to_heads(o):
    n = o.shape[0] * o.shape[1] // N_HEADS
    wide = o.reshape(n, N_HEADS, LANES).reshape(n, N_HEADS * LANES)
    lane = lax.broadcasted_iota(jnp.int32, (1, LANES), 1)
    pairs = []
    for p in range(N_HEADS // 2):
        halves = []
        for e in range(2):
            h = 2 * p + e
            slab = wide[:, h * LANES:(h + 1) * LANES]
            halves.append(slab if h // GQA_GROUP == e else pltpu.roll(slab, HEAD_DIM, 1))
        pairs.append(jnp.where(lane < HEAD_DIM, halves[0], halves[1]))
    return jnp.concatenate(pairs, axis=1)


def _sample_mixer_kernel(x_ref, win_ref, wout_ref, convw_ref, state_ref, bias_ref, sink_ref,
                         g_ref, b_ref, ck_ref, cv_ref,
                         xo_ref, u_ref, ok_ref, ov_ref,
                         q_scr, kn_scr, vn_scr, o_scr, mix_scr, *, n_new, n_seq, chunk, alpha):
    i = pl.program_id(0)
    n_chunks = n_seq // chunk
    c0 = D_ATTN + 2 * D_KV

    @pl.when(i == 0)
    def _():
        z = jnp.dot(x_ref[...].astype(BF16), win_ref[...], preferred_element_type=F32)
        q_scr[...] = _heads_to_rows(z[:, :D_ATTN] * Q_SCALE, n_seq).astype(BF16)
        kn_scr[...] = z[:, D_ATTN:D_ATTN + D_KV].reshape(n_seq, n_new, D_KV)
        vn_scr[...] = z[:, D_ATTN + D_KV:c0].reshape(n_seq, n_new, D_KV)
        u = z[:, c0 + D_CONV:c0 + 2 * D_CONV] * z[:, c0 + 2 * D_CONV:c0 + 3 * D_CONV]
        u_ref[...] = u
        n = u.shape[0]
        tok = lax.broadcasted_iota(jnp.int32, u.shape, 0) % n_new
        st = state_ref[...]
        um2 = jnp.where(tok < 2, st, pltpu.roll(u, 2, 0))
        um1 = jnp.where(tok < 1, pltpu.roll(st, n - 1, 0), pltpu.roll(u, 1, 0))
        conv = convw_ref[0:1, :] * um2 + convw_ref[1:2, :] * um1 + convw_ref[2:3, :] * u
        mix_scr[...] = (z[:, c0:c0 + D_CONV] * conv).astype(BF16)

    @pl.when((i >= 1) & (i <= n_chunks))
    def _():
        s0 = pl.multiple_of((i - 1) * chunk, chunk)
        qb = q_scr[pl.ds(s0, chunk)]
        kn = kn_scr[pl.ds(s0, chunk)]
        vn = vn_scr[pl.ds(s0, chunk)]
        ck = ck_ref[...]
        cv = cv_ref[...]
        bias = bias_ref[...]
        sink = sink_ref[...]
        s_c = jnp.einsum("bqd,bkd->bqk", qb, ck.astype(BF16),
                         preferred_element_type=F32) + bias[None, :, :WINDOW]
        qf = qb.astype(F32)
        knf = kn.astype(BF16).astype(F32)
        vnf = vn.astype(BF16).astype(F32)
        s_n = [jnp.sum(qf * knf[:, t:t + 1, :], axis=-1, keepdims=True)
               + bias[None, :, WINDOW + t:WINDOW + t + 1] for t in range(n_new)]
        m = jnp.maximum(jnp.max(s_c, axis=-1, keepdims=True), sink[None])
        for s in s_n:
            m = jnp.maximum(m, s)
        p_c = jnp.exp(s_c - m)
        den = jnp.sum(p_c, axis=-1, keepdims=True) + jnp.exp(sink[None] - m)
        o = jnp.einsum("bqk,bkd->bqd", p_c.astype(BF16), cv.astype(BF16),
                       preferred_element_type=F32)
        for t, s in enumerate(s_n):
            p = jnp.exp(s - m)
            den = den + p
            o = o + p.astype(BF16).astype(F32) * vnf[:, t:t + 1, :]
        o_scr[pl.ds(s0, chunk)] = o / den
        ok_ref[:, 0:WINDOW - n_new, :] = ck_ref[:, n_new:WINDOW, :]
        ok_ref[:, WINDOW - n_new:WINDOW, :] = kn
        ov_ref[:, 0:WINDOW - n_new, :] = cv_ref[:, n_new:WINDOW, :]
        ov_ref[:, WINDOW - n_new:WINDOW, :] = vn

    @pl.when(i == n_chunks + 1)
    def _():
        a = _rows_to_heads(o_scr[...]).astype(BF16)
        y = (jnp.dot(a, wout_ref[0:D_ATTN, :], preferred_element_type=F32)
             + jnp.dot(mix_scr[...], wout_ref[D_ATTN:, :], preferred_element_type=F32))
        xo_ref[...] = _layer_norm(alpha * x_ref[...] + y, g_ref[...], b_ref[...])


def _sample_mixer(xs, n_seq, n_new, win, wout, convw, state_rows, bias_s, sink_rows, ln_g, ln_b,
                  k_buf, v_buf, layer, alpha):
    assert n_seq % SAMPLE_SEQS == 0
    n = n_seq * n_new
    qr = n_new * N_HEADS
    n_chunks = n_seq // SAMPLE_SEQS
    x_spec = pl.BlockSpec((n, D_MODEL), lambda i: (0, 0))
    lay_spec = pl.BlockSpec((None, SAMPLE_SEQS, WINDOW, D_KV),
                            lambda i: (layer, jnp.clip(i - 1, 0, n_chunks - 1), 0, 0))
    kern = functools.partial(_sample_mixer_kernel, n_new=n_new, n_seq=n_seq, chunk=SAMPLE_SEQS,
                             alpha=alpha)
    return pl.pallas_call(
        kern,
        out_shape=(jax.ShapeDtypeStruct(xs.shape, F32),
                   jax.ShapeDtypeStruct((n, D_CONV), F32),
                   jax.ShapeDtypeStruct(k_buf.shape, F32), jax.ShapeDtypeStruct(v_buf.shape, F32)),
        grid=(n_chunks + 2,),
        in_specs=[x_spec,
                  _const_spec((D_MODEL, D_IN)),
                  _const_spec((D_MODEL, D_MODEL)),
                  _layer_spec((CONV_WIDTH, D_CONV), layer),
                  _layer_spec((n, D_CONV), layer),
                  _layer_spec(bias_s.shape), _layer_spec(sink_rows.shape[1:], layer),
                  _layer_spec((1, D_MODEL), layer, 1), _layer_spec((1, D_MODEL), layer, 1),
                  lay_spec, lay_spec],
        out_specs=(x_spec, pl.BlockSpec((n, D_CONV), lambda i: (0, 0)), lay_spec, lay_spec),
        scratch_shapes=[pltpu.VMEM((n_seq, qr, LANES), BF16),
                        pltpu.VMEM((n_seq, n_new, D_KV), F32),
                        pltpu.VMEM((n_seq, n_new, D_KV), F32),
                        pltpu.VMEM((n_seq, qr, LANES), F32),
                        pltpu.VMEM((n, D_CONV), BF16)],
        input_output_aliases={0: 0, 9: 2, 10: 3},
        compiler_params=_params("arbitrary"),
        name="sample_mixer",
    )(xs, win, wout, convw, state_rows, bias_s, sink_rows, ln_g, ln_b, k_buf, v_buf)


def kernel(x_prompt, x_sample, cache_k, cache_v, state_conv, meta_tokens, rel_bias, w_in, conv_w,
           attn_sink, w_out, ffn_w_gate, ffn_w_up, ffn_w_down, ln_g, ln_b):
    depth = w_in.shape[0]
    alpha = float((2 * depth) ** 0.25)
    n_prompt, seq, d_model = x_prompt.shape
    n_sample, n_new, _ = x_sample.shape
    assert d_model == D_MODEL and seq % MIX_ROWS == 0
    assert n_new >= CONV_WIDTH - 1 and cache_k.shape[2] == WINDOW
    rows_p = n_prompt * seq
    rows_s = n_sample * n_new
    assert rows_s % BLOCK == 0

    xp = x_prompt.reshape(rows_p, D_MODEL)
    xs = jnp.concatenate([x_sample.reshape(rows_s, D_MODEL), meta_tokens.astype(F32)], axis=0)

    bias_p, bias_s = _bias_tables(rel_bias, n_new)
    col = jnp.arange(2 * BLOCK)[None, :]
    fmask_prompt = jnp.where(col < META_PAD, NEG_INF, 0.0).astype(F32)
    fmask_meta = jnp.where(col < BLOCK + META_PAD, NEG_INF, 0.0).astype(F32)
    zeros_kv = jnp.zeros((1, BLOCK, D_KV), F32)
    zeros_u = jnp.zeros((1, SUBLANES, D_CONV), F32)

    ffn_f32 = (ffn_w_gate, ffn_w_up, ffn_w_down)
    w_ffn = tuple(a[0, 0].astype(BF16) for a in ffn_f32)

    def ffn(xp, xs, w_cur, layer, which):
        nxt = (layer, 1) if which == 0 else (layer + 1, 0)
        to_cast = [(a, nxt) for a in ffn_f32] if nxt[0] < depth else []
        if which == 0:
            to_cast += [(w_in, (layer,)), (w_out, (layer,))]
        xp, xs, *cast = _ffn(xp, xs, w_cur, to_cast, ln_g4, ln_b4, layer, 2 * which, alpha)
        return xp, xs, tuple(cast[:3]), tuple(cast[3:])

    ln_g4 = ln_g.reshape(depth, 3, 1, D_MODEL)
    ln_b4 = ln_b.reshape(depth, 3, 1, D_MODEL)
    relay = tuple(jnp.transpose(c, (0, 1, 3, 4, 2)).reshape(-1, N_KV_HEADS, HEAD_DIM, WINDOW)
                  for c in (cache_k, cache_v))
    state_rows = jnp.pad(state_conv, ((0, 0), (0, 0), (0, n_new - (CONV_WIDTH - 1)), (0, 0))
                         ).reshape(depth, rows_s, D_CONV)
    sink_rows = jnp.tile(attn_sink, (1, n_new)).reshape(depth, n_new * N_HEADS, 1)

    kp, vp, cp, cs = [], [], [], []
    for l in range(depth):
        xp, xs, w_ffn, (win, wout) = ffn(xp, xs, w_ffn, l, 0)

        xs, k_m, v_m, u_m = _mixer(
            xs, rows_s, 1, N_META, BLOCK, True, win, wout, conv_w, attn_sink, bias_p,
            fmask_meta, zeros_kv, zeros_kv, zeros_u, ln_g4, ln_b4, l, alpha, "mixer_meta")
        xp, k_p, v_p, u_p, *relaid = _mixer(
            xp, 0, n_prompt, seq, MIX_ROWS, False, win, wout, conv_w, attn_sink, bias_p,
            fmask_prompt, k_m, v_m, u_m, ln_g4, ln_b4, l, alpha, "mixer_prompt",
            relay if l == 0 else ())
        if relaid:
            k_buf, v_buf = (a.reshape(depth, n_sample, WINDOW, D_KV) for a in relaid)

        xs, u_s, k_buf, v_buf = _sample_mixer(
            xs, n_sample, n_new, win, wout, conv_w, state_rows, bias_s, sink_rows, ln_g4, ln_b4,
            k_buf, v_buf, l, alpha)

        xp, xs, w_ffn, _ = ffn(xp, xs, w_ffn, l, 1)

        kp.append(k_p)
        vp.append(v_p)
        cp.append(u_p[:, SUBLANES - (CONV_WIDTH - 1):, :])
        cs.append(u_s.reshape(n_sample, n_new, D_CONV)[:, n_new - (CONV_WIDTH - 1):, :])

    kv_shape = (depth, -1, WINDOW, N_KV_HEADS, HEAD_DIM)
    return (xp.reshape(n_prompt, seq, D_MODEL),
            xs[:rows_s].reshape(n_sample, n_new, D_MODEL),
            jnp.stack(kp).reshape(kv_shape), jnp.stack(vp).reshape(kv_shape), jnp.stack(cp),
            k_buf.reshape(kv_shape), v_buf.reshape(kv_shape), jnp.stack(cs))
```

```python
import functools
import math

import jax
import jax.numpy as jnp
from jax import lax
from jax.experimental import pallas as pl
from jax.experimental.pallas import tpu as pltpu

F32 = jnp.float32
BF16 = jnp.bfloat16

D_MODEL = 1024
N_HEADS = 8
N_KV_HEADS = 2
HEAD_DIM = 64
GQA_GROUP = N_HEADS // N_KV_HEADS
D_ATTN = N_HEADS * HEAD_DIM
D_CONV = D_MODEL - D_ATTN
D_KV = N_KV_HEADS * HEAD_DIM
D_IN = D_ATTN + 2 * D_KV + 3 * D_CONV
D_FF = 2816
CONV_WIDTH = 3
WINDOW = 128
BLOCK = 128
N_META = 16
N_BUCKETS = 32
MAX_DISTANCE = 128
LN_EPS = 1e-5
Q_SCALE = HEAD_DIM ** -0.5

V7X_VMEM_LIMIT_BYTES = 60 * 1024 * 1024
V7X_MXU_COLUMNS = 256
SUBLANES = 8
LANES = 128

FFN_ROWS = 1024
FFN_NORM_PARTS = 8
FFN_CHUNK = V7X_MXU_COLUMNS
MIX_ROWS = 512
MIX_NORM_PARTS = 4
SAMPLE_SEQS = 32
META_PAD = BLOCK - N_META

NEG_INF = float("-inf")


def _const_spec(shape):
    nd = len(shape)
    return pl.BlockSpec(shape, lambda *_: (0,) * nd, pipeline_mode=pl.Buffered(1))


def _layer_spec(shape, *lead):
    block = (None,) * len(lead) + tuple(shape)
    idx = tuple(lead) + (0,) * len(shape)
    return pl.BlockSpec(block, lambda *_: idx, pipeline_mode=pl.Buffered(1))


def _params(*sem):
    return pltpu.CompilerParams(dimension_semantics=sem,
                                vmem_limit_bytes=V7X_VMEM_LIMIT_BYTES)


def _layer_norm(r, g, b):
    mu = jnp.mean(r, axis=-1, keepdims=True)
    rc = r - mu
    var = jnp.mean(rc * rc, axis=-1, keepdims=True)
    return rc * lax.rsqrt(var + LN_EPS) * g + b


def _t5_bucket(d):
    d = jnp.maximum(d, 0)
    max_exact = N_BUCKETS // 2
    df = jnp.maximum(d, 1).astype(F32)
    large = max_exact + (jnp.log(df / max_exact) / math.log(MAX_DISTANCE / max_exact)
                         * (N_BUCKETS - max_exact)).astype(jnp.int32)
    large = jnp.minimum(large, N_BUCKETS - 1)
    return jnp.where(d < max_exact, d, large)


def _bias_kernel(tab_ref, bp_ref, bs_ref, *, n_new):
    qi = lax.broadcasted_iota(jnp.int32, (BLOCK, 2 * BLOCK), 0)
    sj = lax.broadcasted_iota(jnp.int32, (BLOCK, 2 * BLOCK), 1)
    d = qi + BLOCK - sj
    valid = (d >= 0) & (d <= WINDOW)
    bk = _t5_bucket(d)
    for h in range(N_HEADS):
        acc = jnp.zeros(d.shape, F32)
        for b in range(N_BUCKETS):
            acc = jnp.where(bk == b, tab_ref[b, h], acc)
        bp_ref[h] = jnp.where(valid, acc, NEG_INF)
    rows, cols = bs_ref.shape
    r = lax.broadcasted_iota(jnp.int32, (rows, cols), 0)
    s = lax.broadcasted_iota(jnp.int32, (rows, cols), 1)
    s_pos = jnp.where(s < WINDOW, (s + n_new) % WINDOW, s)
    ds = r // N_HEADS + WINDOW - s_pos
    hs = r % N_HEADS
    valid_s = (ds >= 0) & (ds <= WINDOW) & (s < WINDOW + n_new)
    bks = _t5_bucket(ds)
    acc = jnp.zeros((rows, cols), F32)
    for h in range(N_HEADS):
        for b in range(N_BUCKETS):
            acc = jnp.where((bks == b) & (hs == h), tab_ref[b, h], acc)
    bs_ref[...] = jnp.where(valid_s, acc, NEG_INF)


def _bias_tables(rel_bias, n_new):
    rows = n_new * N_HEADS
    return pl.pallas_call(
        functools.partial(_bias_kernel, n_new=n_new),
        out_shape=(jax.ShapeDtypeStruct((N_HEADS, BLOCK, 2 * BLOCK), F32),
                   jax.ShapeDtypeStruct((rows, WINDOW + SUBLANES), F32)),
        in_specs=[pl.BlockSpec(memory_space=pltpu.SMEM)],
        name="bias_tables",
    )(rel_bias)


def _zero_after(v):
    u = lax.bitcast_convert_type(v, jnp.uint32)
    z = lax.shift_right_logical(lax.shift_right_logical(u, jnp.uint32(16)), jnp.uint32(16))
    return lax.bitcast_convert_type(z, F32)


def _fold_rows(a):
    a = a.reshape(a.shape[0] // 16, 16, a.shape[1]).sum(axis=0)
    return a.reshape(16, a.shape[1] // LANES, LANES).sum(axis=1)


def _ffn_pre_norm(x, wg_ref, wu_ref, wd_ref, act_ref, alpha, side_work):
    rows = x.shape[0]
    xb = x.astype(BF16)
    for ci, c in enumerate(range(0, D_FF, FFN_CHUNK)):
        gate = jnp.dot(xb, wg_ref[:, c:c + FFN_CHUNK], preferred_element_type=F32)
        up = jnp.dot(xb, wu_ref[:, c:c + FFN_CHUNK], preferred_element_type=F32)
        act = jax.nn.silu(gate) * up
        act_ref[0:rows, c:c + FFN_CHUNK] = act.astype(BF16)
        z = side_work(ci)
        if z is not None:
            act_ref[0:16, c:c + LANES] = (act[0:16, 0:LANES] + z).astype(BF16)
    y = jnp.dot(act_ref[0:rows, :], wd_ref[...], preferred_element_type=F32)
    return alpha * x + 0.5 * y


def _ffn_kernel(xp_ref, xs_ref, wg_ref, wu_ref, wd_ref, g_ref, b_ref, *rest, alpha, n_big,
                n_cast):
    f32_refs, rest = rest[:n_cast], rest[n_cast:]
    op_ref, os_ref = rest[:2]
    bf16_refs, (act_ref, r_ref) = rest[2:2 + n_cast], rest[2 + n_cast:]
    i = pl.program_id(0)
    w = (wg_ref, wu_ref, wd_ref)
    g, b = g_ref[...], b_ref[...]
    part = FFN_ROWS // FFN_NORM_PARTS

    def norm_previous_tile(ci):
        if ci >= FFN_NORM_PARTS:
            return None
        rs = slice(ci * part, (ci + 1) * part)
        out = _layer_norm(r_ref[rs, :], g, b)
        op_ref[rs, :] = out
        return _zero_after(_fold_rows(out))

    def norm_and_convert(ci):
        if ci == FFN_NORM_PARTS:
            for src, dst in zip(f32_refs, bf16_refs):
                dst[...] = src[...].astype(BF16)
        return norm_previous_tile(ci)

    @pl.when(i == 0)
    def _():
        r_ref[...] = jnp.zeros(r_ref.shape, F32)

    @pl.when(i < n_big)
    def _():
        r_ref[...] = _ffn_pre_norm(xp_ref[...], *w, act_ref, alpha, norm_and_convert)

    @pl.when(i == n_big)
    def _():
        os_ref[...] = _ffn_pre_norm(xs_ref[...], *w, act_ref, alpha, norm_previous_tile)

    @pl.when(i == n_big + 1)
    def _():
        os_ref[...] = _layer_norm(os_ref[...], g, b)


def _ffn(xp, xs, w_cur, to_cast, ln_g, ln_b, layer, ln_idx, alpha):
    assert xp.shape[0] % FFN_ROWS == 0 and D_FF % FFN_CHUNK == 0
    assert FFN_NORM_PARTS < D_FF // FFN_CHUNK and FFN_ROWS % (16 * FFN_NORM_PARTS) == 0
    n_big = xp.shape[0] // FFN_ROWS
    rows_s = xs.shape[0]
    assert rows_s <= FFN_ROWS
    last = lambda i: jnp.minimum(i, n_big - 1)
    in_spec = pl.BlockSpec((FFN_ROWS, D_MODEL), lambda i: (last(i), 0))
    out_spec = pl.BlockSpec((FFN_ROWS, D_MODEL), lambda i: (jnp.clip(i - 1, 0, n_big - 1), 0))
    small_in = pl.BlockSpec((rows_s, D_MODEL), lambda i: (0, 0), pipeline_mode=pl.Buffered(1))
    small_out = pl.BlockSpec((rows_s, D_MODEL), lambda i: (0, 0), pipeline_mode=pl.Buffered(1))
    in_specs = [in_spec, small_in] + [_const_spec(a.shape) for a in w_cur] + [
        _layer_spec((1, D_MODEL), layer, ln_idx), _layer_spec((1, D_MODEL), layer, ln_idx)]
    out_specs = [out_spec, small_out]
    out_shape = [jax.ShapeDtypeStruct(xp.shape, F32), jax.ShapeDtypeStruct(xs.shape, F32)]
    args = [xp, xs, *w_cur, ln_g, ln_b]
    for a, lead in to_cast:
        r, c = a.shape[-2] // n_big, a.shape[-1]
        assert a.shape[-2] % (16 * n_big) == 0 and len(lead) == a.ndim - 2
        in_specs.append(pl.BlockSpec((None,) * len(lead) + (r, c),
                                     lambda i, lead=lead: lead + (last(i), 0)))
        out_specs.append(pl.BlockSpec((r, c), lambda i: (last(i), 0)))
        out_shape.append(jax.ShapeDtypeStruct(a.shape[-2:], BF16))
        args.append(a)
    return pl.pallas_call(
        functools.partial(_ffn_kernel, alpha=alpha, n_big=n_big, n_cast=len(to_cast)),
        out_shape=out_shape,
        grid=(n_big + 2,),
        in_specs=in_specs,
        out_specs=out_specs,
        scratch_shapes=[pltpu.VMEM((FFN_ROWS, D_FF), BF16), pltpu.VMEM((FFN_ROWS, D_MODEL), F32)],
        compiler_params=_params("arbitrary"),
        name="ffn_ln",
    )(*args)


def _mixer_kernel(x_ref, win_ref, wout_ref, convw_ref, sink_ref, bias_ref, fmask_ref,
                  kinit_ref, vinit_ref, uinit_ref, g_ref, b_ref, *rest,
                  rows, steps, n_tiles, alpha, layer, n_relaid, rotate):
    src_refs, rest = rest[:n_relaid], rest[n_relaid:]
    (xo_ref, klast_ref, vlast_ref, ulast_ref), rest = rest[:4], rest[4:]
    dst_refs, (kd_scr, vd_scr, uscr, ascr, r_scr) = rest[:n_relaid], rest[n_relaid:]
    step = pl.program_id(0)
    g, b = g_ref[...], b_ref[...]
    io_rows = x_ref.shape[0]

    @pl.when(step == 0)
    def _():
        r_scr[...] = jnp.zeros(r_scr.shape, F32)

    @pl.when(step == n_tiles)
    def _():
        xo_ref[...] = _layer_norm(r_scr[rows - io_rows:, :], g, b)

    @pl.when(step < n_tiles)
    def _():
        _mixer_tile(x_ref, win_ref, wout_ref, convw_ref, sink_ref, bias_ref, fmask_ref,
                    kinit_ref, vinit_ref, uinit_ref, g, b, xo_ref, klast_ref, vlast_ref,
                    ulast_ref, kd_scr, vd_scr, uscr, ascr, r_scr, t=step % steps, rows=rows,
                    alpha=alpha, layer=layer)
        for src, dst in zip(src_refs, dst_refs):
            for i in range(src.shape[0]):
                buf = src[i].reshape(D_KV, WINDOW).T
                dst[i, 0:WINDOW - rotate, :] = buf[rotate:, :]
                dst[i, WINDOW - rotate:, :] = buf[0:rotate, :]


def _mixer_tile(x_ref, win_ref, wout_ref, convw_ref, sink_ref, bias_ref, fmask_ref,
                kinit_ref, vinit_ref, uinit_ref, g, b, xo_ref, klast_ref, vlast_ref, ulast_ref,
                kd_scr, vd_scr, uscr, ascr, r_scr, *, t, rows, alpha, layer):
    lane = lax.broadcasted_iota(jnp.int32, (1, LANES), 1)
    low = lane < HEAD_DIM

    def dup_heads(a):
        sw = pltpu.roll(a, HEAD_DIM, 1)
        return (jnp.where(low, a, sw).astype(BF16), jnp.where(low, sw, a).astype(BF16))

    @pl.when(t == 0)
    def _():
        for kv, (kd, vd) in enumerate(zip(dup_heads(kinit_ref[0]), dup_heads(vinit_ref[0]))):
            kd_scr[kv, 0:BLOCK, :] = kd
            vd_scr[kv, 0:BLOCK, :] = vd
        uscr[0:SUBLANES, :] = uinit_ref[0]

    x = x_ref[...]
    whole_tile = x.shape[0] == rows
    if not whole_tile:
        x = jnp.concatenate([jnp.zeros((rows - x.shape[0], D_MODEL), F32), x], axis=0)
    xb = x.astype(BF16)
    c0 = D_ATTN + 2 * D_KV
    zq = jnp.dot(xb, win_ref[:, 0:c0], preferred_element_type=F32)
    zc = jnp.dot(xb, win_ref[:, c0:], preferred_element_type=F32)
    k = zq[:, D_ATTN:D_ATTN + D_KV]
    v = zq[:, D_ATTN + D_KV:c0]
    if whole_tile:
        part = rows // MIX_NORM_PARTS
        edges = []
        for ci in range(MIX_NORM_PARTS):
            rs = slice(ci * part, (ci + 1) * part)
            out = _layer_norm(r_scr[rs, :], g, b)
            xo_ref[rs, :] = out
            edges.append(_zero_after(_fold_rows(out)))
        k = jnp.concatenate([k[0:16] + sum(edges[0::2]), k[16:]], axis=0)
        v = jnp.concatenate([v[0:16] + sum(edges[1::2]), v[16:]], axis=0)
    u = zc[:, D_CONV:2 * D_CONV] * zc[:, 2 * D_CONV:3 * D_CONV]
    klast_ref[0] = k[rows - BLOCK:, :]
    vlast_ref[0] = v[rows - BLOCK:, :]
    ulast_ref[0] = u[rows - SUBLANES:, :]
    for kv, (kd, vd) in enumerate(zip(dup_heads(k), dup_heads(v))):
        kd_scr[kv, BLOCK:BLOCK + rows, :] = kd
        vd_scr[kv, BLOCK:BLOCK + rows, :] = vd
    uscr[SUBLANES:SUBLANES + rows, :] = u

    qs = zq[:, :D_ATTN] * Q_SCALE
    lane_q = lax.broadcasted_iota(jnp.int32, (1, D_ATTN), 1) % LANES
    q_even = jnp.where(lane_q < HEAD_DIM, qs, 0.0).astype(BF16)
    q_odd = jnp.where(lane_q < HEAD_DIM, 0.0, qs).astype(BF16)

    first = jnp.where(t == 0, fmask_ref[...], 0.0)
    ones_cols = jnp.ones((2 * BLOCK, LANES), BF16)
    for j in range(rows // BLOCK):
        r0 = j * BLOCK
        for kv in range(N_KV_HEADS):
            kd = kd_scr[kv, r0:r0 + 2 * BLOCK, :]
            vdx = jnp.concatenate([vd_scr[kv, r0:r0 + 2 * BLOCK, :], ones_cols], axis=1)
            heads = range(kv * GQA_GROUP, (kv + 1) * GQA_GROUP)
            q4 = jnp.concatenate(
                [(q_odd if h % 2 else q_even)[r0:r0 + BLOCK, (h // 2) * LANES:(h // 2 + 1) * LANES]
                 for h in heads], axis=0)
            s4 = lax.dot_general(q4, kd, (((1,), (1,)), ((), ())), preferred_element_type=F32)
            ps, es = [], []
            for g, h in enumerate(heads):
                s = s4[g * BLOCK:(g + 1) * BLOCK] + bias_ref[h]
                if j == 0:
                    s = s + first
                sl, sr = s[:, :LANES], s[:, LANES:]
                sk = sink_ref[layer, h]
                m1 = jnp.max(jnp.maximum(sl, sr), axis=-1, keepdims=True)
                mb = jnp.broadcast_to(jnp.maximum(m1, sk), (BLOCK, LANES))
                ps.append(jnp.concatenate([jnp.exp(sl - mb), jnp.exp(sr - mb)],
                                          axis=1).astype(BF16))
                es.append(jnp.exp(sk - mb))
            ox = jnp.dot(jnp.concatenate(ps, axis=0), vdx, preferred_element_type=F32)
            outs = []
            for g in range(GQA_GROUP):
                og = ox[g * BLOCK:(g + 1) * BLOCK]
                outs.append(og[:, :LANES] / (og[:, LANES:] + es[g]))
            for i in range(GQA_GROUP // 2):
                slab = jnp.where(low, outs[2 * i], outs[2 * i + 1])
                col = (kv * GQA_GROUP // 2 + i) * LANES
                ascr[r0:r0 + BLOCK, col:col + LANES] = slab.astype(BF16)

    um2 = uscr[SUBLANES - 2:SUBLANES - 2 + rows, :]
    um1 = uscr[SUBLANES - 1:SUBLANES - 1 + rows, :]
    conv = convw_ref[0:1, :] * um2 + convw_ref[1:2, :] * um1 + convw_ref[2:3, :] * u
    mix = zc[:, 0:D_CONV] * conv
    y = (jnp.dot(ascr[...], wout_ref[0:D_ATTN, :], preferred_element_type=F32)
         + jnp.dot(mix.astype(BF16), wout_ref[D_ATTN:, :], preferred_element_type=F32))
    r_scr[...] = alpha * x + y

    for kv in range(N_KV_HEADS):
        kd_scr[kv, 0:BLOCK, :] = kd_scr[kv, rows:rows + BLOCK, :]
        vd_scr[kv, 0:BLOCK, :] = vd_scr[kv, rows:rows + BLOCK, :]
    uscr[0:SUBLANES, :] = uscr[rows:rows + SUBLANES, :]


def _mixer(x, row0, n_seq, seq_len, rows, in_place, win, wout, convw, sink, bias_p,
           fmask, kinit, vinit, uinit, ln_g, ln_b, layer, alpha, name, relay=(), rotate=0):
    io_rows = min(rows, seq_len)
    assert seq_len % io_rows == 0 and rows % BLOCK == 0 and row0 % io_rows == 0
    assert rows % (16 * MIX_NORM_PARTS) == 0 and io_rows % SUBLANES == 0
    assert io_rows == rows or n_seq == 1
    steps = seq_len // io_rows
    n_tiles = n_seq * steps
    blk0 = row0 // io_rows
    x_spec = pl.BlockSpec((io_rows, D_MODEL), lambda s: (blk0 + jnp.minimum(s, n_tiles - 1), 0))
    xo_spec = pl.BlockSpec((io_rows, D_MODEL), lambda s: (blk0 + jnp.maximum(s - 1, 0), 0))
    seq_spec = lambda shape: pl.BlockSpec(
        (1,) + shape, lambda s: (jnp.minimum(s // steps, n_seq - 1), 0, 0))
    kern = functools.partial(_mixer_kernel, rows=rows, steps=steps, n_tiles=n_tiles, alpha=alpha,
                             layer=layer, n_relaid=len(relay), rotate=rotate)
    relay_in, relay_out, relay_shape = [], [], []
    for a in relay:
        per_step = a.shape[0] // n_tiles
        assert a.shape[0] % n_tiles == 0 and a.shape[1] * a.shape[2] == D_KV
        at_step = lambda s: (jnp.minimum(s, n_tiles - 1), 0, 0)
        relay_in.append(pl.BlockSpec((per_step,) + a.shape[1:], lambda s: at_step(s) + (0,)))
        relay_out.append(pl.BlockSpec((per_step, WINDOW, D_KV), at_step))
        relay_shape.append(jax.ShapeDtypeStruct((a.shape[0], WINDOW, D_KV), F32))
    return pl.pallas_call(
        kern,
        out_shape=(jax.ShapeDtypeStruct(x.shape, F32),
                   jax.ShapeDtypeStruct((n_seq, BLOCK, D_KV), F32),
                   jax.ShapeDtypeStruct((n_seq, BLOCK, D_KV), F32),
                   jax.ShapeDtypeStruct((n_seq, SUBLANES, D_CONV), F32), *relay_shape),
        grid=(n_tiles + 1,),
        in_specs=[x_spec,
                  _const_spec((D_MODEL, D_IN)),
                  _const_spec((D_MODEL, D_MODEL)),
                  _layer_spec((CONV_WIDTH, D_CONV), layer),
                  pl.BlockSpec(memory_space=pltpu.SMEM),
                  _const_spec(bias_p.shape), _const_spec(fmask.shape),
                  _const_spec(kinit.shape), _const_spec(vinit.shape), _const_spec(uinit.shape),
                  _layer_spec((1, D_MODEL), layer, 1),
                  _layer_spec((1, D_MODEL), layer, 1), *relay_in],
        out_specs=(xo_spec, seq_spec((BLOCK, D_KV)), seq_spec((BLOCK, D_KV)),
                   seq_spec((SUBLANES, D_CONV)), *relay_out),
        scratch_shapes=[pltpu.VMEM((N_KV_HEADS, BLOCK + rows, D_KV), BF16),
                        pltpu.VMEM((N_KV_HEADS, BLOCK + rows, D_KV), BF16),
                        pltpu.VMEM((SUBLANES + rows, D_CONV), F32),
                        pltpu.VMEM((rows, D_ATTN), BF16),
                        pltpu.VMEM((rows, D_MODEL), F32)],
        input_output_aliases={0: 0} if in_place else {},
        compiler_params=_params("arbitrary"),
        name=name,
    )(x, win, wout, convw, sink, bias_p, fmask, kinit, vinit, uinit, ln_g, ln_b, *relay)


def _heads_to_rows(q, n_seq):
    lane = lax.broadcasted_iota(jnp.int32, (1, LANES), 1)
    slabs = []
    for h in range(N_HEADS):
        pair = q[:, (h // 2) * LANES:(h // 2 + 1) * LANES]
        kv = h // GQA_GROUP
        data = pair if h % 2 == kv else pltpu.roll(pair, HEAD_DIM, 1)
        on_kv = (lane >= HEAD_DIM) if kv else (lane < HEAD_DIM)
        slabs.append(jnp.where(on_kv, data, 0.0))
    rows = jnp.concatenate(slabs, axis=1).reshape(q.shape[0], N_HEADS, LANES)
    return rows.reshape(n_seq, q.shape[0] // n_seq * N_HEADS, LANES)


def _rows_to_heads(o):
    n = o.shape[0] * o.shape[1] // N_HEADS
    wide = o.reshape(n, N_HEADS, LANES).reshape(n, N_HEADS * LANES)
    lane = lax.broadcasted_iota(jnp.int32, (1, LANES), 1)
    pairs = []
    for p in range(N_HEADS // 2):
        halves = []
        for e in range(2):
            h = 2 * p + e
            slab = wide[:, h * LANES:(h + 1) * LANES]
            halves.append(slab if h // GQA_GROUP == e else pltpu.roll(slab, HEAD_DIM, 1))
        pairs.append(jnp.where(lane < HEAD_DIM, halves[0], halves[1]))
    return jnp.concatenate(pairs, axis=1)


def _sample_mixer_kernel(x_ref, win_ref, wout_ref, convw_ref, state_ref, bias_ref, sink_ref,
                         g_ref, b_ref, ck_ref, cv_ref,
                         xo_ref, u_ref, ok_ref, ov_ref,
                         q_scr, kn_scr, vn_scr, o_scr, mix_scr, *, n_new, n_seq, chunk, alpha):
    i = pl.program_id(0)
    n_chunks = n_seq // chunk
    c0 = D_ATTN + 2 * D_KV

    @pl.when(i == 0)
    def _():
        z = jnp.dot(x_ref[...].astype(BF16), win_ref[...], preferred_element_type=F32)
        q_scr[...] = _heads_to_rows(z[:, :D_ATTN] * Q_SCALE, n_seq).astype(BF16)
        kn_scr[...] = z[:, D_ATTN:D_ATTN + D_KV].reshape(n_seq, n_new, D_KV)
        vn_scr[...] = z[:, D_ATTN + D_KV:c0].reshape(n_seq, n_new, D_KV)
        u = z[:, c0 + D_CONV:c0 + 2 * D_CONV] * z[:, c0 + 2 * D_CONV:c0 + 3 * D_CONV]
        u_ref[...] = u
        n = u.shape[0]
        tok = lax.broadcasted_iota(jnp.int32, u.shape, 0) % n_new
        st = state_ref[...]
        um2 = jnp.where(tok < 2, st, pltpu.roll(u, 2, 0))
        um1 = jnp.where(tok < 1, pltpu.roll(st, n - 1, 0), pltpu.roll(u, 1, 0))
        conv = convw_ref[0:1, :] * um2 + convw_ref[1:2, :] * um1 + convw_ref[2:3, :] * u
        mix_scr[...] = (z[:, c0:c0 + D_CONV] * conv).astype(BF16)

    @pl.when((i >= 1) & (i <= n_chunks))
    def _():
        s0 = pl.multiple_of((i - 1) * chunk, chunk)
        qb = q_scr[pl.ds(s0, chunk)]
        kn = kn_scr[pl.ds(s0, chunk)]
        vn = vn_scr[pl.ds(s0, chunk)]
        ck = ck_ref[...]
        cv = cv_ref[...]
        bias = bias_ref[...]
        sink = sink_ref[...]
        s_c = jnp.einsum("bqd,bkd->bqk", qb, ck.astype(BF16),
                         preferred_element_type=F32) + bias[None, :, :WINDOW]
        qf = qb.astype(F32)
        knf = kn.astype(BF16).astype(F32)
        vnf = vn.astype(BF16).astype(F32)
        s_n = [jnp.sum(qf * knf[:, t:t + 1, :], axis=-1, keepdims=True)
               + bias[None, :, WINDOW + t:WINDOW + t + 1] for t in range(n_new)]
        m = jnp.maximum(jnp.max(s_c, axis=-1, keepdims=True), sink[None])
        for s in s_n:
            m = jnp.maximum(m, s)
        p_c = jnp.exp(s_c - m)
        den = jnp.sum(p_c, axis=-1, keepdims=True) + jnp.exp(sink[None] - m)
        o = jnp.einsum("bqk,bkd->bqd", p_c.astype(BF16), cv.astype(BF16),
                       preferred_element_type=F32)
        for t, s in enumerate(s_n):
            p = jnp.exp(s - m)
            den = den + p
            o = o + p.astype(BF16).astype(F32) * vnf[:, t:t + 1, :]
        o_scr[pl.ds(s0, chunk)] = o / den
        keep = SUBLANES - n_new
        ok_ref[:, 0:keep, :] = ck_ref[:, WINDOW - SUBLANES:WINDOW - n_new, :]
        ok_ref[:, keep:SUBLANES, :] = kn
        ov_ref[:, 0:keep, :] = cv_ref[:, WINDOW - SUBLANES:WINDOW - n_new, :]
        ov_ref[:, keep:SUBLANES, :] = vn

    @pl.when(i == n_chunks + 1)
    def _():
        a = _rows_to_heads(o_scr[...]).astype(BF16)
        y = (jnp.dot(a, wout_ref[0:D_ATTN, :], preferred_element_type=F32)
             + jnp.dot(mix_scr[...], wout_ref[D_ATTN:, :], preferred_element_type=F32))
        xo_ref[...] = _layer_norm(alpha * x_ref[...] + y, g_ref[...], b_ref[...])


def _sample_mixer(xs, n_seq, n_new, win, wout, convw, state_rows, bias_s, sink_rows, ln_g, ln_b,
                  k_buf, v_buf, layer, alpha):
    assert n_seq % SAMPLE_SEQS == 0 and n_new <= SUBLANES
    n = n_seq * n_new
    qr = n_new * N_HEADS
    n_chunks = n_seq // SAMPLE_SEQS
    x_spec = pl.BlockSpec((n, D_MODEL), lambda i: (0, 0))
    at_chunk = lambda i: jnp.clip(i - 1, 0, n_chunks - 1)
    lay_spec = pl.BlockSpec((None, SAMPLE_SEQS, WINDOW, D_KV),
                            lambda i: (layer, at_chunk(i), 0, 0))
    tail_spec = pl.BlockSpec((None, SAMPLE_SEQS, SUBLANES, D_KV),
                             lambda i: (layer, at_chunk(i), WINDOW // SUBLANES - 1, 0))
    kern = functools.partial(_sample_mixer_kernel, n_new=n_new, n_seq=n_seq, chunk=SAMPLE_SEQS,
                             alpha=alpha)
    return pl.pallas_call(
        kern,
        out_shape=(jax.ShapeDtypeStruct(xs.shape, F32),
                   jax.ShapeDtypeStruct((n, D_CONV), F32),
                   jax.ShapeDtypeStruct(k_buf.shape, F32), jax.ShapeDtypeStruct(v_buf.shape, F32)),
        grid=(n_chunks + 2,),
        in_specs=[x_spec,
                  _const_spec((D_MODEL, D_IN)),
                  _const_spec((D_MODEL, D_MODEL)),
                  _layer_spec((CONV_WIDTH, D_CONV), layer),
                  _layer_spec((n, D_CONV), layer),
                  _layer_spec(bias_s.shape), _layer_spec(sink_rows.shape[1:], layer),
                  _layer_spec((1, D_MODEL), layer, 1), _layer_spec((1, D_MODEL), layer, 1),
                  lay_spec, lay_spec],
        out_specs=(x_spec, pl.BlockSpec((n, D_CONV), lambda i: (0, 0)), tail_spec, tail_spec),
        scratch_shapes=[pltpu.VMEM((n_seq, qr, LANES), BF16),
                        pltpu.VMEM((n_seq, n_new, D_KV), F32),
                        pltpu.VMEM((n_seq, n_new, D_KV), F32),
                        pltpu.VMEM((n_seq, qr, LANES), F32),
                        pltpu.VMEM((n, D_CONV), BF16)],
        input_output_aliases={0: 0, 9: 2, 10: 3},
        compiler_params=_params("arbitrary"),
        name="sample_mixer",
    )(xs, win, wout, convw, state_rows, bias_s, sink_rows, ln_g, ln_b, k_buf, v_buf)


def kernel(x_prompt, x_sample, cache_k, cache_v, state_conv, meta_tokens, rel_bias, w_in, conv_w,
           attn_sink, w_out, ffn_w_gate, ffn_w_up, ffn_w_down, ln_g, ln_b):
    depth = w_in.shape[0]
    alpha = float((2 * depth) ** 0.25)
    n_prompt, seq, d_model = x_prompt.shape
    n_sample, n_new, _ = x_sample.shape
    assert d_model == D_MODEL and seq % MIX_ROWS == 0
    assert n_new >= CONV_WIDTH - 1 and cache_k.shape[2] == WINDOW
    rows_p = n_prompt * seq
    rows_s = n_sample * n_new
    assert rows_s % BLOCK == 0

    xp = x_prompt.reshape(rows_p, D_MODEL)
    xs = jnp.concatenate([x_sample.reshape(rows_s, D_MODEL), meta_tokens.astype(F32)], axis=0)

    bias_p, bias_s = _bias_tables(rel_bias, n_new)
    col = jnp.arange(2 * BLOCK)[None, :]
    fmask_prompt = jnp.where(col < META_PAD, NEG_INF, 0.0).astype(F32)
    fmask_meta = jnp.where(col < BLOCK + META_PAD, NEG_INF, 0.0).astype(F32)
    zeros_kv = jnp.zeros((1, BLOCK, D_KV), F32)
    zeros_u = jnp.zeros((1, SUBLANES, D_CONV), F32)

    ffn_f32 = (ffn_w_gate, ffn_w_up, ffn_w_down)
    w_ffn = tuple(a[0, 0].astype(BF16) for a in ffn_f32)

    def ffn(xp, xs, w_cur, layer, which):
        nxt = (layer, 1) if which == 0 else (layer + 1, 0)
        to_cast = [(a, nxt) for a in ffn_f32] if nxt[0] < depth else []
        if which == 0:
            to_cast += [(w_in, (layer,)), (w_out, (layer,))]
        xp, xs, *cast = _ffn(xp, xs, w_cur, to_cast, ln_g4, ln_b4, layer, 2 * which, alpha)
        return xp, xs, tuple(cast[:3]), tuple(cast[3:])

    ln_g4 = ln_g.reshape(depth, 3, 1, D_MODEL)
    ln_b4 = ln_b.reshape(depth, 3, 1, D_MODEL)
    relay = tuple(jnp.transpose(c, (0, 1, 3, 4, 2)).reshape(-1, N_KV_HEADS, HEAD_DIM, WINDOW)
                  for c in (cache_k, cache_v))
    state_rows = jnp.pad(state_conv, ((0, 0), (0, 0), (0, n_new - (CONV_WIDTH - 1)), (0, 0))
                         ).reshape(depth, rows_s, D_CONV)
    sink_rows = jnp.tile(attn_sink, (1, n_new)).reshape(depth, n_new * N_HEADS, 1)

    kp, vp, cp, cs = [], [], [], []
    for l in range(depth):
        xp, xs, w_ffn, (win, wout) = ffn(xp, xs, w_ffn, l, 0)

        xs, k_m, v_m, u_m = _mixer(
            xs, rows_s, 1, N_META, BLOCK, True, win, wout, conv_w, attn_sink, bias_p,
            fmask_meta, zeros_kv, zeros_kv, zeros_u, ln_g4, ln_b4, l, alpha, "mixer_meta")
        xp, k_p, v_p, u_p, *relaid = _mixer(
            xp, 0, n_prompt, seq, MIX_ROWS, False, win, wout, conv_w, attn_sink, bias_p,
            fmask_prompt, k_m, v_m, u_m, ln_g4, ln_b4, l, alpha, "mixer_prompt",
            relay if l == 0 else (), n_new)
        if relaid:
            k_buf, v_buf = (a.reshape(depth, n_sample, WINDOW, D_KV) for a in relaid)

        xs, u_s, k_buf, v_buf = _sample_mixer(
            xs, n_sample, n_new, win, wout, conv_w, state_rows, bias_s, sink_rows, ln_g4, ln_b4,
            k_buf, v_buf, l, alpha)

        xp, xs, w_ffn, _ = ffn(xp, xs, w_ffn, l, 1)

        kp.append(k_p)
        vp.append(v_p)
        cp.append(u_p[:, SUBLANES - (CONV_WIDTH - 1):, :])
        cs.append(u_s.reshape(n_sample, n_new, D_CONV)[:, n_new - (CONV_WIDTH - 1):, :])

    kv_shape = (depth, -1, WINDOW, N_KV_HEADS, HEAD_DIM)
    return (xp.reshape(n_prompt, seq, D_MODEL),
            xs[:rows_s].reshape(n_sample, n_new, D_MODEL),
            jnp.stack(kp).reshape(kv_shape), jnp.stack(vp).reshape(kv_shape), jnp.stack(cp),
            k_buf.reshape(kv_shape), v_buf.reshape(kv_shape), jnp.stack(cs))
```

```python
import functools
import math

import jax
import jax.numpy as jnp
from jax import lax
from jax.experimental import pallas as pl
from jax.experimental.pallas import tpu as pltpu

F32 = jnp.float32
BF16 = jnp.bfloat16

D_MODEL = 1024
N_HEADS = 8
N_KV_HEADS = 2
HEAD_DIM = 64
GQA_GROUP = N_HEADS // N_KV_HEADS
D_ATTN = N_HEADS * HEAD_DIM
D_CONV = D_MODEL - D_ATTN
D_KV = N_KV_HEADS * HEAD_DIM
D_IN = D_ATTN + 2 * D_KV + 3 * D_CONV
D_FF = 2816
CONV_WIDTH = 3
WINDOW = 128
BLOCK = 128
N_META = 16
N_BUCKETS = 32
MAX_DISTANCE = 128
LN_EPS = 1e-5
Q_SCALE = HEAD_DIM ** -0.5

V7X_VMEM_LIMIT_BYTES = 60 * 1024 * 1024
V7X_MXU_COLUMNS = 256
SUBLANES = 8
LANES = 128

FFN_ROWS = 1024
FFN_NORM_PARTS = 8
FFN_CHUNK = V7X_MXU_COLUMNS
MIX_ROWS = 512
MIX_NORM_PARTS = 4
SAMPLE_SEQS = 32
META_PAD = BLOCK - N_META

NEG_INF = float("-inf")


def _const_spec(shape):
    nd = len(shape)
    return pl.BlockSpec(shape, lambda *_: (0,) * nd, pipeline_mode=pl.Buffered(1))


def _layer_spec(shape, *lead):
    block = (None,) * len(lead) + tuple(shape)
    idx = tuple(lead) + (0,) * len(shape)
    return pl.BlockSpec(block, lambda *_: idx, pipeline_mode=pl.Buffered(1))


def _params(*sem):
    return pltpu.CompilerParams(dimension_semantics=sem,
                                vmem_limit_bytes=V7X_VMEM_LIMIT_BYTES)


def _layer_norm(r, g, b):
    mu = jnp.mean(r, axis=-1, keepdims=True)
    rc = r - mu
    var = jnp.mean(rc * rc, axis=-1, keepdims=True)
    return rc * lax.rsqrt(var + LN_EPS) * g + b


def _t5_bucket(d):
    d = jnp.maximum(d, 0)
    max_exact = N_BUCKETS // 2
    df = jnp.maximum(d, 1).astype(F32)
    large = max_exact + (jnp.log(df / max_exact) / math.log(MAX_DISTANCE / max_exact)
                         * (N_BUCKETS - max_exact)).astype(jnp.int32)
    large = jnp.minimum(large, N_BUCKETS - 1)
    return jnp.where(d < max_exact, d, large)


def _bias_kernel(tab_ref, bp_ref, bs_ref, *, n_new):
    qi = lax.broadcasted_iota(jnp.int32, (BLOCK, 2 * BLOCK), 0)
    sj = lax.broadcasted_iota(jnp.int32, (BLOCK, 2 * BLOCK), 1)
    d = qi + BLOCK - sj
    valid = (d >= 0) & (d <= WINDOW)
    bk = _t5_bucket(d)
    for h in range(N_HEADS):
        acc = jnp.zeros(d.shape, F32)
        for b in range(N_BUCKETS):
            acc = jnp.where(bk == b, tab_ref[b, h], acc)
        bp_ref[h] = jnp.where(valid, acc, NEG_INF)
    rows, cols = bs_ref.shape
    r = lax.broadcasted_iota(jnp.int32, (rows, cols), 0)
    s = lax.broadcasted_iota(jnp.int32, (rows, cols), 1)
    s_pos = jnp.where(s < WINDOW, (s + n_new) % WINDOW, s)
    ds = r // N_HEADS + WINDOW - s_pos
    hs = r % N_HEADS
    valid_s = (ds >= 0) & (ds <= WINDOW) & (s < WINDOW + n_new)
    bks = _t5_bucket(ds)
    acc = jnp.zeros((rows, cols), F32)
    for h in range(N_HEADS):
        for b in range(N_BUCKETS):
            acc = jnp.where((bks == b) & (hs == h), tab_ref[b, h], acc)
    bs_ref[...] = jnp.where(valid_s, acc, NEG_INF)


def _bias_tables(rel_bias, n_new):
    rows = n_new * N_HEADS
    return pl.pallas_call(
        functools.partial(_bias_kernel, n_new=n_new),
        out_shape=(jax.ShapeDtypeStruct((N_HEADS, BLOCK, 2 * BLOCK), F32),
                   jax.ShapeDtypeStruct((rows, WINDOW + SUBLANES), F32)),
        in_specs=[pl.BlockSpec(memory_space=pltpu.SMEM)],
        name="bias_tables",
    )(rel_bias)


def _zero_after(v):
    u = lax.bitcast_convert_type(v, jnp.uint32)
    z = lax.shift_right_logical(lax.shift_right_logical(u, jnp.uint32(16)), jnp.uint32(16))
    return lax.bitcast_convert_type(z, F32)


def _fold_rows(a):
    a = a.reshape(a.shape[0] // 16, 16, a.shape[1]).sum(axis=0)
    return a.reshape(16, a.shape[1] // LANES, LANES).sum(axis=1)


def _ffn_pre_norm(x, wg_ref, wu_ref, wd_ref, act_ref, alpha, side_work):
    rows = x.shape[0]
    xb = x.astype(BF16)
    for ci, c in enumerate(range(0, D_FF, FFN_CHUNK)):
        gate = jnp.dot(xb, wg_ref[:, c:c + FFN_CHUNK], preferred_element_type=F32)
        up = jnp.dot(xb, wu_ref[:, c:c + FFN_CHUNK], preferred_element_type=F32)
        act = jax.nn.silu(gate) * up
        act_ref[0:rows, c:c + FFN_CHUNK] = act.astype(BF16)
        z = side_work(ci)
        if z is not None:
            act_ref[0:16, c:c + LANES] = (act[0:16, 0:LANES] + z).astype(BF16)
    y = jnp.dot(act_ref[0:rows, :], wd_ref[...], preferred_element_type=F32)
    return alpha * x + 0.5 * y


def _ffn_kernel(xp_ref, xs_ref, wg_ref, wu_ref, wd_ref, g_ref, b_ref, *rest, alpha, n_big,
                n_cast):
    f32_refs, rest = rest[:n_cast], rest[n_cast:]
    op_ref, os_ref = rest[:2]
    bf16_refs, (act_ref, r_ref) = rest[2:2 + n_cast], rest[2 + n_cast:]
    i = pl.program_id(0)
    w = (wg_ref, wu_ref, wd_ref)
    g, b = g_ref[...], b_ref[...]
    part = FFN_ROWS // FFN_NORM_PARTS

    def norm_previous_tile(ci):
        if ci >= FFN_NORM_PARTS:
            return None
        rs = slice(ci * part, (ci + 1) * part)
        out = _layer_norm(r_ref[rs, :], g, b)
        op_ref[rs, :] = out
        return _zero_after(_fold_rows(out))

    def norm_and_convert(ci):
        if ci == FFN_NORM_PARTS:
            for src, dst in zip(f32_refs, bf16_refs):
                dst[...] = src[...].astype(BF16)
        return norm_previous_tile(ci)

    @pl.when(i == 0)
    def _():
        r_ref[...] = jnp.zeros(r_ref.shape, F32)

    @pl.when(i < n_big)
    def _():
        r_ref[...] = _ffn_pre_norm(xp_ref[...], *w, act_ref, alpha, norm_and_convert)

    @pl.when(i == n_big)
    def _():
        os_ref[...] = _ffn_pre_norm(xs_ref[...], *w, act_ref, alpha, norm_previous_tile)

    @pl.when(i == n_big + 1)
    def _():
        os_ref[...] = _layer_norm(os_ref[...], g, b)


def _ffn(xp, xs, w_cur, to_cast, ln_g, ln_b, layer, ln_idx, alpha):
    assert xp.shape[0] % FFN_ROWS == 0 and D_FF % FFN_CHUNK == 0
    assert FFN_NORM_PARTS < D_FF // FFN_CHUNK and FFN_ROWS % (16 * FFN_NORM_PARTS) == 0
    n_big = xp.shape[0] // FFN_ROWS
    rows_s = xs.shape[0]
    assert rows_s <= FFN_ROWS
    last = lambda i: jnp.minimum(i, n_big - 1)
    in_spec = pl.BlockSpec((FFN_ROWS, D_MODEL), lambda i: (last(i), 0))
    out_spec = pl.BlockSpec((FFN_ROWS, D_MODEL), lambda i: (jnp.clip(i - 1, 0, n_big - 1), 0))
    small_in = pl.BlockSpec((rows_s, D_MODEL), lambda i: (0, 0), pipeline_mode=pl.Buffered(1))
    small_out = pl.BlockSpec((rows_s, D_MODEL), lambda i: (0, 0), pipeline_mode=pl.Buffered(1))
    in_specs = [in_spec, small_in] + [_const_spec(a.shape) for a in w_cur] + [
        _layer_spec((1, D_MODEL), layer, ln_idx), _layer_spec((1, D_MODEL), layer, ln_idx)]
    out_specs = [out_spec, small_out]
    out_shape = [jax.ShapeDtypeStruct(xp.shape, F32), jax.ShapeDtypeStruct(xs.shape, F32)]
    args = [xp, xs, *w_cur, ln_g, ln_b]
    for a, lead in to_cast:
        r, c = a.shape[-2] // n_big, a.shape[-1]
        assert a.shape[-2] % (16 * n_big) == 0 and len(lead) == a.ndim - 2
        in_specs.append(pl.BlockSpec((None,) * len(lead) + (r, c),
                                     lambda i, lead=lead: lead + (last(i), 0)))
        out_specs.append(pl.BlockSpec((r, c), lambda i: (last(i), 0)))
        out_shape.append(jax.ShapeDtypeStruct(a.shape[-2:], BF16))
        args.append(a)
    return pl.pallas_call(
        functools.partial(_ffn_kernel, alpha=alpha, n_big=n_big, n_cast=len(to_cast)),
        out_shape=out_shape,
        grid=(n_big + 2,),
        in_specs=in_specs,
        out_specs=out_specs,
        scratch_shapes=[pltpu.VMEM((FFN_ROWS, D_FF), BF16), pltpu.VMEM((FFN_ROWS, D_MODEL), F32)],
        compiler_params=_params("arbitrary"),
        name="ffn_ln",
    )(*args)


def _mixer_kernel(x_ref, win_ref, wout_ref, convw_ref, sink_ref, bias_ref, fmask_ref,
                  kinit_ref, vinit_ref, uinit_ref, g_ref, b_ref, *rest,
                  rows, steps, n_tiles, alpha, layer, n_relaid, rotate):
    src_refs, rest = rest[:n_relaid], rest[n_relaid:]
    (xo_ref, klast_ref, vlast_ref, ulast_ref), rest = rest[:4], rest[4:]
    dst_refs, (kd_scr, vd_scr, uscr, ascr, r_scr) = rest[:n_relaid], rest[n_relaid:]
    step = pl.program_id(0)
    g, b = g_ref[...], b_ref[...]
    io_rows = x_ref.shape[0]

    @pl.when(step == 0)
    def _():
        r_scr[...] = jnp.zeros(r_scr.shape, F32)

    @pl.when(step == n_tiles)
    def _():
        xo_ref[...] = _layer_norm(r_scr[rows - io_rows:, :], g, b)

    @pl.when(step < n_tiles)
    def _():
        _mixer_tile(x_ref, win_ref, wout_ref, convw_ref, sink_ref, bias_ref, fmask_ref,
                    kinit_ref, vinit_ref, uinit_ref, g, b, xo_ref, klast_ref, vlast_ref,
                    ulast_ref, kd_scr, vd_scr, uscr, ascr, r_scr, t=step % steps, rows=rows,
                    alpha=alpha, layer=layer)
        for src, dst in zip(src_refs, dst_refs):
            for i in range(src.shape[0]):
                buf = src[i].reshape(D_KV, WINDOW).T
                dst[i, 0:WINDOW - rotate, :] = buf[rotate:, :]
                dst[i, WINDOW - rotate:, :] = buf[0:rotate, :]


def _mixer_tile(x_ref, win_ref, wout_ref, convw_ref, sink_ref, bias_ref, fmask_ref,
                kinit_ref, vinit_ref, uinit_ref, g, b, xo_ref, klast_ref, vlast_ref, ulast_ref,
                kd_scr, vd_scr, uscr, ascr, r_scr, *, t, rows, alpha, layer):
    lane = lax.broadcasted_iota(jnp.int32, (1, LANES), 1)
    low = lane < HEAD_DIM

    def dup_heads(a):
        sw = pltpu.roll(a, HEAD_DIM, 1)
        return (jnp.where(low, a, sw).astype(BF16), jnp.where(low, sw, a).astype(BF16))

    @pl.when(t == 0)
    def _():
        for kv, (kd, vd) in enumerate(zip(dup_heads(kinit_ref[0]), dup_heads(vinit_ref[0]))):
            kd_scr[kv, 0:BLOCK, :] = kd
            vd_scr[kv, 0:BLOCK, :] = vd
        uscr[0:SUBLANES, :] = uinit_ref[0]

    x = x_ref[...]
    whole_tile = x.shape[0] == rows
    if not whole_tile:
        x = jnp.concatenate([jnp.zeros((rows - x.shape[0], D_MODEL), F32), x], axis=0)
    xb = x.astype(BF16)
    c0 = D_ATTN + 2 * D_KV
    zq = jnp.dot(xb, win_ref[:, 0:c0], preferred_element_type=F32)
    zc = jnp.dot(xb, win_ref[:, c0:], preferred_element_type=F32)
    k = zq[:, D_ATTN:D_ATTN + D_KV]
    v = zq[:, D_ATTN + D_KV:c0]
    if whole_tile:
        part = rows // MIX_NORM_PARTS
        edges = []
        for ci in range(MIX_NORM_PARTS):
            rs = slice(ci * part, (ci + 1) * part)
            out = _layer_norm(r_scr[rs, :], g, b)
            xo_ref[rs, :] = out
            edges.append(_zero_after(_fold_rows(out)))
        k = jnp.concatenate([k[0:16] + sum(edges[0::2]), k[16:]], axis=0)
        v = jnp.concatenate([v[0:16] + sum(edges[1::2]), v[16:]], axis=0)
    u = zc[:, D_CONV:2 * D_CONV] * zc[:, 2 * D_CONV:3 * D_CONV]
    klast_ref[0] = k[rows - BLOCK:, :]
    vlast_ref[0] = v[rows - BLOCK:, :]
    ulast_ref[0] = u[rows - SUBLANES:, :]
    for kv, (kd, vd) in enumerate(zip(dup_heads(k), dup_heads(v))):
        kd_scr[kv, BLOCK:BLOCK + rows, :] = kd
        vd_scr[kv, BLOCK:BLOCK + rows, :] = vd
    uscr[SUBLANES:SUBLANES + rows, :] = u

    qs = zq[:, :D_ATTN] * Q_SCALE
    lane_q = lax.broadcasted_iota(jnp.int32, (1, D_ATTN), 1) % LANES
    q_even = jnp.where(lane_q < HEAD_DIM, qs, 0.0).astype(BF16)
    q_odd = jnp.where(lane_q < HEAD_DIM, 0.0, qs).astype(BF16)

    first = jnp.where(t == 0, fmask_ref[...], 0.0)
    ones_cols = jnp.ones((2 * BLOCK, LANES), BF16)
    for j in range(rows // BLOCK):
        r0 = j * BLOCK
        for kv in range(N_KV_HEADS):
            kd = kd_scr[kv, r0:r0 + 2 * BLOCK, :]
            vdx = jnp.concatenate([vd_scr[kv, r0:r0 + 2 * BLOCK, :], ones_cols], axis=1)
            heads = range(kv * GQA_GROUP, (kv + 1) * GQA_GROUP)
            q4 = jnp.concatenate(
                [(q_odd if h % 2 else q_even)[r0:r0 + BLOCK, (h // 2) * LANES:(h // 2 + 1) * LANES]
                 for h in heads], axis=0)
            s4 = lax.dot_general(q4, kd, (((1,), (1,)), ((), ())), preferred_element_type=F32)
            ps, es = [], []
            for g, h in enumerate(heads):
                s = s4[g * BLOCK:(g + 1) * BLOCK] + bias_ref[h]
                if j == 0:
                    s = s + first
                sl, sr = s[:, :LANES], s[:, LANES:]
                sk = sink_ref[layer, h]
                m1 = jnp.max(jnp.maximum(sl, sr), axis=-1, keepdims=True)
                mb = jnp.broadcast_to(jnp.maximum(m1, sk), (BLOCK, LANES))
                ps.append(jnp.concatenate([jnp.exp(sl - mb), jnp.exp(sr - mb)],
                                          axis=1).astype(BF16))
                es.append(jnp.exp(sk - mb))
            ox = jnp.dot(jnp.concatenate(ps, axis=0), vdx, preferred_element_type=F32)
            outs = []
            for g in range(GQA_GROUP):
                og = ox[g * BLOCK:(g + 1) * BLOCK]
                outs.append(og[:, :LANES] / (og[:, LANES:] + es[g]))
            for i in range(GQA_GROUP // 2):
                slab = jnp.where(low, outs[2 * i], outs[2 * i + 1])
                col = (kv * GQA_GROUP // 2 + i) * LANES
                ascr[r0:r0 + BLOCK, col:col + LANES] = slab.astype(BF16)

    um2 = uscr[SUBLANES - 2:SUBLANES - 2 + rows, :]
    um1 = uscr[SUBLANES - 1:SUBLANES - 1 + rows, :]
    conv = convw_ref[0:1, :] * um2 + convw_ref[1:2, :] * um1 + convw_ref[2:3, :] * u
    mix = zc[:, 0:D_CONV] * conv
    y = (jnp.dot(ascr[...], wout_ref[0:D_ATTN, :], preferred_element_type=F32)
         + jnp.dot(mix.astype(BF16), wout_ref[D_ATTN:, :], preferred_element_type=F32))
    r_scr[...] = alpha * x + y

    for kv in range(N_KV_HEADS):
        kd_scr[kv, 0:BLOCK, :] = kd_scr[kv, rows:rows + BLOCK, :]
        vd_scr[kv, 0:BLOCK, :] = vd_scr[kv, rows:rows + BLOCK, :]
    uscr[0:SUBLANES, :] = uscr[rows:rows + SUBLANES, :]


def _mixer(x, row0, n_seq, seq_len, rows, in_place, win, wout, convw, sink, bias_p,
           fmask, kinit, vinit, uinit, ln_g, ln_b, layer, alpha, name, relay=(), rotate=0):
    io_rows = min(rows, seq_len)
    assert seq_len % io_rows == 0 and rows % BLOCK == 0 and row0 % io_rows == 0
    assert rows % (16 * MIX_NORM_PARTS) == 0 and io_rows % SUBLANES == 0
    assert io_rows == rows or n_seq == 1
    steps = seq_len // io_rows
    n_tiles = n_seq * steps
    blk0 = row0 // io_rows
    x_spec = pl.BlockSpec((io_rows, D_MODEL), lambda s: (blk0 + jnp.minimum(s, n_tiles - 1), 0))
    xo_spec = pl.BlockSpec((io_rows, D_MODEL), lambda s: (blk0 + jnp.maximum(s - 1, 0), 0))
    seq_spec = lambda shape: pl.BlockSpec(
        (1,) + shape, lambda s: (jnp.minimum(s // steps, n_seq - 1), 0, 0))
    kern = functools.partial(_mixer_kernel, rows=rows, steps=steps, n_tiles=n_tiles, alpha=alpha,
                             layer=layer, n_relaid=len(relay), rotate=rotate)
    relay_in, relay_out, relay_shape = [], [], []
    for a in relay:
        per_step = a.shape[0] // n_tiles
        assert a.shape[0] % n_tiles == 0 and a.shape[1] * a.shape[2] == D_KV
        at_step = lambda s: (jnp.minimum(s, n_tiles - 1), 0, 0)
        relay_in.append(pl.BlockSpec((per_step,) + a.shape[1:], lambda s: at_step(s) + (0,)))
        relay_out.append(pl.BlockSpec((per_step, WINDOW, D_KV), at_step))
        relay_shape.append(jax.ShapeDtypeStruct((a.shape[0], WINDOW, D_KV), F32))
    return pl.pallas_call(
        kern,
        out_shape=(jax.ShapeDtypeStruct(x.shape, F32),
                   jax.ShapeDtypeStruct((n_seq, BLOCK, D_KV), F32),
                   jax.ShapeDtypeStruct((n_seq, BLOCK, D_KV), F32),
                   jax.ShapeDtypeStruct((n_seq, SUBLANES, D_CONV), F32), *relay_shape),
        grid=(n_tiles + 1,),
        in_specs=[x_spec,
                  _const_spec((D_MODEL, D_IN)),
                  _const_spec((D_MODEL, D_MODEL)),
                  _layer_spec((CONV_WIDTH, D_CONV), layer),
                  pl.BlockSpec(memory_space=pltpu.SMEM),
                  _const_spec(bias_p.shape), _const_spec(fmask.shape),
                  _const_spec(kinit.shape), _const_spec(vinit.shape), _const_spec(uinit.shape),
                  _layer_spec((1, D_MODEL), layer, 1),
                  _layer_spec((1, D_MODEL), layer, 1), *relay_in],
        out_specs=(xo_spec, seq_spec((BLOCK, D_KV)), seq_spec((BLOCK, D_KV)),
                   seq_spec((SUBLANES, D_CONV)), *relay_out),
        scratch_shapes=[pltpu.VMEM((N_KV_HEADS, BLOCK + rows, D_KV), BF16),
                        pltpu.VMEM((N_KV_HEADS, BLOCK + rows, D_KV), BF16),
                        pltpu.VMEM((SUBLANES + rows, D_CONV), F32),
                        pltpu.VMEM((rows, D_ATTN), BF16),
                        pltpu.VMEM((rows, D_MODEL), F32)],
        input_output_aliases={0: 0} if in_place else {},
        compiler_params=_params("arbitrary"),
        name=name,
    )(x, win, wout, convw, sink, bias_p, fmask, kinit, vinit, uinit, ln_g, ln_b, *relay)


def _heads_to_rows(q, n_seq):
    lane = lax.broadcasted_iota(jnp.int32, (1, LANES), 1)
    slabs = []
    for h in range(N_HEADS):
        pair = q[:, (h // 2) * LANES:(h // 2 + 1) * LANES]
        kv = h // GQA_GROUP
        data = pair if h % 2 == kv else pltpu.roll(pair, HEAD_DIM, 1)
        on_kv = (lane >= HEAD_DIM) if kv else (lane < HEAD_DIM)
        slabs.append(jnp.where(on_kv, data, 0.0))
    rows = jnp.concatenate(slabs, axis=1).reshape(q.shape[0], N_HEADS, LANES)
    return rows.reshape(n_seq, q.shape[0] // n_seq * N_HEADS, LANES)


def _rows_to_heads(o):
    n = o.shape[0] * o.shape[1] // N_HEADS
    wide = o.reshape(n, N_HEADS, LANES).reshape(n, N_HEADS * LANES)
    lane = lax.broadcasted_iota(jnp.int32, (1, LANES), 1)
    pairs = []
    for p in range(N_HEADS // 2):
        halves = []
        for e in range(2):
            h = 2 * p + e
            slab = wide[:, h * LANES:(h + 1) * LANES]
            halves.append(slab if h // GQA_GROUP == e else pltpu.roll(slab, HEAD_DIM, 1))
        pairs.append(jnp.where(lane < HEAD_DIM, halves[0], halves[1]))
    return jnp.concatenate(pairs, axis=1)


def _sample_mixer_kernel(x_ref, win_ref, wout_ref, convw_ref, state_ref, bias_ref, sink_ref,
                         g_ref, b_ref, ck_ref, cv_ref,
                         xo_ref, u_ref, ok_ref, ov_ref,
                         q_scr, kn_scr, vn_scr, o_scr, mix_scr, *, n_new, n_seq, chunk, alpha):
    i = pl.program_id(0)
    n_chunks = n_seq // chunk
    c0 = D_ATTN + 2 * D_KV

    @pl.when(i == 0)
    def _():
        z = jnp.dot(x_ref[...].astype(BF16), win_ref[...], preferred_element_type=F32)
        q_scr[...] = _heads_to_rows(z[:, :D_ATTN] * Q_SCALE, n_seq).astype(BF16)
        kn_scr[:, 0:n_new, :] = z[:, D_ATTN:D_ATTN + D_KV].reshape(n_seq, n_new, D_KV)
        vn_scr[:, 0:n_new, :] = z[:, D_ATTN + D_KV:c0].reshape(n_seq, n_new, D_KV)
        kn_scr[:, n_new:, :] = jnp.zeros((n_seq, SUBLANES - n_new, D_KV), F32)
        vn_scr[:, n_new:, :] = jnp.zeros((n_seq, SUBLANES - n_new, D_KV), F32)
        u = z[:, c0 + D_CONV:c0 + 2 * D_CONV] * z[:, c0 + 2 * D_CONV:c0 + 3 * D_CONV]
        u_ref[...] = u
        n = u.shape[0]
        tok = lax.broadcasted_iota(jnp.int32, u.shape, 0) % n_new
        st = state_ref[...]
        um2 = jnp.where(tok < 2, st, pltpu.roll(u, 2, 0))
        um1 = jnp.where(tok < 1, pltpu.roll(st, n - 1, 0), pltpu.roll(u, 1, 0))
        conv = convw_ref[0:1, :] * um2 + convw_ref[1:2, :] * um1 + convw_ref[2:3, :] * u
        mix_scr[...] = (z[:, c0:c0 + D_CONV] * conv).astype(BF16)

    @pl.when((i >= 1) & (i <= n_chunks))
    def _():
        s0 = pl.multiple_of((i - 1) * chunk, chunk)
        qb = q_scr[pl.ds(s0, chunk)]
        kn = kn_scr[pl.ds(s0, chunk)]
        vn = vn_scr[pl.ds(s0, chunk)]
        ck = ck_ref[...]
        cv = cv_ref[...]
        bias = bias_ref[...]
        sink = sink_ref[...]
        s_c = jnp.einsum("bqd,bkd->bqk", qb, ck.astype(BF16),
                         preferred_element_type=F32) + bias[None, :, :WINDOW]
        s_n = jnp.einsum("bqd,bkd->bqk", qb, kn.astype(BF16),
                         preferred_element_type=F32) + bias[None, :, WINDOW:]
        m = jnp.maximum(jnp.maximum(jnp.max(s_c, axis=-1, keepdims=True),
                                    jnp.max(s_n, axis=-1, keepdims=True)), sink[None])
        p_c = jnp.exp(s_c - m)
        p_n = jnp.exp(s_n - m)
        den = (jnp.sum(p_c, axis=-1, keepdims=True) + jnp.sum(p_n, axis=-1, keepdims=True)
               + jnp.exp(sink[None] - m))
        o = (jnp.einsum("bqk,bkd->bqd", p_c.astype(BF16), cv.astype(BF16),
                        preferred_element_type=F32)
             + jnp.einsum("bqk,bkd->bqd", p_n.astype(BF16), vn.astype(BF16),
                          preferred_element_type=F32))
        o_scr[pl.ds(s0, chunk)] = o / den
        keep = SUBLANES - n_new
        ok_ref[:, 0:keep, :] = ck_ref[:, WINDOW - SUBLANES:WINDOW - n_new, :]
        ok_ref[:, keep:SUBLANES, :] = kn[:, 0:n_new, :]
        ov_ref[:, 0:keep, :] = cv_ref[:, WINDOW - SUBLANES:WINDOW - n_new, :]
        ov_ref[:, keep:SUBLANES, :] = vn[:, 0:n_new, :]

    @pl.when(i == n_chunks + 1)
    def _():
        a = _rows_to_heads(o_scr[...]).astype(BF16)
        y = (jnp.dot(a, wout_ref[0:D_ATTN, :], preferred_element_type=F32)
             + jnp.dot(mix_scr[...], wout_ref[D_ATTN:, :], preferred_element_type=F32))
        xo_ref[...] = _layer_norm(alpha * x_ref[...] + y, g_ref[...], b_ref[...])


def _sample_mixer(xs, n_seq, n_new, win, wout, convw, state_rows, bias_s, sink_rows, ln_g, ln_b,
                  k_buf, v_buf, layer, alpha):
    assert n_seq % SAMPLE_SEQS == 0 and n_new <= SUBLANES
    n = n_seq * n_new
    qr = n_new * N_HEADS
    n_chunks = n_seq // SAMPLE_SEQS
    x_spec = pl.BlockSpec((n, D_MODEL), lambda i: (0, 0))
    at_chunk = lambda i: jnp.clip(i - 1, 0, n_chunks - 1)
    lay_spec = pl.BlockSpec((None, SAMPLE_SEQS, WINDOW, D_KV),
                            lambda i: (layer, at_chunk(i), 0, 0))
    tail_spec = pl.BlockSpec((None, SAMPLE_SEQS, SUBLANES, D_KV),
                             lambda i: (layer, at_chunk(i), WINDOW // SUBLANES - 1, 0))
    kern = functools.partial(_sample_mixer_kernel, n_new=n_new, n_seq=n_seq, chunk=SAMPLE_SEQS,
                             alpha=alpha)
    return pl.pallas_call(
        kern,
        out_shape=(jax.ShapeDtypeStruct(xs.shape, F32),
                   jax.ShapeDtypeStruct((n, D_CONV), F32),
                   jax.ShapeDtypeStruct(k_buf.shape, F32), jax.ShapeDtypeStruct(v_buf.shape, F32)),
        grid=(n_chunks + 2,),
        in_specs=[x_spec,
                  _const_spec((D_MODEL, D_IN)),
                  _const_spec((D_MODEL, D_MODEL)),
                  _layer_spec((CONV_WIDTH, D_CONV), layer),
                  _layer_spec((n, D_CONV), layer),
                  _layer_spec(bias_s.shape), _layer_spec(sink_rows.shape[1:], layer),
                  _layer_spec((1, D_MODEL), layer, 1), _layer_spec((1, D_MODEL), layer, 1),
                  lay_spec, lay_spec],
        out_specs=(x_spec, pl.BlockSpec((n, D_CONV), lambda i: (0, 0)), tail_spec, tail_spec),
        scratch_shapes=[pltpu.VMEM((n_seq, qr, LANES), BF16),
                        pltpu.VMEM((n_seq, SUBLANES, D_KV), F32),
                        pltpu.VMEM((n_seq, SUBLANES, D_KV), F32),
                        pltpu.VMEM((n_seq, qr, LANES), F32),
                        pltpu.VMEM((n, D_CONV), BF16)],
        input_output_aliases={0: 0, 9: 2, 10: 3},
        compiler_params=_params("arbitrary"),
        name="sample_mixer",
    )(xs, win, wout, convw, state_rows, bias_s, sink_rows, ln_g, ln_b, k_buf, v_buf)


def kernel(x_prompt, x_sample, cache_k, cache_v, state_conv, meta_tokens, rel_bias, w_in, conv_w,
           attn_sink, w_out, ffn_w_gate, ffn_w_up, ffn_w_down, ln_g, ln_b):
    depth = w_in.shape[0]
    alpha = float((2 * depth) ** 0.25)
    n_prompt, seq, d_model = x_prompt.shape
    n_sample, n_new, _ = x_sample.shape
    assert d_model == D_MODEL and seq % MIX_ROWS == 0
    assert n_new >= CONV_WIDTH - 1 and cache_k.shape[2] == WINDOW
    rows_p = n_prompt * seq
    rows_s = n_sample * n_new
    assert rows_s % BLOCK == 0

    xp = x_prompt.reshape(rows_p, D_MODEL)
    xs = jnp.concatenate([x_sample.reshape(rows_s, D_MODEL), meta_tokens.astype(F32)], axis=0)

    bias_p, bias_s = _bias_tables(rel_bias, n_new)
    col = jnp.arange(2 * BLOCK)[None, :]
    fmask_prompt = jnp.where(col < META_PAD, NEG_INF, 0.0).astype(F32)
    fmask_meta = jnp.where(col < BLOCK + META_PAD, NEG_INF, 0.0).astype(F32)
    zeros_kv = jnp.zeros((1, BLOCK, D_KV), F32)
    zeros_u = jnp.zeros((1, SUBLANES, D_CONV), F32)

    ffn_f32 = (ffn_w_gate, ffn_w_up, ffn_w_down)
    w_ffn = tuple(a[0, 0].astype(BF16) for a in ffn_f32)

    def ffn(xp, xs, w_cur, layer, which):
        nxt = (layer, 1) if which == 0 else (layer + 1, 0)
        to_cast = [(a, nxt) for a in ffn_f32] if nxt[0] < depth else []
        if which == 0:
            to_cast += [(w_in, (layer,)), (w_out, (layer,))]
        xp, xs, *cast = _ffn(xp, xs, w_cur, to_cast, ln_g4, ln_b4, layer, 2 * which, alpha)
        return xp, xs, tuple(cast[:3]), tuple(cast[3:])

    ln_g4 = ln_g.reshape(depth, 3, 1, D_MODEL)
    ln_b4 = ln_b.reshape(depth, 3, 1, D_MODEL)
    relay = tuple(jnp.transpose(c, (0, 1, 3, 4, 2)).reshape(-1, N_KV_HEADS, HEAD_DIM, WINDOW)
                  for c in (cache_k, cache_v))
    state_rows = jnp.pad(state_conv, ((0, 0), (0, 0), (0, n_new - (CONV_WIDTH - 1)), (0, 0))
                         ).reshape(depth, rows_s, D_CONV)
    sink_rows = jnp.tile(attn_sink, (1, n_new)).reshape(depth, n_new * N_HEADS, 1)

    kp, vp, cp, cs = [], [], [], []
    for l in range(depth):
        xp, xs, w_ffn, (win, wout) = ffn(xp, xs, w_ffn, l, 0)

        xs, k_m, v_m, u_m = _mixer(
            xs, rows_s, 1, N_META, BLOCK, True, win, wout, conv_w, attn_sink, bias_p,
            fmask_meta, zeros_kv, zeros_kv, zeros_u, ln_g4, ln_b4, l, alpha, "mixer_meta")
        xp, k_p, v_p, u_p, *relaid = _mixer(
            xp, 0, n_prompt, seq, MIX_ROWS, False, win, wout, conv_w, attn_sink, bias_p,
            fmask_prompt, k_m, v_m, u_m, ln_g4, ln_b4, l, alpha, "mixer_prompt",
            relay if l == 0 else (), n_new)
        if relaid:
            k_buf, v_buf = (a.reshape(depth, n_sample, WINDOW, D_KV) for a in relaid)

        xs, u_s, k_buf, v_buf = _sample_mixer(
            xs, n_sample, n_new, win, wout, conv_w, state_rows, bias_s, sink_rows, ln_g4, ln_b4,
            k_buf, v_buf, l, alpha)

        xp, xs, w_ffn, _ = ffn(xp, xs, w_ffn, l, 1)

        kp.append(k_p)
        vp.append(v_p)
        cp.append(u_p[:, SUBLANES - (CONV_WIDTH - 1):, :])
        cs.append(u_s.reshape(n_sample, n_new, D_CONV)[:, n_new - (CONV_WIDTH - 1):, :])

    kv_shape = (depth, -1, WINDOW, N_KV_HEADS, HEAD_DIM)
    return (xp.reshape(n_prompt, seq, D_MODEL),
            xs[:rows_s].reshape(n_sample, n_new, D_MODEL),
            jnp.stack(kp).reshape(kv_shape), jnp.stack(vp).reshape(kv_shape), jnp.stack(cp),
            k_buf.reshape(kv_shape), v_buf.reshape(kv_shape), jnp.stack(cs))
```

```python
import functools
import math

import jax
import jax.numpy as jnp
from jax import lax
from jax.experimental import pallas as pl
from jax.experimental.pallas import tpu as pltpu

F32 = jnp.float32
BF16 = jnp.bfloat16

D_MODEL = 1024
N_HEADS = 8
N_KV_HEADS = 2
HEAD_DIM = 64
GQA_GROUP = N_HEADS // N_KV_HEADS
D_ATTN = N_HEADS * HEAD_DIM
D_CONV = D_MODEL - D_ATTN
D_KV = N_KV_HEADS * HEAD_DIM
D_IN = D_ATTN + 2 * D_KV + 3 * D_CONV
D_FF = 2816
CONV_WIDTH = 3
WINDOW = 128
BLOCK = 128
N_META = 16
N_BUCKETS = 32
MAX_DISTANCE = 128
LN_EPS = 1e-5
Q_SCALE = HEAD_DIM ** -0.5

V7X_VMEM_LIMIT_BYTES = 60 * 1024 * 1024
V7X_MXU_COLUMNS = 256
SUBLANES = 8
LANES = 128

FFN_ROWS = 1024
FFN_NORM_PARTS = 8
FFN_CHUNK = V7X_MXU_COLUMNS
MIX_ROWS = 512
MIX_NORM_PARTS = 2
SAMPLE_SEQS = 32
META_PAD = BLOCK - N_META

NEG_INF = float("-inf")


def _const_spec(shape):
    nd = len(shape)
    return pl.BlockSpec(shape, lambda *_: (0,) * nd, pipeline_mode=pl.Buffered(1))


def _layer_spec(shape, *lead):
    block = (None,) * len(lead) + tuple(shape)
    idx = tuple(lead) + (0,) * len(shape)
    return pl.BlockSpec(block, lambda *_: idx, pipeline_mode=pl.Buffered(1))


def _params(*sem):
    return pltpu.CompilerParams(dimension_semantics=sem,
                                vmem_limit_bytes=V7X_VMEM_LIMIT_BYTES)


def _layer_norm(r, g, b):
    mu = jnp.mean(r, axis=-1, keepdims=True)
    rc = r - mu
    var = jnp.mean(rc * rc, axis=-1, keepdims=True)
    return rc * lax.rsqrt(var + LN_EPS) * g + b


def _t5_bucket(d):
    d = jnp.maximum(d, 0)
    max_exact = N_BUCKETS // 2
    df = jnp.maximum(d, 1).astype(F32)
    large = max_exact + (jnp.log(df / max_exact) / math.log(MAX_DISTANCE / max_exact)
                         * (N_BUCKETS - max_exact)).astype(jnp.int32)
    large = jnp.minimum(large, N_BUCKETS - 1)
    return jnp.where(d < max_exact, d, large)


def _bias_kernel(tab_ref, bp_ref, bs_ref, *, n_new):
    qi = lax.broadcasted_iota(jnp.int32, (BLOCK, 2 * BLOCK), 0)
    sj = lax.broadcasted_iota(jnp.int32, (BLOCK, 2 * BLOCK), 1)
    d = qi + BLOCK - sj
    valid = (d >= 0) & (d <= WINDOW)
    bk = _t5_bucket(d)
    for h in range(N_HEADS):
        acc = jnp.zeros(d.shape, F32)
        for b in range(N_BUCKETS):
            acc = jnp.where(bk == b, tab_ref[b, h], acc)
        bp_ref[h] = jnp.where(valid, acc, NEG_INF)
    rows, cols = bs_ref.shape
    r = lax.broadcasted_iota(jnp.int32, (rows, cols), 0)
    s = lax.broadcasted_iota(jnp.int32, (rows, cols), 1)
    s_pos = jnp.where(s < WINDOW, (s + n_new) % WINDOW, s)
    ds = r // N_HEADS + WINDOW - s_pos
    hs = r % N_HEADS
    valid_s = (ds >= 0) & (ds <= WINDOW) & (s < WINDOW + n_new)
    bks = _t5_bucket(ds)
    acc = jnp.zeros((rows, cols), F32)
    for h in range(N_HEADS):
        for b in range(N_BUCKETS):
            acc = jnp.where((bks == b) & (hs == h), tab_ref[b, h], acc)
    bs_ref[...] = jnp.where(valid_s, acc, NEG_INF)


def _bias_tables(rel_bias, n_new):
    rows = n_new * N_HEADS
    return pl.pallas_call(
        functools.partial(_bias_kernel, n_new=n_new),
        out_shape=(jax.ShapeDtypeStruct((N_HEADS, BLOCK, 2 * BLOCK), F32),
                   jax.ShapeDtypeStruct((rows, WINDOW + SUBLANES), F32)),
        in_specs=[pl.BlockSpec(memory_space=pltpu.SMEM)],
        name="bias_tables",
    )(rel_bias)


def _zero_after(v):
    u = lax.bitcast_convert_type(v, jnp.uint32)
    z = lax.shift_right_logical(lax.shift_right_logical(u, jnp.uint32(16)), jnp.uint32(16))
    return lax.bitcast_convert_type(z, F32)


def _fold_rows(a):
    a = a.reshape(a.shape[0] // 16, 16, a.shape[1]).sum(axis=0)
    return a.reshape(16, a.shape[1] // LANES, LANES).sum(axis=1)


def _ffn_pre_norm(x, wg_ref, wu_ref, wd_ref, act_ref, alpha, side_work):
    rows = x.shape[0]
    xb = x.astype(BF16)
    for ci, c in enumerate(range(0, D_FF, FFN_CHUNK)):
        gate = jnp.dot(xb, wg_ref[:, c:c + FFN_CHUNK], preferred_element_type=F32)
        up = jnp.dot(xb, wu_ref[:, c:c + FFN_CHUNK], preferred_element_type=F32)
        act = jax.nn.silu(gate) * up
        act_ref[0:rows, c:c + FFN_CHUNK] = act.astype(BF16)
        z = side_work(ci)
        if z is not None:
            act_ref[0:16, c:c + LANES] = (act[0:16, 0:LANES] + z).astype(BF16)
    y = jnp.dot(act_ref[0:rows, :], wd_ref[...], preferred_element_type=F32)
    return alpha * x + 0.5 * y


def _ffn_kernel(xp_ref, xs_ref, wg_ref, wu_ref, wd_ref, g_ref, b_ref, *rest, alpha, n_big,
                n_cast):
    f32_refs, rest = rest[:n_cast], rest[n_cast:]
    op_ref, os_ref = rest[:2]
    bf16_refs, (act_ref, r_ref) = rest[2:2 + n_cast], rest[2 + n_cast:]
    i = pl.program_id(0)
    w = (wg_ref, wu_ref, wd_ref)
    g, b = g_ref[...], b_ref[...]
    part = FFN_ROWS // FFN_NORM_PARTS

    def norm_previous_tile(ci):
        if ci >= FFN_NORM_PARTS:
            return None
        rs = slice(ci * part, (ci + 1) * part)
        out = _layer_norm(r_ref[rs, :], g, b)
        op_ref[rs, :] = out
        return _zero_after(_fold_rows(out))

    def norm_and_convert(ci):
        if ci == FFN_NORM_PARTS:
            for src, dst in zip(f32_refs, bf16_refs):
                dst[...] = src[...].astype(BF16)
        return norm_previous_tile(ci)

    @pl.when(i == 0)
    def _():
        r_ref[...] = jnp.zeros(r_ref.shape, F32)

    @pl.when(i < n_big)
    def _():
        r_ref[...] = _ffn_pre_norm(xp_ref[...], *w, act_ref, alpha, norm_and_convert)

    @pl.when(i == n_big)
    def _():
        os_ref[...] = _ffn_pre_norm(xs_ref[...], *w, act_ref, alpha, norm_previous_tile)

    @pl.when(i == n_big + 1)
    def _():
        os_ref[...] = _layer_norm(os_ref[...], g, b)


def _ffn(xp, xs, w_cur, to_cast, ln_g, ln_b, layer, ln_idx, alpha):
    assert xp.shape[0] % FFN_ROWS == 0 and D_FF % FFN_CHUNK == 0
    assert FFN_NORM_PARTS < D_FF // FFN_CHUNK and FFN_ROWS % (16 * FFN_NORM_PARTS) == 0
    n_big = xp.shape[0] // FFN_ROWS
    rows_s = xs.shape[0]
    assert rows_s <= FFN_ROWS
    last = lambda i: jnp.minimum(i, n_big - 1)
    in_spec = pl.BlockSpec((FFN_ROWS, D_MODEL), lambda i: (last(i), 0))
    out_spec = pl.BlockSpec((FFN_ROWS, D_MODEL), lambda i: (jnp.clip(i - 1, 0, n_big - 1), 0))
    small_in = pl.BlockSpec((rows_s, D_MODEL), lambda i: (0, 0), pipeline_mode=pl.Buffered(1))
    small_out = pl.BlockSpec((rows_s, D_MODEL), lambda i: (0, 0), pipeline_mode=pl.Buffered(1))
    in_specs = [in_spec, small_in] + [_const_spec(a.shape) for a in w_cur] + [
        _layer_spec((1, D_MODEL), layer, ln_idx), _layer_spec((1, D_MODEL), layer, ln_idx)]
    out_specs = [out_spec, small_out]
    out_shape = [jax.ShapeDtypeStruct(xp.shape, F32), jax.ShapeDtypeStruct(xs.shape, F32)]
    args = [xp, xs, *w_cur, ln_g, ln_b]
    for a, lead in to_cast:
        r, c = a.shape[-2] // n_big, a.shape[-1]
        assert a.shape[-2] % (16 * n_big) == 0 and len(lead) == a.ndim - 2
        in_specs.append(pl.BlockSpec((None,) * len(lead) + (r, c),
                                     lambda i, lead=lead: lead + (last(i), 0)))
        out_specs.append(pl.BlockSpec((r, c), lambda i: (last(i), 0)))
        out_shape.append(jax.ShapeDtypeStruct(a.shape[-2:], BF16))
        args.append(a)
    return pl.pallas_call(
        functools.partial(_ffn_kernel, alpha=alpha, n_big=n_big, n_cast=len(to_cast)),
        out_shape=out_shape,
        grid=(n_big + 2,),
        in_specs=in_specs,
        out_specs=out_specs,
        scratch_shapes=[pltpu.VMEM((FFN_ROWS, D_FF), BF16), pltpu.VMEM((FFN_ROWS, D_MODEL), F32)],
        compiler_params=_params("arbitrary"),
        name="ffn_ln",
    )(*args)


def _mixer_kernel(x_ref, win_ref, wout_ref, convw_ref, sink_ref, bias_ref, fmask_ref,
                  kinit_ref, vinit_ref, uinit_ref, g_ref, b_ref, *rest,
                  rows, steps, n_tiles, alpha, layer, n_relaid, rotate):
    src_refs, rest = rest[:n_relaid], rest[n_relaid:]
    (xo_ref, klast_ref, vlast_ref, ulast_ref), rest = rest[:4], rest[4:]
    dst_refs, (kd_scr, vd_scr, uscr, ascr, r_scr) = rest[:n_relaid], rest[n_relaid:]
    step = pl.program_id(0)
    g, b = g_ref[...], b_ref[...]
    io_rows = x_ref.shape[0]

    @pl.when(step == 0)
    def _():
        r_scr[...] = jnp.zeros(r_scr.shape, F32)

    @pl.when(step == n_tiles)
    def _():
        xo_ref[...] = _layer_norm(r_scr[rows - io_rows:, :], g, b)

    @pl.when(step < n_tiles)
    def _():
        _mixer_tile(x_ref, win_ref, wout_ref, convw_ref, sink_ref, bias_ref, fmask_ref,
                    kinit_ref, vinit_ref, uinit_ref, g, b, xo_ref, klast_ref, vlast_ref,
                    ulast_ref, kd_scr, vd_scr, uscr, ascr, r_scr, t=step % steps, rows=rows,
                    alpha=alpha, layer=layer)
        for src, dst in zip(src_refs, dst_refs):
            for i in range(src.shape[0]):
                buf = src[i].reshape(D_KV, WINDOW).T
                dst[i, 0:WINDOW - rotate, :] = buf[rotate:, :]
                dst[i, WINDOW - rotate:, :] = buf[0:rotate, :]


def _mixer_tile(x_ref, win_ref, wout_ref, convw_ref, sink_ref, bias_ref, fmask_ref,
                kinit_ref, vinit_ref, uinit_ref, g, b, xo_ref, klast_ref, vlast_ref, ulast_ref,
                kd_scr, vd_scr, uscr, ascr, r_scr, *, t, rows, alpha, layer):
    lane = lax.broadcasted_iota(jnp.int32, (1, LANES), 1)
    low = lane < HEAD_DIM

    def dup_heads(a):
        sw = pltpu.roll(a, HEAD_DIM, 1)
        return (jnp.where(low, a, sw).astype(BF16), jnp.where(low, sw, a).astype(BF16))

    @pl.when(t == 0)
    def _():
        for kv, (kd, vd) in enumerate(zip(dup_heads(kinit_ref[0]), dup_heads(vinit_ref[0]))):
            kd_scr[kv, 0:BLOCK, :] = kd
            vd_scr[kv, 0:BLOCK, :] = vd
        uscr[0:SUBLANES, :] = uinit_ref[0]

    x = x_ref[...]
    whole_tile = x.shape[0] == rows
    if not whole_tile:
        x = jnp.concatenate([jnp.zeros((rows - x.shape[0], D_MODEL), F32), x], axis=0)
    xb = x.astype(BF16)
    c0 = D_ATTN + 2 * D_KV
    zq = jnp.dot(xb, win_ref[:, 0:c0], preferred_element_type=F32)
    zc = jnp.dot(xb, win_ref[:, c0:], preferred_element_type=F32)
    k = zq[:, D_ATTN:D_ATTN + D_KV]
    v = zq[:, D_ATTN + D_KV:c0]
    if whole_tile:
        part = rows // MIX_NORM_PARTS
        edges = []
        for ci in range(MIX_NORM_PARTS):
            rs = slice(ci * part, (ci + 1) * part)
            out = _layer_norm(r_scr[rs, :], g, b)
            xo_ref[rs, :] = out
            edges.append(_zero_after(_fold_rows(out)))
        k = jnp.concatenate([k[0:16] + sum(edges[0::2]), k[16:]], axis=0)
        v = jnp.concatenate([v[0:16] + sum(edges[1::2]), v[16:]], axis=0)
    u = zc[:, D_CONV:2 * D_CONV] * zc[:, 2 * D_CONV:3 * D_CONV]
    klast_ref[0] = k[rows - BLOCK:, :]
    vlast_ref[0] = v[rows - BLOCK:, :]
    ulast_ref[0] = u[rows - SUBLANES:, :]
    for kv, (kd, vd) in enumerate(zip(dup_heads(k), dup_heads(v))):
        kd_scr[kv, BLOCK:BLOCK + rows, :] = kd
        vd_scr[kv, BLOCK:BLOCK + rows, :] = vd
    uscr[SUBLANES:SUBLANES + rows, :] = u

    qs = zq[:, :D_ATTN] * Q_SCALE
    lane_q = lax.broadcasted_iota(jnp.int32, (1, D_ATTN), 1) % LANES
    q_even = jnp.where(lane_q < HEAD_DIM, qs, 0.0).astype(BF16)
    q_odd = jnp.where(lane_q < HEAD_DIM, 0.0, qs).astype(BF16)

    first = jnp.where(t == 0, fmask_ref[...], 0.0)
    ones_cols = jnp.ones((2 * BLOCK, LANES), BF16)
    for j in range(rows // BLOCK):
        r0 = j * BLOCK
        for kv in range(N_KV_HEADS):
            kd = kd_scr[kv, r0:r0 + 2 * BLOCK, :]
            vdx = jnp.concatenate([vd_scr[kv, r0:r0 + 2 * BLOCK, :], ones_cols], axis=1)
            heads = range(kv * GQA_GROUP, (kv + 1) * GQA_GROUP)
            q4 = jnp.concatenate(
                [(q_odd if h % 2 else q_even)[r0:r0 + BLOCK, (h // 2) * LANES:(h // 2 + 1) * LANES]
                 for h in heads], axis=0)
            s4 = lax.dot_general(q4, kd, (((1,), (1,)), ((), ())), preferred_element_type=F32)
            ps, es = [], []
            for g, h in enumerate(heads):
                s = s4[g * BLOCK:(g + 1) * BLOCK] + bias_ref[h]
                if j == 0:
                    s = s + first
                sl, sr = s[:, :LANES], s[:, LANES:]
                sk = sink_ref[layer, h]
                m1 = jnp.max(jnp.maximum(sl, sr), axis=-1, keepdims=True)
                mb = jnp.broadcast_to(jnp.maximum(m1, sk), (BLOCK, LANES))
                ps.append(jnp.concatenate([jnp.exp(sl - mb), jnp.exp(sr - mb)],
                                          axis=1).astype(BF16))
                es.append(jnp.exp(sk - mb))
            ox = jnp.dot(jnp.concatenate(ps, axis=0), vdx, preferred_element_type=F32)
            outs = []
            for g in range(GQA_GROUP):
                og = ox[g * BLOCK:(g + 1) * BLOCK]
                outs.append(og[:, :LANES] / (og[:, LANES:] + es[g]))
            for i in range(GQA_GROUP // 2):
                slab = jnp.where(low, outs[2 * i], outs[2 * i + 1])
                col = (kv * GQA_GROUP // 2 + i) * LANES
                ascr[r0:r0 + BLOCK, col:col + LANES] = slab.astype(BF16)

    um2 = uscr[SUBLANES - 2:SUBLANES - 2 + rows, :]
    um1 = uscr[SUBLANES - 1:SUBLANES - 1 + rows, :]
    conv = convw_ref[0:1, :] * um2 + convw_ref[1:2, :] * um1 + convw_ref[2:3, :] * u
    mix = zc[:, 0:D_CONV] * conv
    y = (jnp.dot(ascr[...], wout_ref[0:D_ATTN, :], preferred_element_type=F32)
         + jnp.dot(mix.astype(BF16), wout_ref[D_ATTN:, :], preferred_element_type=F32))
    r_scr[...] = alpha * x + y

    for kv in range(N_KV_HEADS):
        kd_scr[kv, 0:BLOCK, :] = kd_scr[kv, rows:rows + BLOCK, :]
        vd_scr[kv, 0:BLOCK, :] = vd_scr[kv, rows:rows + BLOCK, :]
    uscr[0:SUBLANES, :] = uscr[rows:rows + SUBLANES, :]


def _mixer(x, row0, n_seq, seq_len, rows, in_place, win, wout, convw, sink, bias_p,
           fmask, kinit, vinit, uinit, ln_g, ln_b, layer, alpha, name, relay=(), rotate=0):
    io_rows = min(rows, seq_len)
    assert seq_len % io_rows == 0 and rows % BLOCK == 0 and row0 % io_rows == 0
    assert rows % (16 * MIX_NORM_PARTS) == 0 and io_rows % SUBLANES == 0
    assert io_rows == rows or n_seq == 1
    steps = seq_len // io_rows
    n_tiles = n_seq * steps
    blk0 = row0 // io_rows
    x_spec = pl.BlockSpec((io_rows, D_MODEL), lambda s: (blk0 + jnp.minimum(s, n_tiles - 1), 0))
    xo_spec = pl.BlockSpec((io_rows, D_MODEL), lambda s: (blk0 + jnp.maximum(s - 1, 0), 0))
    seq_spec = lambda shape: pl.BlockSpec(
        (1,) + shape, lambda s: (jnp.minimum(s // steps, n_seq - 1), 0, 0))
    kern = functools.partial(_mixer_kernel, rows=rows, steps=steps, n_tiles=n_tiles, alpha=alpha,
                             layer=layer, n_relaid=len(relay), rotate=rotate)
    relay_in, relay_out, relay_shape = [], [], []
    for a in relay:
        per_step = a.shape[0] // n_tiles
        assert a.shape[0] % n_tiles == 0 and a.shape[1] * a.shape[2] == D_KV
        at_step = lambda s: (jnp.minimum(s, n_tiles - 1), 0, 0)
        relay_in.append(pl.BlockSpec((per_step,) + a.shape[1:], lambda s: at_step(s) + (0,)))
        relay_out.append(pl.BlockSpec((per_step, WINDOW, D_KV), at_step))
        relay_shape.append(jax.ShapeDtypeStruct((a.shape[0], WINDOW, D_KV), F32))
    return pl.pallas_call(
        kern,
        out_shape=(jax.ShapeDtypeStruct(x.shape, F32),
                   jax.ShapeDtypeStruct((n_seq, BLOCK, D_KV), F32),
                   jax.ShapeDtypeStruct((n_seq, BLOCK, D_KV), F32),
                   jax.ShapeDtypeStruct((n_seq, SUBLANES, D_CONV), F32), *relay_shape),
        grid=(n_tiles + 1,),
        in_specs=[x_spec,
                  _const_spec((D_MODEL, D_IN)),
                  _const_spec((D_MODEL, D_MODEL)),
                  _layer_spec((CONV_WIDTH, D_CONV), layer),
                  pl.BlockSpec(memory_space=pltpu.SMEM),
                  _const_spec(bias_p.shape), _const_spec(fmask.shape),
                  _const_spec(kinit.shape), _const_spec(vinit.shape), _const_spec(uinit.shape),
                  _layer_spec((1, D_MODEL), layer, 1),
                  _layer_spec((1, D_MODEL), layer, 1), *relay_in],
        out_specs=(xo_spec, seq_spec((BLOCK, D_KV)), seq_spec((BLOCK, D_KV)),
                   seq_spec((SUBLANES, D_CONV)), *relay_out),
        scratch_shapes=[pltpu.VMEM((N_KV_HEADS, BLOCK + rows, D_KV), BF16),
                        pltpu.VMEM((N_KV_HEADS, BLOCK + rows, D_KV), BF16),
                        pltpu.VMEM((SUBLANES + rows, D_CONV), F32),
                        pltpu.VMEM((rows, D_ATTN), BF16),
                        pltpu.VMEM((rows, D_MODEL), F32)],
        input_output_aliases={0: 0} if in_place else {},
        compiler_params=_params("arbitrary"),
        name=name,
    )(x, win, wout, convw, sink, bias_p, fmask, kinit, vinit, uinit, ln_g, ln_b, *relay)


def _heads_to_rows(q, n_seq):
    lane = lax.broadcasted_iota(jnp.int32, (1, LANES), 1)
    slabs = []
    for h in range(N_HEADS):
        pair = q[:, (h // 2) * LANES:(h // 2 + 1) * LANES]
        kv = h // GQA_GROUP
        data = pair if h % 2 == kv else pltpu.roll(pair, HEAD_DIM, 1)
        on_kv = (lane >= HEAD_DIM) if kv else (lane < HEAD_DIM)
        slabs.append(jnp.where(on_kv, data, 0.0))
    rows = jnp.concatenate(slabs, axis=1).reshape(q.shape[0], N_HEADS, LANES)
    return rows.reshape(n_seq, q.shape[0] // n_seq * N_HEADS, LANES)


def _rows_to_heads(o):
    n = o.shape[0] * o.shape[1] // N_HEADS
    wide = o.reshape(n, N_HEADS, LANES).reshape(n, N_HEADS * LANES)
    lane = lax.broadcasted_iota(jnp.int32, (1, LANES), 1)
    pairs = []
    for p in range(N_HEADS // 2):
        halves = []
        for e in range(2):
            h = 2 * p + e
            slab = wide[:, h * LANES:(h + 1) * LANES]
            halves.append(slab if h // GQA_GROUP == e else pltpu.roll(slab, HEAD_DIM, 1))
        pairs.append(jnp.where(lane < HEAD_DIM, halves[0], halves[1]))
    return jnp.concatenate(pairs, axis=1)


def _sample_mixer_kernel(x_ref, win_ref, wout_ref, convw_ref, state_ref, bias_ref, sink_ref,
                         g_ref, b_ref, ck_ref, cv_ref,
                         xo_ref, u_ref, ok_ref, ov_ref,
                         q_scr, kn_scr, vn_scr, o_scr, mix_scr, *, n_new, n_seq, chunk, alpha):
    i = pl.program_id(0)
    n_chunks = n_seq // chunk
    c0 = D_ATTN + 2 * D_KV

    @pl.when(i == 0)
    def _():
        z = jnp.dot(x_ref[...].astype(BF16), win_ref[...], preferred_element_type=F32)
        q_scr[...] = _heads_to_rows(z[:, :D_ATTN] * Q_SCALE, n_seq).astype(BF16)
        kn_scr[:, 0:n_new, :] = z[:, D_ATTN:D_ATTN + D_KV].reshape(n_seq, n_new, D_KV)
        vn_scr[:, 0:n_new, :] = z[:, D_ATTN + D_KV:c0].reshape(n_seq, n_new, D_KV)
        kn_scr[:, n_new:, :] = jnp.zeros((n_seq, SUBLANES - n_new, D_KV), F32)
        vn_scr[:, n_new:, :] = jnp.zeros((n_seq, SUBLANES - n_new, D_KV), F32)
        u = z[:, c0 + D_CONV:c0 + 2 * D_CONV] * z[:, c0 + 2 * D_CONV:c0 + 3 * D_CONV]
        u_ref[...] = u
        n = u.shape[0]
        tok = lax.broadcasted_iota(jnp.int32, u.shape, 0) % n_new
        st = state_ref[...]
        um2 = jnp.where(tok < 2, st, pltpu.roll(u, 2, 0))
        um1 = jnp.where(tok < 1, pltpu.roll(st, n - 1, 0), pltpu.roll(u, 1, 0))
        conv = convw_ref[0:1, :] * um2 + convw_ref[1:2, :] * um1 + convw_ref[2:3, :] * u
        mix_scr[...] = (z[:, c0:c0 + D_CONV] * conv).astype(BF16)

    @pl.when((i >= 1) & (i <= n_chunks))
    def _():
        s0 = pl.multiple_of((i - 1) * chunk, chunk)
        qb = q_scr[pl.ds(s0, chunk)]
        kn = kn_scr[pl.ds(s0, chunk)]
        vn = vn_scr[pl.ds(s0, chunk)]
        ck = ck_ref[...]
        cv = cv_ref[...]
        bias = bias_ref[...]
        sink = sink_ref[...]
        s_c = jnp.einsum("bqd,bkd->bqk", qb, ck.astype(BF16),
                         preferred_element_type=F32) + bias[None, :, :WINDOW]
        s_n = jnp.einsum("bqd,bkd->bqk", qb, kn.astype(BF16),
                         preferred_element_type=F32) + bias[None, :, WINDOW:]
        m = jnp.maximum(jnp.maximum(jnp.max(s_c, axis=-1, keepdims=True),
                                    jnp.max(s_n, axis=-1, keepdims=True)), sink[None])
        p_c = jnp.exp(s_c - m)
        p_n = jnp.exp(s_n - m)
        den = (jnp.sum(p_c, axis=-1, keepdims=True) + jnp.sum(p_n, axis=-1, keepdims=True)
               + jnp.exp(sink[None] - m))
        o = (jnp.einsum("bqk,bkd->bqd", p_c.astype(BF16), cv.astype(BF16),
                        preferred_element_type=F32)
             + jnp.einsum("bqk,bkd->bqd", p_n.astype(BF16), vn.astype(BF16),
                          preferred_element_type=F32))
        o_scr[pl.ds(s0, chunk)] = o / den
        keep = SUBLANES - n_new
        ok_ref[:, 0:keep, :] = ck_ref[:, WINDOW - SUBLANES:WINDOW - n_new, :]
        ok_ref[:, keep:SUBLANES, :] = kn[:, 0:n_new, :]
        ov_ref[:, 0:keep, :] = cv_ref[:, WINDOW - SUBLANES:WINDOW - n_new, :]
        ov_ref[:, keep:SUBLANES, :] = vn[:, 0:n_new, :]

    @pl.when(i == n_chunks + 1)
    def _():
        a = _rows_to_heads(o_scr[...]).astype(BF16)
        y = (jnp.dot(a, wout_ref[0:D_ATTN, :], preferred_element_type=F32)
             + jnp.dot(mix_scr[...], wout_ref[D_ATTN:, :], preferred_element_type=F32))
        xo_ref[...] = _layer_norm(alpha * x_ref[...] + y, g_ref[...], b_ref[...])


def _sample_mixer(xs, n_seq, n_new, win, wout, convw, state_rows, bias_s, sink_rows, ln_g, ln_b,
                  k_buf, v_buf, layer, alpha):
    assert n_seq % SAMPLE_SEQS == 0 and n_new <= SUBLANES
    n = n_seq * n_new
    qr = n_new * N_HEADS
    n_chunks = n_seq // SAMPLE_SEQS
    x_spec = pl.BlockSpec((n, D_MODEL), lambda i: (0, 0))
    at_chunk = lambda i: jnp.clip(i - 1, 0, n_chunks - 1)
    lay_spec = pl.BlockSpec((None, SAMPLE_SEQS, WINDOW, D_KV),
                            lambda i: (layer, at_chunk(i), 0, 0))
    tail_spec = pl.BlockSpec((None, SAMPLE_SEQS, SUBLANES, D_KV),
                             lambda i: (layer, at_chunk(i), WINDOW // SUBLANES - 1, 0))
    kern = functools.partial(_sample_mixer_kernel, n_new=n_new, n_seq=n_seq, chunk=SAMPLE_SEQS,
                             alpha=alpha)
    return pl.pallas_call(
        kern,
        out_shape=(jax.ShapeDtypeStruct(xs.shape, F32),
                   jax.ShapeDtypeStruct((n, D_CONV), F32),
                   jax.ShapeDtypeStruct(k_buf.shape, F32), jax.ShapeDtypeStruct(v_buf.shape, F32)),
        grid=(n_chunks + 2,),
        in_specs=[x_spec,
                  _const_spec((D_MODEL, D_IN)),
                  _const_spec((D_MODEL, D_MODEL)),
                  _layer_spec((CONV_WIDTH, D_CONV), layer),
                  _layer_spec((n, D_CONV), layer),
                  _layer_spec(bias_s.shape), _layer_spec(sink_rows.shape[1:], layer),
                  _layer_spec((1, D_MODEL), layer, 1), _layer_spec((1, D_MODEL), layer, 1),
                  lay_spec, lay_spec],
        out_specs=(x_spec, pl.BlockSpec((n, D_CONV), lambda i: (0, 0)), tail_spec, tail_spec),
        scratch_shapes=[pltpu.VMEM((n_seq, qr, LANES), BF16),
                        pltpu.VMEM((n_seq, SUBLANES, D_KV), F32),
                        pltpu.VMEM((n_seq, SUBLANES, D_KV), F32),
                        pltpu.VMEM((n_seq, qr, LANES), F32),
                        pltpu.VMEM((n, D_CONV), BF16)],
        input_output_aliases={0: 0, 9: 2, 10: 3},
        compiler_params=_params("arbitrary"),
        name="sample_mixer",
    )(xs, win, wout, convw, state_rows, bias_s, sink_rows, ln_g, ln_b, k_buf, v_buf)


def kernel(x_prompt, x_sample, cache_k, cache_v, state_conv, meta_tokens, rel_bias, w_in, conv_w,
           attn_sink, w_out, ffn_w_gate, ffn_w_up, ffn_w_down, ln_g, ln_b):
    depth = w_in.shape[0]
    alpha = float((2 * depth) ** 0.25)
    n_prompt, seq, d_model = x_prompt.shape
    n_sample, n_new, _ = x_sample.shape
    assert d_model == D_MODEL and seq % MIX_ROWS == 0
    assert n_new >= CONV_WIDTH - 1 and cache_k.shape[2] == WINDOW
    rows_p = n_prompt * seq
    rows_s = n_sample * n_new
    assert rows_s % BLOCK == 0

    xp = x_prompt.reshape(rows_p, D_MODEL)
    xs = jnp.concatenate([x_sample.reshape(rows_s, D_MODEL), meta_tokens.astype(F32)], axis=0)

    bias_p, bias_s = _bias_tables(rel_bias, n_new)
    col = jnp.arange(2 * BLOCK)[None, :]
    fmask_prompt = jnp.where(col < META_PAD, NEG_INF, 0.0).astype(F32)
    fmask_meta = jnp.where(col < BLOCK + META_PAD, NEG_INF, 0.0).astype(F32)
    zeros_kv = jnp.zeros((1, BLOCK, D_KV), F32)
    zeros_u = jnp.zeros((1, SUBLANES, D_CONV), F32)

    ffn_f32 = (ffn_w_gate, ffn_w_up, ffn_w_down)
    w_ffn = tuple(a[0, 0].astype(BF16) for a in ffn_f32)

    def ffn(xp, xs, w_cur, layer, which):
        nxt = (layer, 1) if which == 0 else (layer + 1, 0)
        to_cast = [(a, nxt) for a in ffn_f32] if nxt[0] < depth else []
        if which == 0:
            to_cast += [(w_in, (layer,)), (w_out, (layer,))]
        xp, xs, *cast = _ffn(xp, xs, w_cur, to_cast, ln_g4, ln_b4, layer, 2 * which, alpha)
        return xp, xs, tuple(cast[:3]), tuple(cast[3:])

    ln_g4 = ln_g.reshape(depth, 3, 1, D_MODEL)
    ln_b4 = ln_b.reshape(depth, 3, 1, D_MODEL)
    relay = tuple(jnp.transpose(c, (0, 1, 3, 4, 2)).reshape(-1, N_KV_HEADS, HEAD_DIM, WINDOW)
                  for c in (cache_k, cache_v))
    state_rows = jnp.pad(state_conv, ((0, 0), (0, 0), (0, n_new - (CONV_WIDTH - 1)), (0, 0))
                         ).reshape(depth, rows_s, D_CONV)
    sink_rows = jnp.tile(attn_sink, (1, n_new)).reshape(depth, n_new * N_HEADS, 1)

    kp, vp, cp, cs = [], [], [], []
    for l in range(depth):
        xp, xs, w_ffn, (win, wout) = ffn(xp, xs, w_ffn, l, 0)

        xs, k_m, v_m, u_m = _mixer(
            xs, rows_s, 1, N_META, BLOCK, True, win, wout, conv_w, attn_sink, bias_p,
            fmask_meta, zeros_kv, zeros_kv, zeros_u, ln_g4, ln_b4, l, alpha, "mixer_meta")
        xp, k_p, v_p, u_p, *relaid = _mixer(
            xp, 0, n_prompt, seq, MIX_ROWS, False, win, wout, conv_w, attn_sink, bias_p,
            fmask_prompt, k_m, v_m, u_m, ln_g4, ln_b4, l, alpha, "mixer_prompt",
            relay if l == 0 else (), n_new)
        if relaid:
            k_buf, v_buf = (a.reshape(depth, n_sample, WINDOW, D_KV) for a in relaid)

        xs, u_s, k_buf, v_buf = _sample_mixer(
            xs, n_sample, n_new, win, wout, conv_w, state_rows, bias_s, sink_rows, ln_g4, ln_b4,
            k_buf, v_buf, l, alpha)

        xp, xs, w_ffn, _ = ffn(xp, xs, w_ffn, l, 1)

        kp.append(k_p)
        vp.append(v_p)
        cp.append(u_p[:, SUBLANES - (CONV_WIDTH - 1):, :])
        cs.append(u_s.reshape(n_sample, n_new, D_CONV)[:, n_new - (CONV_WIDTH - 1):, :])

    kv_shape = (depth, -1, WINDOW, N_KV_HEADS, HEAD_DIM)
    return (xp.reshape(n_prompt, seq, D_MODEL),
            xs[:rows_s].reshape(n_sample, n_new, D_MODEL),
            jnp.stack(kp).reshape(kv_shape), jnp.stack(vp).reshape(kv_shape), jnp.stack(cp),
            k_buf.reshape(kv_shape), v_buf.reshape(kv_shape), jnp.stack(cs))
```

```python
import functools
import math

import jax
import jax.numpy as jnp
from jax import lax
from jax.experimental import pallas as pl
from jax.experimental.pallas import tpu as pltpu

F32 = jnp.float32
BF16 = jnp.bfloat16

D_MODEL = 1024
N_HEADS = 8
N_KV_HEADS = 2
HEAD_DIM = 64
GQA_GROUP = N_HEADS // N_KV_HEADS
D_ATTN = N_HEADS * HEAD_DIM
D_CONV = D_MODEL - D_ATTN
D_KV = N_KV_HEADS * HEAD_DIM
D_IN = D_ATTN + 2 * D_KV + 3 * D_CONV
D_FF = 2816
CONV_WIDTH = 3
WINDOW = 128
BLOCK = 128
N_META = 16
N_BUCKETS = 32
MAX_DISTANCE = 128
LN_EPS = 1e-5
Q_SCALE = HEAD_DIM ** -0.5

V7X_VMEM_LIMIT_BYTES = 60 * 1024 * 1024
V7X_MXU_COLUMNS = 256
SUBLANES = 8
LANES = 128

FFN_ROWS = 1024
FFN_NORM_PARTS = 8
FFN_CHUNK = V7X_MXU_COLUMNS
MIX_ROWS = 512
MIX_NORM_PARTS = 2
SAMPLE_SEQS = 32
META_PAD = BLOCK - N_META

NEG_INF = float("-inf")


def _const_spec(shape):
    nd = len(shape)
    return pl.BlockSpec(shape, lambda *_: (0,) * nd, pipeline_mode=pl.Buffered(1))


def _layer_spec(shape, *lead):
    block = (None,) * len(lead) + tuple(shape)
    idx = tuple(lead) + (0,) * len(shape)
    return pl.BlockSpec(block, lambda *_: idx, pipeline_mode=pl.Buffered(1))


def _params(*sem):
    return pltpu.CompilerParams(dimension_semantics=sem,
                                vmem_limit_bytes=V7X_VMEM_LIMIT_BYTES)


def _layer_norm(r, g, b):
    mu = jnp.mean(r, axis=-1, keepdims=True)
    rc = r - mu
    var = jnp.mean(rc * rc, axis=-1, keepdims=True)
    return rc * lax.rsqrt(var + LN_EPS) * g + b


def _t5_bucket(d):
    d = jnp.maximum(d, 0)
    max_exact = N_BUCKETS // 2
    df = jnp.maximum(d, 1).astype(F32)
    large = max_exact + (jnp.log(df / max_exact) / math.log(MAX_DISTANCE / max_exact)
                         * (N_BUCKETS - max_exact)).astype(jnp.int32)
    large = jnp.minimum(large, N_BUCKETS - 1)
    return jnp.where(d < max_exact, d, large)


def _bias_kernel(tab_ref, bp_ref, bs_ref, *, n_new):
    qi = lax.broadcasted_iota(jnp.int32, (BLOCK, 2 * BLOCK), 0)
    sj = lax.broadcasted_iota(jnp.int32, (BLOCK, 2 * BLOCK), 1)
    d = qi + BLOCK - sj
    valid = (d >= 0) & (d <= WINDOW)
    bk = _t5_bucket(d)
    for h in range(N_HEADS):
        acc = jnp.zeros(d.shape, F32)
        for b in range(N_BUCKETS):
            acc = jnp.where(bk == b, tab_ref[b, h], acc)
        bp_ref[h] = jnp.where(valid, acc, NEG_INF)
    rows, cols = bs_ref.shape
    r = lax.broadcasted_iota(jnp.int32, (rows, cols), 0)
    s = lax.broadcasted_iota(jnp.int32, (rows, cols), 1)
    s_pos = jnp.where(s < WINDOW, (s + n_new) % WINDOW, s)
    ds = r // N_HEADS + WINDOW - s_pos
    hs = r % N_HEADS
    valid_s = (ds >= 0) & (ds <= WINDOW) & (s < WINDOW + n_new)
    bks = _t5_bucket(ds)
    acc = jnp.zeros((rows, cols), F32)
    for h in range(N_HEADS):
        for b in range(N_BUCKETS):
            acc = jnp.where((bks == b) & (hs == h), tab_ref[b, h], acc)
    bs_ref[...] = jnp.where(valid_s, acc, NEG_INF)


def _bias_tables(rel_bias, n_new):
    rows = n_new * N_HEADS
    return pl.pallas_call(
        functools.partial(_bias_kernel, n_new=n_new),
        out_shape=(jax.ShapeDtypeStruct((N_HEADS, BLOCK, 2 * BLOCK), F32),
                   jax.ShapeDtypeStruct((rows, WINDOW + SUBLANES), F32)),
        in_specs=[pl.BlockSpec(memory_space=pltpu.SMEM)],
        name="bias_tables",
    )(rel_bias)


def _zero_after(v):
    u = lax.bitcast_convert_type(v, jnp.uint32)
    z = lax.shift_right_logical(lax.shift_right_logical(u, jnp.uint32(16)), jnp.uint32(16))
    return lax.bitcast_convert_type(z, F32)


def _fold_rows(a):
    a = a.reshape(a.shape[0] // 16, 16, a.shape[1]).sum(axis=0)
    return a.reshape(16, a.shape[1] // LANES, LANES).sum(axis=1)


def _ffn_pre_norm(x, wg_ref, wu_ref, wd_ref, act_ref, alpha, side_work):
    rows = x.shape[0]
    xb = x.astype(BF16)
    for ci, c in enumerate(range(0, D_FF, FFN_CHUNK)):
        gate = jnp.dot(xb, wg_ref[:, c:c + FFN_CHUNK], preferred_element_type=F32)
        up = jnp.dot(xb, wu_ref[:, c:c + FFN_CHUNK], preferred_element_type=F32)
        act = jax.nn.silu(gate) * up
        act_ref[0:rows, c:c + FFN_CHUNK] = act.astype(BF16)
        z = side_work(ci)
        if z is not None:
            act_ref[0:16, c:c + LANES] = (act[0:16, 0:LANES] + z).astype(BF16)
    y = jnp.dot(act_ref[0:rows, :], wd_ref[...], preferred_element_type=F32)
    return alpha * x + 0.5 * y


def _ffn_kernel(xp_ref, xs_ref, wg_ref, wu_ref, wd_ref, g_ref, b_ref, *rest, alpha, n_big,
                n_cast):
    f32_refs, rest = rest[:n_cast], rest[n_cast:]
    op_ref, os_ref = rest[:2]
    bf16_refs, (act_ref, r_ref) = rest[2:2 + n_cast], rest[2 + n_cast:]
    i = pl.program_id(0)
    w = (wg_ref, wu_ref, wd_ref)
    g, b = g_ref[...], b_ref[...]
    part = FFN_ROWS // FFN_NORM_PARTS

    def norm_previous_tile(ci):
        if ci >= FFN_NORM_PARTS:
            return None
        rs = slice(ci * part, (ci + 1) * part)
        out = _layer_norm(r_ref[rs, :], g, b)
        op_ref[rs, :] = out
        return _zero_after(_fold_rows(out))

    def norm_and_convert(ci):
        if ci == FFN_NORM_PARTS:
            for src, dst in zip(f32_refs, bf16_refs):
                dst[...] = src[...].astype(BF16)
        return norm_previous_tile(ci)

    @pl.when(i == 0)
    def _():
        r_ref[...] = jnp.zeros(r_ref.shape, F32)

    @pl.when(i < n_big)
    def _():
        r_ref[...] = _ffn_pre_norm(xp_ref[...], *w, act_ref, alpha, norm_and_convert)

    @pl.when(i == n_big)
    def _():
        os_ref[...] = _ffn_pre_norm(xs_ref[...], *w, act_ref, alpha, norm_previous_tile)

    @pl.when(i == n_big + 1)
    def _():
        os_ref[...] = _layer_norm(os_ref[...], g, b)


def _ffn(xp, xs, w_cur, to_cast, ln_g, ln_b, layer, ln_idx, alpha):
    assert xp.shape[0] % FFN_ROWS == 0 and D_FF % FFN_CHUNK == 0
    assert FFN_NORM_PARTS < D_FF // FFN_CHUNK and FFN_ROWS % (16 * FFN_NORM_PARTS) == 0
    n_big = xp.shape[0] // FFN_ROWS
    rows_s = xs.shape[0]
    assert rows_s <= FFN_ROWS
    last = lambda i: jnp.minimum(i, n_big - 1)
    in_spec = pl.BlockSpec((FFN_ROWS, D_MODEL), lambda i: (last(i), 0))
    out_spec = pl.BlockSpec((FFN_ROWS, D_MODEL), lambda i: (jnp.clip(i - 1, 0, n_big - 1), 0))
    small_in = pl.BlockSpec((rows_s, D_MODEL), lambda i: (0, 0), pipeline_mode=pl.Buffered(1))
    small_out = pl.BlockSpec((rows_s, D_MODEL), lambda i: (0, 0), pipeline_mode=pl.Buffered(1))
    in_specs = [in_spec, small_in] + [_const_spec(a.shape) for a in w_cur] + [
        _layer_spec((1, D_MODEL), layer, ln_idx), _layer_spec((1, D_MODEL), layer, ln_idx)]
    out_specs = [out_spec, small_out]
    out_shape = [jax.ShapeDtypeStruct(xp.shape, F32), jax.ShapeDtypeStruct(xs.shape, F32)]
    args = [xp, xs, *w_cur, ln_g, ln_b]
    for a, lead in to_cast:
        r, c = a.shape[-2] // n_big, a.shape[-1]
        assert a.shape[-2] % (16 * n_big) == 0 and len(lead) == a.ndim - 2
        in_specs.append(pl.BlockSpec((None,) * len(lead) + (r, c),
                                     lambda i, lead=lead: lead + (last(i), 0)))
        out_specs.append(pl.BlockSpec((r, c), lambda i: (last(i), 0)))
        out_shape.append(jax.ShapeDtypeStruct(a.shape[-2:], BF16))
        args.append(a)
    return pl.pallas_call(
        functools.partial(_ffn_kernel, alpha=alpha, n_big=n_big, n_cast=len(to_cast)),
        out_shape=out_shape,
        grid=(n_big + 2,),
        in_specs=in_specs,
        out_specs=out_specs,
        scratch_shapes=[pltpu.VMEM((FFN_ROWS, D_FF), BF16), pltpu.VMEM((FFN_ROWS, D_MODEL), F32)],
        compiler_params=_params("arbitrary"),
        name="ffn_ln",
    )(*args)


def _mixer_kernel(x_ref, win_ref, wout_ref, convw_ref, sink_ref, bias_ref, fmask_ref,
                  kinit_ref, vinit_ref, uinit_ref, g_ref, b_ref, *rest,
                  rows, steps, n_tiles, alpha, layer, n_relaid, rotate):
    src_refs, rest = rest[:n_relaid], rest[n_relaid:]
    (xo_ref, klast_ref, vlast_ref, ulast_ref), rest = rest[:4], rest[4:]
    dst_refs, (kd_scr, vd_scr, uscr, ascr, r_scr) = rest[:n_relaid], rest[n_relaid:]
    step = pl.program_id(0)
    g, b = g_ref[...], b_ref[...]
    io_rows = x_ref.shape[0]

    @pl.when(step == 0)
    def _():
        r_scr[...] = jnp.zeros(r_scr.shape, F32)

    @pl.when(step == n_tiles)
    def _():
        xo_ref[...] = _layer_norm(r_scr[rows - io_rows:, :], g, b)

    @pl.when(step < n_tiles)
    def _():
        _mixer_tile(x_ref, win_ref, wout_ref, convw_ref, sink_ref, bias_ref, fmask_ref,
                    kinit_ref, vinit_ref, uinit_ref, g, b, xo_ref, klast_ref, vlast_ref,
                    ulast_ref, kd_scr, vd_scr, uscr, ascr, r_scr, t=step % steps, rows=rows,
                    alpha=alpha, layer=layer)
        for src, dst in zip(src_refs, dst_refs):
            for i in range(src.shape[0]):
                buf = src[i].reshape(D_KV, WINDOW).T
                dst[i, 0:WINDOW - rotate, :] = buf[rotate:, :]
                dst[i, WINDOW - rotate:, :] = buf[0:rotate, :]


def _mixer_tile(x_ref, win_ref, wout_ref, convw_ref, sink_ref, bias_ref, fmask_ref,
                kinit_ref, vinit_ref, uinit_ref, g, b, xo_ref, klast_ref, vlast_ref, ulast_ref,
                kd_scr, vd_scr, uscr, ascr, r_scr, *, t, rows, alpha, layer):
    lane = lax.broadcasted_iota(jnp.int32, (1, LANES), 1)
    low = lane < HEAD_DIM

    def dup_heads(a):
        sw = pltpu.roll(a, HEAD_DIM, 1)
        return (jnp.where(low, a, sw).astype(BF16), jnp.where(low, sw, a).astype(BF16))

    @pl.when(t == 0)
    def _():
        for kv, (kd, vd) in enumerate(zip(dup_heads(kinit_ref[0]), dup_heads(vinit_ref[0]))):
            kd_scr[kv, 0:BLOCK, :] = kd
            vd_scr[kv, 0:BLOCK, :] = vd
        uscr[0:SUBLANES, :] = uinit_ref[0]

    x = x_ref[...]
    whole_tile = x.shape[0] == rows
    if not whole_tile:
        x = jnp.concatenate([jnp.zeros((rows - x.shape[0], D_MODEL), F32), x], axis=0)
    xb = x.astype(BF16)
    c0 = D_ATTN + 2 * D_KV
    zq = jnp.dot(xb, win_ref[:, 0:c0], preferred_element_type=F32)
    zc = jnp.dot(xb, win_ref[:, c0:], preferred_element_type=F32)
    k = zq[:, D_ATTN:D_ATTN + D_KV]
    v = zq[:, D_ATTN + D_KV:c0]
    if whole_tile:
        part = rows // MIX_NORM_PARTS
        edges = []
        for ci in range(MIX_NORM_PARTS):
            rs = slice(ci * part, (ci + 1) * part)
            out = _layer_norm(r_scr[rs, :], g, b)
            xo_ref[rs, :] = out
            edges.append(_zero_after(_fold_rows(out)))
        k = jnp.concatenate([k[0:16] + sum(edges[0::2]), k[16:]], axis=0)
        v = jnp.concatenate([v[0:16] + sum(edges[1::2]), v[16:]], axis=0)
    u = zc[:, D_CONV:2 * D_CONV] * zc[:, 2 * D_CONV:3 * D_CONV]
    klast_ref[0] = k[rows - BLOCK:, :]
    vlast_ref[0] = v[rows - BLOCK:, :]
    ulast_ref[0] = u[rows - SUBLANES:, :]
    for kv, (kd, vd) in enumerate(zip(dup_heads(k), dup_heads(v))):
        kd_scr[kv, BLOCK:BLOCK + rows, :] = kd
        vd_scr[kv, BLOCK:BLOCK + rows, :] = vd
    uscr[SUBLANES:SUBLANES + rows, :] = u

    qs = zq[:, :D_ATTN] * Q_SCALE
    lane_q = lax.broadcasted_iota(jnp.int32, (1, D_ATTN), 1) % LANES
    q_even = jnp.where(lane_q < HEAD_DIM, qs, 0.0).astype(BF16)
    q_odd = jnp.where(lane_q < HEAD_DIM, 0.0, qs).astype(BF16)

    first = jnp.where(t == 0, fmask_ref[...], 0.0)
    for j in range(rows // BLOCK):
        r0 = j * BLOCK
        kcat = jnp.concatenate([kd_scr[0, r0:r0 + 2 * BLOCK, :], kd_scr[1, r0:r0 + 2 * BLOCK, :]],
                               axis=1)
        vcat = jnp.concatenate([vd_scr[0, r0:r0 + 2 * BLOCK, :], vd_scr[1, r0:r0 + 2 * BLOCK, :]],
                               axis=1)
        zero_half = jnp.zeros((GQA_GROUP * BLOCK, LANES), BF16)
        halves = []
        for kv in range(N_KV_HEADS):
            q4 = jnp.concatenate(
                [(q_odd if h % 2 else q_even)[r0:r0 + BLOCK, (h // 2) * LANES:(h // 2 + 1) * LANES]
                 for h in range(kv * GQA_GROUP, (kv + 1) * GQA_GROUP)], axis=0)
            halves.append(jnp.concatenate([q4, zero_half] if kv == 0 else [zero_half, q4], axis=1))
        s8 = lax.dot_general(jnp.concatenate(halves, axis=0), kcat, (((1,), (1,)), ((), ())),
                             preferred_element_type=F32)
        ps, dens = [], []
        for h in range(N_HEADS):
            s = s8[h * BLOCK:(h + 1) * BLOCK] + bias_ref[h]
            if j == 0:
                s = s + first
            sl, sr = s[:, :LANES], s[:, LANES:]
            sk = sink_ref[layer, h]
            m = jnp.maximum(jnp.max(jnp.maximum(sl, sr), axis=-1, keepdims=True), sk)
            mb = jnp.broadcast_to(m, (BLOCK, LANES))
            p16 = jnp.concatenate([jnp.exp(sl - mb), jnp.exp(sr - mb)], axis=1).astype(BF16)
            ps.append(p16)
            p32 = p16.astype(F32)
            dens.append(jnp.sum(p32[:, :LANES] + p32[:, LANES:], axis=-1, keepdims=True)
                        + jnp.exp(sk - m))
        ox = jnp.dot(jnp.concatenate(ps, axis=0), vcat, preferred_element_type=F32)
        for p in range(N_HEADS // 2):
            kv = (2 * p) // GQA_GROUP
            outs = [ox[h * BLOCK:(h + 1) * BLOCK, kv * LANES:(kv + 1) * LANES] / dens[h]
                    for h in (2 * p, 2 * p + 1)]
            ascr[r0:r0 + BLOCK, p * LANES:(p + 1) * LANES] = (
                jnp.where(low, outs[0], outs[1]).astype(BF16))

    um2 = uscr[SUBLANES - 2:SUBLANES - 2 + rows, :]
    um1 = uscr[SUBLANES - 1:SUBLANES - 1 + rows, :]
    conv = convw_ref[0:1, :] * um2 + convw_ref[1:2, :] * um1 + convw_ref[2:3, :] * u
    mix = zc[:, 0:D_CONV] * conv
    y = (jnp.dot(ascr[...], wout_ref[0:D_ATTN, :], preferred_element_type=F32)
         + jnp.dot(mix.astype(BF16), wout_ref[D_ATTN:, :], preferred_element_type=F32))
    r_scr[...] = alpha * x + y

    for kv in range(N_KV_HEADS):
        kd_scr[kv, 0:BLOCK, :] = kd_scr[kv, rows:rows + BLOCK, :]
        vd_scr[kv, 0:BLOCK, :] = vd_scr[kv, rows:rows + BLOCK, :]
    uscr[0:SUBLANES, :] = uscr[rows:rows + SUBLANES, :]


def _mixer(x, row0, n_seq, seq_len, rows, in_place, win, wout, convw, sink, bias_p,
           fmask, kinit, vinit, uinit, ln_g, ln_b, layer, alpha, name, relay=(), rotate=0):
    io_rows = min(rows, seq_len)
    assert seq_len % io_rows == 0 and rows % BLOCK == 0 and row0 % io_rows == 0
    assert rows % (16 * MIX_NORM_PARTS) == 0 and io_rows % SUBLANES == 0
    assert io_rows == rows or n_seq == 1
    steps = seq_len // io_rows
    n_tiles = n_seq * steps
    blk0 = row0 // io_rows
    x_spec = pl.BlockSpec((io_rows, D_MODEL), lambda s: (blk0 + jnp.minimum(s, n_tiles - 1), 0))
    xo_spec = pl.BlockSpec((io_rows, D_MODEL), lambda s: (blk0 + jnp.maximum(s - 1, 0), 0))
    seq_spec = lambda shape: pl.BlockSpec(
        (1,) + shape, lambda s: (jnp.minimum(s // steps, n_seq - 1), 0, 0))
    kern = functools.partial(_mixer_kernel, rows=rows, steps=steps, n_tiles=n_tiles, alpha=alpha,
                             layer=layer, n_relaid=len(relay), rotate=rotate)
    relay_in, relay_out, relay_shape = [], [], []
    for a in relay:
        per_step = a.shape[0] // n_tiles
        assert a.shape[0] % n_tiles == 0 and a.shape[1] * a.shape[2] == D_KV
        at_step = lambda s: (jnp.minimum(s, n_tiles - 1), 0, 0)
        relay_in.append(pl.BlockSpec((per_step,) + a.shape[1:], lambda s: at_step(s) + (0,)))
        relay_out.append(pl.BlockSpec((per_step, WINDOW, D_KV), at_step))
        relay_shape.append(jax.ShapeDtypeStruct((a.shape[0], WINDOW, D_KV), F32))
    return pl.pallas_call(
        kern,
        out_shape=(jax.ShapeDtypeStruct(x.shape, F32),
                   jax.ShapeDtypeStruct((n_seq, BLOCK, D_KV), F32),
                   jax.ShapeDtypeStruct((n_seq, BLOCK, D_KV), F32),
                   jax.ShapeDtypeStruct((n_seq, SUBLANES, D_CONV), F32), *relay_shape),
        grid=(n_tiles + 1,),
        in_specs=[x_spec,
                  _const_spec((D_MODEL, D_IN)),
                  _const_spec((D_MODEL, D_MODEL)),
                  _layer_spec((CONV_WIDTH, D_CONV), layer),
                  pl.BlockSpec(memory_space=pltpu.SMEM),
                  _const_spec(bias_p.shape), _const_spec(fmask.shape),
                  _const_spec(kinit.shape), _const_spec(vinit.shape), _const_spec(uinit.shape),
                  _layer_spec((1, D_MODEL), layer, 1),
                  _layer_spec((1, D_MODEL), layer, 1), *relay_in],
        out_specs=(xo_spec, seq_spec((BLOCK, D_KV)), seq_spec((BLOCK, D_KV)),
                   seq_spec((SUBLANES, D_CONV)), *relay_out),
        scratch_shapes=[pltpu.VMEM((N_KV_HEADS, BLOCK + rows, D_KV), BF16),
                        pltpu.VMEM((N_KV_HEADS, BLOCK + rows, D_KV), BF16),
                        pltpu.VMEM((SUBLANES + rows, D_CONV), F32),
                        pltpu.VMEM((rows, D_ATTN), BF16),
                        pltpu.VMEM((rows, D_MODEL), F32)],
        input_output_aliases={0: 0} if in_place else {},
        compiler_params=_params("arbitrary"),
        name=name,
    )(x, win, wout, convw, sink, bias_p, fmask, kinit, vinit, uinit, ln_g, ln_b, *relay)


def _heads_to_rows(q, n_seq):
    lane = lax.broadcasted_iota(jnp.int32, (1, LANES), 1)
    slabs = []
    for h in range(N_HEADS):
        pair = q[:, (h // 2) * LANES:(h // 2 + 1) * LANES]
        kv = h // GQA_GROUP
        data = pair if h % 2 == kv else pltpu.roll(pair, HEAD_DIM, 1)
        on_kv = (lane >= HEAD_DIM) if kv else (lane < HEAD_DIM)
        slabs.append(jnp.where(on_kv, data, 0.0))
    rows = jnp.concatenate(slabs, axis=1).reshape(q.shape[0], N_HEADS, LANES)
    return rows.reshape(n_seq, q.shape[0] // n_seq * N_HEADS, LANES)


def _rows_to_heads(o):
    n = o.shape[0] * o.shape[1] // N_HEADS
    wide = o.reshape(n, N_HEADS, LANES).reshape(n, N_HEADS * LANES)
    lane = lax.broadcasted_iota(jnp.int32, (1, LANES), 1)
    pairs = []
    for p in range(N_HEADS // 2):
        halves = []
        for e in range(2):
            h = 2 * p + e
            slab = wide[:, h * LANES:(h + 1) * LANES]
            halves.append(slab if h // GQA_GROUP == e else pltpu.roll(slab, HEAD_DIM, 1))
        pairs.append(jnp.where(lane < HEAD_DIM, halves[0], halves[1]))
    return jnp.concatenate(pairs, axis=1)


def _sample_mixer_kernel(x_ref, win_ref, wout_ref, convw_ref, state_ref, bias_ref, sink_ref,
                         g_ref, b_ref, ck_ref, cv_ref,
                         xo_ref, u_ref, ok_ref, ov_ref,
                         q_scr, kn_scr, vn_scr, o_scr, mix_scr, *, n_new, n_seq, chunk, alpha):
    i = pl.program_id(0)
    n_chunks = n_seq // chunk
    c0 = D_ATTN + 2 * D_KV

    @pl.when(i == 0)
    def _():
        z = jnp.dot(x_ref[...].astype(BF16), win_ref[...], preferred_element_type=F32)
        q_scr[...] = _heads_to_rows(z[:, :D_ATTN] * Q_SCALE, n_seq).astype(BF16)
        kn_scr[:, 0:n_new, :] = z[:, D_ATTN:D_ATTN + D_KV].reshape(n_seq, n_new, D_KV)
        vn_scr[:, 0:n_new, :] = z[:, D_ATTN + D_KV:c0].reshape(n_seq, n_new, D_KV)
        kn_scr[:, n_new:, :] = jnp.zeros((n_seq, SUBLANES - n_new, D_KV), F32)
        vn_scr[:, n_new:, :] = jnp.zeros((n_seq, SUBLANES - n_new, D_KV), F32)
        u = z[:, c0 + D_CONV:c0 + 2 * D_CONV] * z[:, c0 + 2 * D_CONV:c0 + 3 * D_CONV]
        u_ref[...] = u
        n = u.shape[0]
        tok = lax.broadcasted_iota(jnp.int32, u.shape, 0) % n_new
        st = state_ref[...]
        um2 = jnp.where(tok < 2, st, pltpu.roll(u, 2, 0))
        um1 = jnp.where(tok < 1, pltpu.roll(st, n - 1, 0), pltpu.roll(u, 1, 0))
        conv = convw_ref[0:1, :] * um2 + convw_ref[1:2, :] * um1 + convw_ref[2:3, :] * u
        mix_scr[...] = (z[:, c0:c0 + D_CONV] * conv).astype(BF16)

    @pl.when((i >= 1) & (i <= n_chunks))
    def _():
        s0 = pl.multiple_of((i - 1) * chunk, chunk)
        qb = q_scr[pl.ds(s0, chunk)]
        kn = kn_scr[pl.ds(s0, chunk)]
        vn = vn_scr[pl.ds(s0, chunk)]
        ck = ck_ref[...]
        cv = cv_ref[...]
        bias = bias_ref[...]
        sink = sink_ref[...]
        s_c = jnp.einsum("bqd,bkd->bqk", qb, ck.astype(BF16),
                         preferred_element_type=F32) + bias[None, :, :WINDOW]
        s_n = jnp.einsum("bqd,bkd->bqk", qb, kn.astype(BF16),
                         preferred_element_type=F32) + bias[None, :, WINDOW:]
        m = jnp.maximum(jnp.maximum(jnp.max(s_c, axis=-1, keepdims=True),
                                    jnp.max(s_n, axis=-1, keepdims=True)), sink[None])
        p_c = jnp.exp(s_c - m)
        p_n = jnp.exp(s_n - m)
        den = (jnp.sum(p_c, axis=-1, keepdims=True) + jnp.sum(p_n, axis=-1, keepdims=True)
               + jnp.exp(sink[None] - m))
        o = (jnp.einsum("bqk,bkd->bqd", p_c.astype(BF16), cv.astype(BF16),
                        preferred_element_type=F32)
             + jnp.einsum("bqk,bkd->bqd", p_n.astype(BF16), vn.astype(BF16),
                          preferred_element_type=F32))
        o_scr[pl.ds(s0, chunk)] = o / den
        keep = SUBLANES - n_new
        ok_ref[:, 0:keep, :] = ck_ref[:, WINDOW - SUBLANES:WINDOW - n_new, :]
        ok_ref[:, keep:SUBLANES, :] = kn[:, 0:n_new, :]
        ov_ref[:, 0:keep, :] = cv_ref[:, WINDOW - SUBLANES:WINDOW - n_new, :]
        ov_ref[:, keep:SUBLANES, :] = vn[:, 0:n_new, :]

    @pl.when(i == n_chunks + 1)
    def _():
        a = _rows_to_heads(o_scr[...]).astype(BF16)
        y = (jnp.dot(a, wout_ref[0:D_ATTN, :], preferred_element_type=F32)
             + jnp.dot(mix_scr[...], wout_ref[D_ATTN:, :], preferred_element_type=F32))
        xo_ref[...] = _layer_norm(alpha * x_ref[...] + y, g_ref[...], b_ref[...])


def _sample_mixer(xs, n_seq, n_new, win, wout, convw, state_rows, bias_s, sink_rows, ln_g, ln_b,
                  k_buf, v_buf, layer, alpha):
    assert n_seq % SAMPLE_SEQS == 0 and n_new <= SUBLANES
    n = n_seq * n_new
    qr = n_new * N_HEADS
    n_chunks = n_seq // SAMPLE_SEQS
    x_spec = pl.BlockSpec((n, D_MODEL), lambda i: (0, 0))
    at_chunk = lambda i: jnp.clip(i - 1, 0, n_chunks - 1)
    lay_spec = pl.BlockSpec((None, SAMPLE_SEQS, WINDOW, D_KV),
                            lambda i: (layer, at_chunk(i), 0, 0))
    tail_spec = pl.BlockSpec((None, SAMPLE_SEQS, SUBLANES, D_KV),
                             lambda i: (layer, at_chunk(i), WINDOW // SUBLANES - 1, 0))
    kern = functools.partial(_sample_mixer_kernel, n_new=n_new, n_seq=n_seq, chunk=SAMPLE_SEQS,
                             alpha=alpha)
    return pl.pallas_call(
        kern,
        out_shape=(jax.ShapeDtypeStruct(xs.shape, F32),
                   jax.ShapeDtypeStruct((n, D_CONV), F32),
                   jax.ShapeDtypeStruct(k_buf.shape, F32), jax.ShapeDtypeStruct(v_buf.shape, F32)),
        grid=(n_chunks + 2,),
        in_specs=[x_spec,
                  _const_spec((D_MODEL, D_IN)),
                  _const_spec((D_MODEL, D_MODEL)),
                  _layer_spec((CONV_WIDTH, D_CONV), layer),
                  _layer_spec((n, D_CONV), layer),
                  _layer_spec(bias_s.shape), _layer_spec(sink_rows.shape[1:], layer),
                  _layer_spec((1, D_MODEL), layer, 1), _layer_spec((1, D_MODEL), layer, 1),
                  lay_spec, lay_spec],
        out_specs=(x_spec, pl.BlockSpec((n, D_CONV), lambda i: (0, 0)), tail_spec, tail_spec),
        scratch_shapes=[pltpu.VMEM((n_seq, qr, LANES), BF16),
                        pltpu.VMEM((n_seq, SUBLANES, D_KV), F32),
                        pltpu.VMEM((n_seq, SUBLANES, D_KV), F32),
                        pltpu.VMEM((n_seq, qr, LANES), F32),
                        pltpu.VMEM((n, D_CONV), BF16)],
        input_output_aliases={0: 0, 9: 2, 10: 3},
        compiler_params=_params("arbitrary"),
        name="sample_mixer",
    )(xs, win, wout, convw, state_rows, bias_s, sink_rows, ln_g, ln_b, k_buf, v_buf)


def kernel(x_prompt, x_sample, cache_k, cache_v, state_conv, meta_tokens, rel_bias, w_in, conv_w,
           attn_sink, w_out, ffn_w_gate, ffn_w_up, ffn_w_down, ln_g, ln_b):
    depth = w_in.shape[0]
    alpha = float((2 * depth) ** 0.25)
    n_prompt, seq, d_model = x_prompt.shape
    n_sample, n_new, _ = x_sample.shape
    assert d_model == D_MODEL and seq % MIX_ROWS == 0
    assert n_new >= CONV_WIDTH - 1 and cache_k.shape[2] == WINDOW
    rows_p = n_prompt * seq
    rows_s = n_sample * n_new
    assert rows_s % BLOCK == 0

    xp = x_prompt.reshape(rows_p, D_MODEL)
    xs = jnp.concatenate([x_sample.reshape(rows_s, D_MODEL), meta_tokens.astype(F32)], axis=0)

    bias_p, bias_s = _bias_tables(rel_bias, n_new)
    col = jnp.arange(2 * BLOCK)[None, :]
    fmask_prompt = jnp.where(col < META_PAD, NEG_INF, 0.0).astype(F32)
    fmask_meta = jnp.where(col < BLOCK + META_PAD, NEG_INF, 0.0).astype(F32)
    zeros_kv = jnp.zeros((1, BLOCK, D_KV), F32)
    zeros_u = jnp.zeros((1, SUBLANES, D_CONV), F32)

    ffn_f32 = (ffn_w_gate, ffn_w_up, ffn_w_down)
    w_ffn = tuple(a[0, 0].astype(BF16) for a in ffn_f32)

    def ffn(xp, xs, w_cur, layer, which):
        nxt = (layer, 1) if which == 0 else (layer + 1, 0)
        to_cast = [(a, nxt) for a in ffn_f32] if nxt[0] < depth else []
        if which == 0:
            to_cast += [(w_in, (layer,)), (w_out, (layer,))]
        xp, xs, *cast = _ffn(xp, xs, w_cur, to_cast, ln_g4, ln_b4, layer, 2 * which, alpha)
        return xp, xs, tuple(cast[:3]), tuple(cast[3:])

    ln_g4 = ln_g.reshape(depth, 3, 1, D_MODEL)
    ln_b4 = ln_b.reshape(depth, 3, 1, D_MODEL)
    relay = tuple(jnp.transpose(c, (0, 1, 3, 4, 2)).reshape(-1, N_KV_HEADS, HEAD_DIM, WINDOW)
                  for c in (cache_k, cache_v))
    state_rows = jnp.pad(state_conv, ((0, 0), (0, 0), (0, n_new - (CONV_WIDTH - 1)), (0, 0))
                         ).reshape(depth, rows_s, D_CONV)
    sink_rows = jnp.tile(attn_sink, (1, n_new)).reshape(depth, n_new * N_HEADS, 1)

    kp, vp, cp, cs = [], [], [], []
    for l in range(depth):
        xp, xs, w_ffn, (win, wout) = ffn(xp, xs, w_ffn, l, 0)

        xs, k_m, v_m, u_m = _mixer(
            xs, rows_s, 1, N_META, BLOCK, True, win, wout, conv_w, attn_sink, bias_p,
            fmask_meta, zeros_kv, zeros_kv, zeros_u, ln_g4, ln_b4, l, alpha, "mixer_meta")
        xp, k_p, v_p, u_p, *relaid = _mixer(
            xp, 0, n_prompt, seq, MIX_ROWS, False, win, wout, conv_w, attn_sink, bias_p,
            fmask_prompt, k_m, v_m, u_m, ln_g4, ln_b4, l, alpha, "mixer_prompt",
            relay if l == 0 else (), n_new)
        if relaid:
            k_buf, v_buf = (a.reshape(depth, n_sample, WINDOW, D_KV) for a in relaid)

        xs, u_s, k_buf, v_buf = _sample_mixer(
            xs, n_sample, n_new, win, wout, conv_w, state_rows, bias_s, sink_rows, ln_g4, ln_b4,
            k_buf, v_buf, l, alpha)

        xp, xs, w_ffn, _ = ffn(xp, xs, w_ffn, l, 1)

        kp.append(k_p)
        vp.append(v_p)
        cp.append(u_p[:, SUBLANES - (CONV_WIDTH - 1):, :])
        cs.append(u_s.reshape(n_sample, n_new, D_CONV)[:, n_new - (CONV_WIDTH - 1):, :])

    kv_shape = (depth, -1, WINDOW, N_KV_HEADS, HEAD_DIM)
    return (xp.reshape(n_prompt, seq, D_MODEL),
            xs[:rows_s].reshape(n_sample, n_new, D_MODEL),
            jnp.stack(kp).reshape(kv_shape), jnp.stack(vp).reshape(kv_shape), jnp.stack(cp),
            k_buf.reshape(kv_shape), v_buf.reshape(kv_shape), jnp.stack(cs))
```

```python
import functools
import math

import jax
import jax.numpy as jnp
from jax import lax
from jax.experimental import pallas as pl
from jax.experimental.pallas import tpu as pltpu

F32 = jnp.float32
BF16 = jnp.bfloat16

D_MODEL = 1024
N_HEADS = 8
N_KV_HEADS = 2
HEAD_DIM = 64
GQA_GROUP = N_HEADS // N_KV_HEADS
D_ATTN = N_HEADS * HEAD_DIM
D_CONV = D_MODEL - D_ATTN
D_KV = N_KV_HEADS * HEAD_DIM
D_IN = D_ATTN + 2 * D_KV + 3 * D_CONV
D_FF = 2816
CONV_WIDTH = 3
WINDOW = 128
BLOCK = 128
N_META = 16
N_BUCKETS = 32
MAX_DISTANCE = 128
LN_EPS = 1e-5
Q_SCALE = HEAD_DIM ** -0.5

V7X_VMEM_LIMIT_BYTES = 60 * 1024 * 1024
V7X_MXU_COLUMNS = 256
SUBLANES = 8
LANES = 128

FFN_ROWS = 1024
FFN_NORM_PARTS = 8
FFN_CHUNK = V7X_MXU_COLUMNS
MIX_ROWS = 512
MIX_NORM_PARTS = 2
SAMPLE_SEQS = 32
META_PAD = BLOCK - N_META

NEG_INF = float("-inf")


def _const_spec(shape):
    nd = len(shape)
    return pl.BlockSpec(shape, lambda *_: (0,) * nd, pipeline_mode=pl.Buffered(1))


def _layer_spec(shape, *lead):
    block = (None,) * len(lead) + tuple(shape)
    idx = tuple(lead) + (0,) * len(shape)
    return pl.BlockSpec(block, lambda *_: idx, pipeline_mode=pl.Buffered(1))


def _params(*sem):
    return pltpu.CompilerParams(dimension_semantics=sem,
                                vmem_limit_bytes=V7X_VMEM_LIMIT_BYTES)


def _layer_norm(r, g, b):
    mu = jnp.mean(r, axis=-1, keepdims=True)
    rc = r - mu
    var = jnp.mean(rc * rc, axis=-1, keepdims=True)
    return rc * lax.rsqrt(var + LN_EPS) * g + b


def _t5_bucket(d):
    d = jnp.maximum(d, 0)
    max_exact = N_BUCKETS // 2
    df = jnp.maximum(d, 1).astype(F32)
    large = max_exact + (jnp.log(df / max_exact) / math.log(MAX_DISTANCE / max_exact)
                         * (N_BUCKETS - max_exact)).astype(jnp.int32)
    large = jnp.minimum(large, N_BUCKETS - 1)
    return jnp.where(d < max_exact, d, large)


def _bias_kernel(tab_ref, bp_ref, bs_ref, *, n_new):
    qi = lax.broadcasted_iota(jnp.int32, (BLOCK, 2 * BLOCK), 0)
    sj = lax.broadcasted_iota(jnp.int32, (BLOCK, 2 * BLOCK), 1)
    d = qi + BLOCK - sj
    valid = (d >= 0) & (d <= WINDOW)
    bk = _t5_bucket(d)
    for h in range(N_HEADS):
        acc = jnp.zeros(d.shape, F32)
        for b in range(N_BUCKETS):
            acc = jnp.where(bk == b, tab_ref[b, h], acc)
        bp_ref[h] = jnp.where(valid, acc, NEG_INF)
    rows, cols = bs_ref.shape
    r = lax.broadcasted_iota(jnp.int32, (rows, cols), 0)
    s = lax.broadcasted_iota(jnp.int32, (rows, cols), 1)
    s_pos = jnp.where(s < WINDOW, (s + n_new) % WINDOW, s)
    ds = r // N_HEADS + WINDOW - s_pos
    hs = r % N_HEADS
    valid_s = (ds >= 0) & (ds <= WINDOW) & (s < WINDOW + n_new)
    bks = _t5_bucket(ds)
    acc = jnp.zeros((rows, cols), F32)
    for h in range(N_HEADS):
        for b in range(N_BUCKETS):
            acc = jnp.where((bks == b) & (hs == h), tab_ref[b, h], acc)
    bs_ref[...] = jnp.where(valid_s, acc, NEG_INF)


def _bias_tables(rel_bias, n_new):
    rows = n_new * N_HEADS
    return pl.pallas_call(
        functools.partial(_bias_kernel, n_new=n_new),
        out_shape=(jax.ShapeDtypeStruct((N_HEADS, BLOCK, 2 * BLOCK), F32),
                   jax.ShapeDtypeStruct((rows, WINDOW + SUBLANES), F32)),
        in_specs=[pl.BlockSpec(memory_space=pltpu.SMEM)],
        name="bias_tables",
    )(rel_bias)


def _zero_after(v):
    u = lax.bitcast_convert_type(v, jnp.uint32)
    z = lax.shift_right_logical(lax.shift_right_logical(u, jnp.uint32(16)), jnp.uint32(16))
    return lax.bitcast_convert_type(z, F32)


def _fold_rows(a):
    a = a.reshape(a.shape[0] // 16, 16, a.shape[1]).sum(axis=0)
    return a.reshape(16, a.shape[1] // LANES, LANES).sum(axis=1)


def _ffn_pre_norm(x, wg_ref, wu_ref, wd_ref, act_ref, alpha, side_work):
    rows = x.shape[0]
    xb = x.astype(BF16)
    for ci, c in enumerate(range(0, D_FF, FFN_CHUNK)):
        gate = jnp.dot(xb, wg_ref[:, c:c + FFN_CHUNK], preferred_element_type=F32)
        up = jnp.dot(xb, wu_ref[:, c:c + FFN_CHUNK], preferred_element_type=F32)
        act = jax.nn.silu(gate) * up
        act_ref[0:rows, c:c + FFN_CHUNK] = act.astype(BF16)
        z = side_work(ci)
        if z is not None:
            act_ref[0:16, c:c + LANES] = (act[0:16, 0:LANES] + z).astype(BF16)
    y = jnp.dot(act_ref[0:rows, :], wd_ref[...], preferred_element_type=F32)
    return alpha * x + 0.5 * y


def _ffn_kernel(xp_ref, xs_ref, wg_ref, wu_ref, wd_ref, g_ref, b_ref, *rest, alpha, n_big,
                n_cast):
    f32_refs, rest = rest[:n_cast], rest[n_cast:]
    op_ref, os_ref = rest[:2]
    bf16_refs, (act_ref, r_ref) = rest[2:2 + n_cast], rest[2 + n_cast:]
    i = pl.program_id(0)
    w = (wg_ref, wu_ref, wd_ref)
    g, b = g_ref[...], b_ref[...]
    part = FFN_ROWS // FFN_NORM_PARTS

    def norm_previous_tile(ci):
        if ci >= FFN_NORM_PARTS:
            return None
        rs = slice(ci * part, (ci + 1) * part)
        out = _layer_norm(r_ref[rs, :], g, b)
        op_ref[rs, :] = out
        return _zero_after(_fold_rows(out))

    def norm_and_convert(ci):
        if ci == FFN_NORM_PARTS:
            for src, dst in zip(f32_refs, bf16_refs):
                dst[...] = src[...].astype(BF16)
        return norm_previous_tile(ci)

    @pl.when(i == 0)
    def _():
        r_ref[...] = jnp.zeros(r_ref.shape, F32)

    @pl.when(i < n_big)
    def _():
        r_ref[...] = _ffn_pre_norm(xp_ref[...], *w, act_ref, alpha, norm_and_convert)

    @pl.when(i == n_big)
    def _():
        os_ref[...] = _ffn_pre_norm(xs_ref[...], *w, act_ref, alpha, norm_previous_tile)

    @pl.when(i == n_big + 1)
    def _():
        os_ref[...] = _layer_norm(os_ref[...], g, b)


def _ffn(xp, xs, w_cur, to_cast, ln_g, ln_b, layer, ln_idx, alpha):
    assert xp.shape[0] % FFN_ROWS == 0 and D_FF % FFN_CHUNK == 0
    assert FFN_NORM_PARTS < D_FF // FFN_CHUNK and FFN_ROWS % (16 * FFN_NORM_PARTS) == 0
    n_big = xp.shape[0] // FFN_ROWS
    rows_s = xs.shape[0]
    assert rows_s <= FFN_ROWS
    last = lambda i: jnp.minimum(i, n_big - 1)
    in_spec = pl.BlockSpec((FFN_ROWS, D_MODEL), lambda i: (last(i), 0))
    out_spec = pl.BlockSpec((FFN_ROWS, D_MODEL), lambda i: (jnp.clip(i - 1, 0, n_big - 1), 0))
    small_in = pl.BlockSpec((rows_s, D_MODEL), lambda i: (0, 0), pipeline_mode=pl.Buffered(1))
    small_out = pl.BlockSpec((rows_s, D_MODEL), lambda i: (0, 0), pipeline_mode=pl.Buffered(1))
    in_specs = [in_spec, small_in] + [_const_spec(a.shape) for a in w_cur] + [
        _layer_spec((1, D_MODEL), layer, ln_idx), _layer_spec((1, D_MODEL), layer, ln_idx)]
    out_specs = [out_spec, small_out]
    out_shape = [jax.ShapeDtypeStruct(xp.shape, F32), jax.ShapeDtypeStruct(xs.shape, F32)]
    args = [xp, xs, *w_cur, ln_g, ln_b]
    for a, lead in to_cast:
        r, c = a.shape[-2] // n_big, a.shape[-1]
        assert a.shape[-2] % (16 * n_big) == 0 and len(lead) == a.ndim - 2
        in_specs.append(pl.BlockSpec((None,) * len(lead) + (r, c),
                                     lambda i, lead=lead: lead + (last(i), 0)))
        out_specs.append(pl.BlockSpec((r, c), lambda i: (last(i), 0)))
        out_shape.append(jax.ShapeDtypeStruct(a.shape[-2:], BF16))
        args.append(a)
    return pl.pallas_call(
        functools.partial(_ffn_kernel, alpha=alpha, n_big=n_big, n_cast=len(to_cast)),
        out_shape=out_shape,
        grid=(n_big + 2,),
        in_specs=in_specs,
        out_specs=out_specs,
        scratch_shapes=[pltpu.VMEM((FFN_ROWS, D_FF), BF16), pltpu.VMEM((FFN_ROWS, D_MODEL), F32)],
        compiler_params=_params("arbitrary"),
        name="ffn_ln",
    )(*args)


def _mixer_kernel(x_ref, win_ref, wout_ref, convw_ref, sink_ref, bias_ref, fmask_ref,
                  kinit_ref, vinit_ref, uinit_ref, g_ref, b_ref, *rest,
                  rows, steps, n_tiles, alpha, layer, n_relaid, rotate):
    src_refs, rest = rest[:n_relaid], rest[n_relaid:]
    (xo_ref, klast_ref, vlast_ref, ulast_ref), rest = rest[:4], rest[4:]
    dst_refs, (kd_scr, vd_scr, uscr, ascr, r_scr) = rest[:n_relaid], rest[n_relaid:]
    step = pl.program_id(0)
    g, b = g_ref[...], b_ref[...]
    io_rows = x_ref.shape[0]

    @pl.when(step == 0)
    def _():
        r_scr[...] = jnp.zeros(r_scr.shape, F32)

    @pl.when(step == n_tiles)
    def _():
        xo_ref[...] = _layer_norm(r_scr[rows - io_rows:, :], g, b)

    @pl.when(step < n_tiles)
    def _():
        _mixer_tile(x_ref, win_ref, wout_ref, convw_ref, sink_ref, bias_ref, fmask_ref,
                    kinit_ref, vinit_ref, uinit_ref, g, b, xo_ref, klast_ref, vlast_ref,
                    ulast_ref, kd_scr, vd_scr, uscr, ascr, r_scr, t=step % steps, rows=rows,
                    alpha=alpha, layer=layer)
        for src, dst in zip(src_refs, dst_refs):
            for i in range(src.shape[0]):
                buf = src[i].reshape(D_KV, WINDOW).T
                dst[i, 0:WINDOW - rotate, :] = buf[rotate:, :]
                dst[i, WINDOW - rotate:, :] = buf[0:rotate, :]


def _mixer_tile(x_ref, win_ref, wout_ref, convw_ref, sink_ref, bias_ref, fmask_ref,
                kinit_ref, vinit_ref, uinit_ref, g, b, xo_ref, klast_ref, vlast_ref, ulast_ref,
                kd_scr, vd_scr, uscr, ascr, r_scr, *, t, rows, alpha, layer):
    lane = lax.broadcasted_iota(jnp.int32, (1, LANES), 1)
    low = lane < HEAD_DIM

    def dup_heads(a):
        sw = pltpu.roll(a, HEAD_DIM, 1)
        return (jnp.where(low, a, sw).astype(BF16), jnp.where(low, sw, a).astype(BF16))

    @pl.when(t == 0)
    def _():
        for kv, (kd, vd) in enumerate(zip(dup_heads(kinit_ref[0]), dup_heads(vinit_ref[0]))):
            kd_scr[kv, 0:BLOCK, :] = kd
            vd_scr[kv, 0:BLOCK, :] = vd
        uscr[0:SUBLANES, :] = uinit_ref[0]

    x = x_ref[...]
    whole_tile = x.shape[0] == rows
    if not whole_tile:
        x = jnp.concatenate([jnp.zeros((rows - x.shape[0], D_MODEL), F32), x], axis=0)
    xb = x.astype(BF16)
    c0 = D_ATTN + 2 * D_KV
    zq = jnp.dot(xb, win_ref[:, 0:c0], preferred_element_type=F32)
    zc = jnp.dot(xb, win_ref[:, c0:], preferred_element_type=F32)
    k = zq[:, D_ATTN:D_ATTN + D_KV]
    v = zq[:, D_ATTN + D_KV:c0]
    if whole_tile:
        part = rows // MIX_NORM_PARTS
        edges = []
        for ci in range(MIX_NORM_PARTS):
            rs = slice(ci * part, (ci + 1) * part)
            out = _layer_norm(r_scr[rs, :], g, b)
            xo_ref[rs, :] = out
            edges.append(_zero_after(_fold_rows(out)))
        k = jnp.concatenate([k[0:16] + sum(edges[0::2]), k[16:]], axis=0)
        v = jnp.concatenate([v[0:16] + sum(edges[1::2]), v[16:]], axis=0)
    u = zc[:, D_CONV:2 * D_CONV] * zc[:, 2 * D_CONV:3 * D_CONV]
    klast_ref[0] = k[rows - BLOCK:, :]
    vlast_ref[0] = v[rows - BLOCK:, :]
    ulast_ref[0] = u[rows - SUBLANES:, :]
    for kv, (kd, vd) in enumerate(zip(dup_heads(k), dup_heads(v))):
        kd_scr[kv, BLOCK:BLOCK + rows, :] = kd
        vd_scr[kv, BLOCK:BLOCK + rows, :] = vd
    uscr[SUBLANES:SUBLANES + rows, :] = u

    qs = zq[:, :D_ATTN] * Q_SCALE
    lane_q = lax.broadcasted_iota(jnp.int32, (1, D_ATTN), 1) % LANES
    q_even = jnp.where(lane_q < HEAD_DIM, qs, 0.0).astype(BF16)
    q_odd = jnp.where(lane_q < HEAD_DIM, 0.0, qs).astype(BF16)

    first = jnp.where(t == 0, fmask_ref[...], 0.0)
    for j in range(rows // BLOCK):
        r0 = j * BLOCK
        kcat = jnp.concatenate([kd_scr[0, r0:r0 + 2 * BLOCK, :], kd_scr[1, r0:r0 + 2 * BLOCK, :]],
                               axis=1)
        vcat = jnp.concatenate([vd_scr[0, r0:r0 + 2 * BLOCK, :], vd_scr[1, r0:r0 + 2 * BLOCK, :]],
                               axis=1)
        zero_half = jnp.zeros((GQA_GROUP * BLOCK, LANES), BF16)
        halves = []
        for kv in range(N_KV_HEADS):
            q4 = jnp.concatenate(
                [(q_odd if h % 2 else q_even)[r0:r0 + BLOCK, (h // 2) * LANES:(h // 2 + 1) * LANES]
                 for h in range(kv * GQA_GROUP, (kv + 1) * GQA_GROUP)], axis=0)
            halves.append(jnp.concatenate([q4, zero_half] if kv == 0 else [zero_half, q4], axis=1))
        s8 = lax.dot_general(jnp.concatenate(halves, axis=0), kcat, (((1,), (1,)), ((), ())),
                             preferred_element_type=F32)
        ps, dens = [], []
        for h in range(N_HEADS):
            s = s8[h * BLOCK:(h + 1) * BLOCK] + bias_ref[h]
            if j == 0:
                s = s + first
            sl, sr = s[:, :LANES], s[:, LANES:]
            sk = sink_ref[layer, h]
            m = jnp.maximum(jnp.max(jnp.maximum(sl, sr), axis=-1, keepdims=True), sk)
            mb = jnp.broadcast_to(m, (BLOCK, LANES))
            pl_, pr_ = jnp.exp(sl - mb), jnp.exp(sr - mb)
            ps.append(jnp.concatenate([pl_, pr_], axis=1).astype(BF16))
            dens.append(jnp.sum(pl_ + pr_, axis=-1, keepdims=True) + jnp.exp(sk - m))
        ox = jnp.dot(jnp.concatenate(ps, axis=0), vcat, preferred_element_type=F32)
        for p in range(N_HEADS // 2):
            kv = (2 * p) // GQA_GROUP
            outs = [ox[h * BLOCK:(h + 1) * BLOCK, kv * LANES:(kv + 1) * LANES] / dens[h]
                    for h in (2 * p, 2 * p + 1)]
            ascr[r0:r0 + BLOCK, p * LANES:(p + 1) * LANES] = (
                jnp.where(low, outs[0], outs[1]).astype(BF16))

    um2 = uscr[SUBLANES - 2:SUBLANES - 2 + rows, :]
    um1 = uscr[SUBLANES - 1:SUBLANES - 1 + rows, :]
    conv = convw_ref[0:1, :] * um2 + convw_ref[1:2, :] * um1 + convw_ref[2:3, :] * u
    mix = zc[:, 0:D_CONV] * conv
    y = (jnp.dot(ascr[...], wout_ref[0:D_ATTN, :], preferred_element_type=F32)
         + jnp.dot(mix.astype(BF16), wout_ref[D_ATTN:, :], preferred_element_type=F32))
    r_scr[...] = alpha * x + y

    for kv in range(N_KV_HEADS):
        kd_scr[kv, 0:BLOCK, :] = kd_scr[kv, rows:rows + BLOCK, :]
        vd_scr[kv, 0:BLOCK, :] = vd_scr[kv, rows:rows + BLOCK, :]
    uscr[0:SUBLANES, :] = uscr[rows:rows + SUBLANES, :]


def _mixer(x, row0, n_seq, seq_len, rows, in_place, win, wout, convw, sink, bias_p,
           fmask, kinit, vinit, uinit, ln_g, ln_b, layer, alpha, name, relay=(), rotate=0):
    io_rows = min(rows, seq_len)
    assert seq_len % io_rows == 0 and rows % BLOCK == 0 and row0 % io_rows == 0
    assert rows % (16 * MIX_NORM_PARTS) == 0 and io_rows % SUBLANES == 0
    assert io_rows == rows or n_seq == 1
    steps = seq_len // io_rows
    n_tiles = n_seq * steps
    blk0 = row0 // io_rows
    x_spec = pl.BlockSpec((io_rows, D_MODEL), lambda s: (blk0 + jnp.minimum(s, n_tiles - 1), 0))
    xo_spec = pl.BlockSpec((io_rows, D_MODEL), lambda s: (blk0 + jnp.maximum(s - 1, 0), 0))
    seq_spec = lambda shape: pl.BlockSpec(
        (1,) + shape, lambda s: (jnp.minimum(s // steps, n_seq - 1), 0, 0))
    kern = functools.partial(_mixer_kernel, rows=rows, steps=steps, n_tiles=n_tiles, alpha=alpha,
                             layer=layer, n_relaid=len(relay), rotate=rotate)
    relay_in, relay_out, relay_shape = [], [], []
    for a in relay:
        per_step = a.shape[0] // n_tiles
        assert a.shape[0] % n_tiles == 0 and a.shape[1] * a.shape[2] == D_KV
        at_step = lambda s: (jnp.minimum(s, n_tiles - 1), 0, 0)
        relay_in.append(pl.BlockSpec((per_step,) + a.shape[1:], lambda s: at_step(s) + (0,)))
        relay_out.append(pl.BlockSpec((per_step, WINDOW, D_KV), at_step))
        relay_shape.append(jax.ShapeDtypeStruct((a.shape[0], WINDOW, D_KV), F32))
    return pl.pallas_call(
        kern,
        out_shape=(jax.ShapeDtypeStruct(x.shape, F32),
                   jax.ShapeDtypeStruct((n_seq, BLOCK, D_KV), F32),
                   jax.ShapeDtypeStruct((n_seq, BLOCK, D_KV), F32),
                   jax.ShapeDtypeStruct((n_seq, SUBLANES, D_CONV), F32), *relay_shape),
        grid=(n_tiles + 1,),
        in_specs=[x_spec,
                  _const_spec((D_MODEL, D_IN)),
                  _const_spec((D_MODEL, D_MODEL)),
                  _layer_spec((CONV_WIDTH, D_CONV), layer),
                  pl.BlockSpec(memory_space=pltpu.SMEM),
                  _const_spec(bias_p.shape), _const_spec(fmask.shape),
                  _const_spec(kinit.shape), _const_spec(vinit.shape), _const_spec(uinit.shape),
                  _layer_spec((1, D_MODEL), layer, 1),
                  _layer_spec((1, D_MODEL), layer, 1), *relay_in],
        out_specs=(xo_spec, seq_spec((BLOCK, D_KV)), seq_spec((BLOCK, D_KV)),
                   seq_spec((SUBLANES, D_CONV)), *relay_out),
        scratch_shapes=[pltpu.VMEM((N_KV_HEADS, BLOCK + rows, D_KV), BF16),
                        pltpu.VMEM((N_KV_HEADS, BLOCK + rows, D_KV), BF16),
                        pltpu.VMEM((SUBLANES + rows, D_CONV), F32),
                        pltpu.VMEM((rows, D_ATTN), BF16),
                        pltpu.VMEM((rows, D_MODEL), F32)],
        input_output_aliases={0: 0} if in_place else {},
        compiler_params=_params("arbitrary"),
        name=name,
    )(x, win, wout, convw, sink, bias_p, fmask, kinit, vinit, uinit, ln_g, ln_b, *relay)


def _heads_to_rows(q, n_seq):
    lane = lax.broadcasted_iota(jnp.int32, (1, LANES), 1)
    slabs = []
    for h in range(N_HEADS):
        pair = q[:, (h // 2) * LANES:(h // 2 + 1) * LANES]
        kv = h // GQA_GROUP
        data = pair if h % 2 == kv else pltpu.roll(pair, HEAD_DIM, 1)
        on_kv = (lane >= HEAD_DIM) if kv else (lane < HEAD_DIM)
        slabs.append(jnp.where(on_kv, data, 0.0))
    rows = jnp.concatenate(slabs, axis=1).reshape(q.shape[0], N_HEADS, LANES)
    return rows.reshape(n_seq, q.shape[0] // n_seq * N_HEADS, LANES)


def _rows_to_heads(o):
    n = o.shape[0] * o.shape[1] // N_HEADS
    wide = o.reshape(n, N_HEADS, LANES).reshape(n, N_HEADS * LANES)
    lane = lax.broadcasted_iota(jnp.int32, (1, LANES), 1)
    pairs = []
    for p in range(N_HEADS // 2):
        halves = []
        for e in range(2):
            h = 2 * p + e
            slab = wide[:, h * LANES:(h + 1) * LANES]
            halves.append(slab if h // GQA_GROUP == e else pltpu.roll(slab, HEAD_DIM, 1))
        pairs.append(jnp.where(lane < HEAD_DIM, halves[0], halves[1]))
    return jnp.concatenate(pairs, axis=1)


def _sample_mixer_kernel(x_ref, win_ref, wout_ref, convw_ref, state_ref, bias_ref, sink_ref,
                         g_ref, b_ref, ck_ref, cv_ref,
                         xo_ref, u_ref, ok_ref, ov_ref,
                         q_scr, kn_scr, vn_scr, o_scr, mix_scr, *, n_new, n_seq, chunk, alpha):
    i = pl.program_id(0)
    n_chunks = n_seq // chunk
    c0 = D_ATTN + 2 * D_KV

    @pl.when(i == 0)
    def _():
        z = jnp.dot(x_ref[...].astype(BF16), win_ref[...], preferred_element_type=F32)
        q_scr[...] = _heads_to_rows(z[:, :D_ATTN] * Q_SCALE, n_seq).astype(BF16)
        kn_scr[:, 0:n_new, :] = z[:, D_ATTN:D_ATTN + D_KV].reshape(n_seq, n_new, D_KV)
        vn_scr[:, 0:n_new, :] = z[:, D_ATTN + D_KV:c0].reshape(n_seq, n_new, D_KV)
        kn_scr[:, n_new:, :] = jnp.zeros((n_seq, SUBLANES - n_new, D_KV), F32)
        vn_scr[:, n_new:, :] = jnp.zeros((n_seq, SUBLANES - n_new, D_KV), F32)
        u = z[:, c0 + D_CONV:c0 + 2 * D_CONV] * z[:, c0 + 2 * D_CONV:c0 + 3 * D_CONV]
        u_ref[...] = u
        n = u.shape[0]
        tok = lax.broadcasted_iota(jnp.int32, u.shape, 0) % n_new
        st = state_ref[...]
        um2 = jnp.where(tok < 2, st, pltpu.roll(u, 2, 0))
        um1 = jnp.where(tok < 1, pltpu.roll(st, n - 1, 0), pltpu.roll(u, 1, 0))
        conv = convw_ref[0:1, :] * um2 + convw_ref[1:2, :] * um1 + convw_ref[2:3, :] * u
        mix_scr[...] = (z[:, c0:c0 + D_CONV] * conv).astype(BF16)

    @pl.when((i >= 1) & (i <= n_chunks))
    def _():
        s0 = pl.multiple_of((i - 1) * chunk, chunk)
        qb = q_scr[pl.ds(s0, chunk)]
        kn = kn_scr[pl.ds(s0, chunk)]
        vn = vn_scr[pl.ds(s0, chunk)]
        ck = ck_ref[...]
        cv = cv_ref[...]
        bias = bias_ref[...]
        sink = sink_ref[...]
        s_c = jnp.einsum("bqd,bkd->bqk", qb, ck.astype(BF16),
                         preferred_element_type=F32) + bias[None, :, :WINDOW]
        s_n = jnp.einsum("bqd,bkd->bqk", qb, kn.astype(BF16),
                         preferred_element_type=F32) + bias[None, :, WINDOW:]
        m = jnp.maximum(jnp.maximum(jnp.max(s_c, axis=-1, keepdims=True),
                                    jnp.max(s_n, axis=-1, keepdims=True)), sink[None])
        p_c = jnp.exp(s_c - m)
        p_n = jnp.exp(s_n - m)
        den = (jnp.sum(p_c, axis=-1, keepdims=True) + jnp.sum(p_n, axis=-1, keepdims=True)
               + jnp.exp(sink[None] - m))
        o = (jnp.einsum("bqk,bkd->bqd", p_c.astype(BF16), cv.astype(BF16),
                        preferred_element_type=F32)
             + jnp.einsum("bqk,bkd->bqd", p_n.astype(BF16), vn.astype(BF16),
                          preferred_element_type=F32))
        o_scr[pl.ds(s0, chunk)] = o / den
        keep = SUBLANES - n_new
        ok_ref[:, 0:keep, :] = ck_ref[:, WINDOW - SUBLANES:WINDOW - n_new, :]
        ok_ref[:, keep:SUBLANES, :] = kn[:, 0:n_new, :]
        ov_ref[:, 0:keep, :] = cv_ref[:, WINDOW - SUBLANES:WINDOW - n_new, :]
        ov_ref[:, keep:SUBLANES, :] = vn[:, 0:n_new, :]

    @pl.when(i == n_chunks + 1)
    def _():
        a = _rows_to_heads(o_scr[...]).astype(BF16)
        y = (jnp.dot(a, wout_ref[0:D_ATTN, :], preferred_element_type=F32)
             + jnp.dot(mix_scr[...], wout_ref[D_ATTN:, :], preferred_element_type=F32))
        xo_ref[...] = _layer_norm(alpha * x_ref[...] + y, g_ref[...], b_ref[...])


def _sample_mixer(xs, n_seq, n_new, win, wout, convw, state_rows, bias_s, sink_rows, ln_g, ln_b,
                  k_buf, v_buf, layer, alpha):
    assert n_seq % SAMPLE_SEQS == 0 and n_new <= SUBLANES
    n = n_seq * n_new
    qr = n_new * N_HEADS
    n_chunks = n_seq // SAMPLE_SEQS
    x_spec = pl.BlockSpec((n, D_MODEL), lambda i: (0, 0))
    at_chunk = lambda i: jnp.clip(i - 1, 0, n_chunks - 1)
    lay_spec = pl.BlockSpec((None, SAMPLE_SEQS, WINDOW, D_KV),
                            lambda i: (layer, at_chunk(i), 0, 0))
    tail_spec = pl.BlockSpec((None, SAMPLE_SEQS, SUBLANES, D_KV),
                             lambda i: (layer, at_chunk(i), WINDOW // SUBLANES - 1, 0))
    kern = functools.partial(_sample_mixer_kernel, n_new=n_new, n_seq=n_seq, chunk=SAMPLE_SEQS,
                             alpha=alpha)
    return pl.pallas_call(
        kern,
        out_shape=(jax.ShapeDtypeStruct(xs.shape, F32),
                   jax.ShapeDtypeStruct((n, D_CONV), F32),
                   jax.ShapeDtypeStruct(k_buf.shape, F32), jax.ShapeDtypeStruct(v_buf.shape, F32)),
        grid=(n_chunks + 2,),
        in_specs=[x_spec,
                  _const_spec((D_MODEL, D_IN)),
                  _const_spec((D_MODEL, D_MODEL)),
                  _layer_spec((CONV_WIDTH, D_CONV), layer),
                  _layer_spec((n, D_CONV), layer),
                  _layer_spec(bias_s.shape), _layer_spec(sink_rows.shape[1:], layer),
                  _layer_spec((1, D_MODEL), layer, 1), _layer_spec((1, D_MODEL), layer, 1),
                  lay_spec, lay_spec],
        out_specs=(x_spec, pl.BlockSpec((n, D_CONV), lambda i: (0, 0)), tail_spec, tail_spec),
        scratch_shapes=[pltpu.VMEM((n_seq, qr, LANES), BF16),
                        pltpu.VMEM((n_seq, SUBLANES, D_KV), F32),
                        pltpu.VMEM((n_seq, SUBLANES, D_KV), F32),
                        pltpu.VMEM((n_seq, qr, LANES), F32),
                        pltpu.VMEM((n, D_CONV), BF16)],
        input_output_aliases={0: 0, 9: 2, 10: 3},
        compiler_params=_params("arbitrary"),
        name="sample_mixer",
    )(xs, win, wout, convw, state_rows, bias_s, sink_rows, ln_g, ln_b, k_buf, v_buf)


def kernel(x_prompt, x_sample, cache_k, cache_v, state_conv, meta_tokens, rel_bias, w_in, conv_w,
           attn_sink, w_out, ffn_w_gate, ffn_w_up, ffn_w_down, ln_g, ln_b):
    depth = w_in.shape[0]
    alpha = float((2 * depth) ** 0.25)
    n_prompt, seq, d_model = x_prompt.shape
    n_sample, n_new, _ = x_sample.shape
    assert d_model == D_MODEL and seq % MIX_ROWS == 0
    assert n_new >= CONV_WIDTH - 1 and cache_k.shape[2] == WINDOW
    rows_p = n_prompt * seq
    rows_s = n_sample * n_new
    assert rows_s % BLOCK == 0

    xp = x_prompt.reshape(rows_p, D_MODEL)
    xs = jnp.concatenate([x_sample.reshape(rows_s, D_MODEL), meta_tokens.astype(F32)], axis=0)

    bias_p, bias_s = _bias_tables(rel_bias, n_new)
    col = jnp.arange(2 * BLOCK)[None, :]
    fmask_prompt = jnp.where(col < META_PAD, NEG_INF, 0.0).astype(F32)
    fmask_meta = jnp.where(col < BLOCK + META_PAD, NEG_INF, 0.0).astype(F32)
    zeros_kv = jnp.zeros((1, BLOCK, D_KV), F32)
    zeros_u = jnp.zeros((1, SUBLANES, D_CONV), F32)

    ffn_f32 = (ffn_w_gate, ffn_w_up, ffn_w_down)
    w_ffn = tuple(a[0, 0].astype(BF16) for a in ffn_f32)

    def ffn(xp, xs, w_cur, layer, which):
        nxt = (layer, 1) if which == 0 else (layer + 1, 0)
        to_cast = [(a, nxt) for a in ffn_f32] if nxt[0] < depth else []
        if which == 0:
            to_cast += [(w_in, (layer,)), (w_out, (layer,))]
        xp, xs, *cast = _ffn(xp, xs, w_cur, to_cast, ln_g4, ln_b4, layer, 2 * which, alpha)
        return xp, xs, tuple(cast[:3]), tuple(cast[3:])

    ln_g4 = ln_g.reshape(depth, 3, 1, D_MODEL)
    ln_b4 = ln_b.reshape(depth, 3, 1, D_MODEL)
    relay = tuple(jnp.transpose(c, (0, 1, 3, 4, 2)).reshape(-1, N_KV_HEADS, HEAD_DIM, WINDOW)
                  for c in (cache_k, cache_v))
    state_rows = jnp.pad(state_conv, ((0, 0), (0, 0), (0, n_new - (CONV_WIDTH - 1)), (0, 0))
                         ).reshape(depth, rows_s, D_CONV)
    sink_rows = jnp.tile(attn_sink, (1, n_new)).reshape(depth, n_new * N_HEADS, 1)

    kp, vp, cp, cs = [], [], [], []
    for l in range(depth):
        xp, xs, w_ffn, (win, wout) = ffn(xp, xs, w_ffn, l, 0)

        xs, k_m, v_m, u_m = _mixer(
            xs, rows_s, 1, N_META, BLOCK, True, win, wout, conv_w, attn_sink, bias_p,
            fmask_meta, zeros_kv, zeros_kv, zeros_u, ln_g4, ln_b4, l, alpha, "mixer_meta")
        xp, k_p, v_p, u_p, *relaid = _mixer(
            xp, 0, n_prompt, seq, MIX_ROWS, False, win, wout, conv_w, attn_sink, bias_p,
            fmask_prompt, k_m, v_m, u_m, ln_g4, ln_b4, l, alpha, "mixer_prompt",
            relay if l == 0 else (), n_new)
        if relaid:
            k_buf, v_buf = (a.reshape(depth, n_sample, WINDOW, D_KV) for a in relaid)

        xs, u_s, k_buf, v_buf = _sample_mixer(
            xs, n_sample, n_new, win, wout, conv_w, state_rows, bias_s, sink_rows, ln_g4, ln_b4,
            k_buf, v_buf, l, alpha)

        xp, xs, w_ffn, _ = ffn(xp, xs, w_ffn, l, 1)

        kp.append(k_p)
        vp.append(v_p)
        cp.append(u_p[:, SUBLANES - (CONV_WIDTH - 1):, :])
        cs.append(u_s.reshape(n_sample, n_new, D_CONV)[:, n_new - (CONV_WIDTH - 1):, :])

    kv_shape = (depth, -1, WINDOW, N_KV_HEADS, HEAD_DIM)
    return (xp.reshape(n_prompt, seq, D_MODEL),
            xs[:rows_s].reshape(n_sample, n_new, D_MODEL),
            jnp.stack(kp).reshape(kv_shape), jnp.stack(vp).reshape(kv_shape), jnp.stack(cp),
            k_buf.reshape(kv_shape), v_buf.reshape(kv_shape), jnp.stack(cs))
```
